```python
import math
import jax, jax.numpy as jnp
from jax import lax
import numpy as np

D_MODEL = 2048
BATCH = 4
SEQ = 2048
DEPTH = 1

DA_HEADS = 8
DA_HEAD_DIM = 64
DA_V_DIM = 2 * DA_HEAD_DIM
DA_WIDTH = DA_HEADS * DA_V_DIM

MLA_HEADS = 8
MLA_Q_RANK = 512
MLA_KV_RANK = 256
MLA_NOPE = 64
MLA_ROPE = 32
MLA_V = 128
MLA_WIDTH = MLA_HEADS * MLA_V

IN_SIZES = (
    DA_HEADS * 2 * DA_HEAD_DIM,
    DA_HEADS * 2 * DA_HEAD_DIM,
    DA_HEADS * DA_V_DIM,
    MLA_Q_RANK,
    MLA_KV_RANK,
    MLA_ROPE,
    D_MODEL,
    D_MODEL,
)
IN_WIDTH = sum(IN_SIZES)

N_GROUPS = 4
EXPERTS_PER_GROUP = 8
N_EXPERTS = N_GROUPS * EXPERTS_PER_GROUP
TOP_K_IN_GROUP = 2
EXPERT_FF = 512
EXPERT_BLOCK = 128

ROPE_THETA = 10000.0
Q_BLOCK = 128
NORM_EPS = 1e-6
NEG_INF = -1e30

kernel_name = "hybrid_diffattn_mla_hiermoe_block"


def rms_norm(x, g):
    xf = x.astype(jnp.float32)
    y = xf * lax.rsqrt(jnp.mean(xf * xf, axis=-1, keepdims=True) + NORM_EPS)
    return (y * g.astype(jnp.float32)).astype(x.dtype)


def rope(x, positions):
    d = x.shape[-1]
    inv_freq = ROPE_THETA ** (-jnp.arange(0, d, 2, dtype=jnp.float32) / d)
    ang = positions.astype(jnp.float32)[..., None] * inv_freq
    cos = jnp.cos(ang)[:, :, None, :]
    sin = jnp.sin(ang)[:, :, None, :]
    x1, x2 = jnp.split(x.astype(jnp.float32), 2, axis=-1)
    return jnp.concatenate([x1 * cos - x2 * sin, x1 * sin + x2 * cos], axis=-1).astype(x.dtype)


def causal_mask(i, seq_len):
    q_idx = i * Q_BLOCK + jnp.arange(Q_BLOCK)
    k_idx = jnp.arange(seq_len)
    return k_idx[None, :] <= q_idx[:, None]


def sweep_query_blocks(block_fn, seq_len):
    out = lax.map(block_fn, jnp.arange(seq_len // Q_BLOCK))
    out = jnp.moveaxis(out, 0, 1)
    return out.reshape(out.shape[0], seq_len, *out.shape[3:])


def diff_attention(q, k, v, lam):
    seq_len = q.shape[1]
    scale = DA_HEAD_DIM ** -0.5

    def block(i):
        q_blk = lax.dynamic_slice_in_dim(q, i * Q_BLOCK, Q_BLOCK, axis=1)
        s = jnp.einsum('bqhcd,bkhcd->bhcqk', q_blk, k,
                       preferred_element_type=jnp.float32) * scale
        s = jnp.where(causal_mask(i, seq_len), s, NEG_INF)
        p = jax.nn.softmax(s, axis=-1)
        p = p[:, :, 0] - lam * p[:, :, 1]
        return jnp.einsum('bhqk,bkhe->bqhe', p.astype(v.dtype), v)

    return sweep_query_blocks(block, seq_len)


def softmax_attention(q, k, v, scale):
    seq_len = q.shape[1]

    def block(i):
        q_blk = lax.dynamic_slice_in_dim(q, i * Q_BLOCK, Q_BLOCK, axis=1)
        s = jnp.einsum('bqhd,bkhd->bhqk', q_blk, k,
                       preferred_element_type=jnp.float32) * scale
        s = jnp.where(causal_mask(i, seq_len), s, NEG_INF)
        p = jax.nn.softmax(s, axis=-1)
        return jnp.einsum('bhqk,bkhe->bqhe', p.astype(v.dtype), v)

    return sweep_query_blocks(block, seq_len)


def hierarchical_route(h, w_group, b_group, w_router, b_router):
    g_logits = jnp.dot(h, w_group, preferred_element_type=jnp.float32) + b_group.astype(jnp.float32)
    p_group = jax.nn.softmax(g_logits, axis=-1)
    g_sel = jnp.argmax(p_group, axis=-1)
    p_g = jnp.take_along_axis(p_group, g_sel[:, None], axis=-1)
    e_logits = (jnp.dot(h, w_router, preferred_element_type=jnp.float32)
                + b_router.astype(jnp.float32)).reshape(-1, N_GROUPS, EXPERTS_PER_GROUP)
    e_logits = jnp.take_along_axis(e_logits, g_sel[:, None, None], axis=1)[:, 0]
    p_e = jax.nn.softmax(e_logits, axis=-1)
    top_w, top_i = lax.top_k(p_e, TOP_K_IN_GROUP)
    top_w = top_w / jnp.sum(top_w, axis=-1, keepdims=True)
    expert_id = g_sel[:, None] * EXPERTS_PER_GROUP + top_i
    return expert_id, p_g * top_w


def routed_experts(h, expert_id, weight, w_gate, w_up, w_down):
    T, D = h.shape
    n_assign = expert_id.size
    flat_e = expert_id.reshape(-1)
    flat_tok = jnp.repeat(jnp.arange(T, dtype=jnp.int32), TOP_K_IN_GROUP)
    flat_w = weight.reshape(-1)
    order = jnp.argsort(flat_e)
    sorted_e = flat_e[order]
    counts = jnp.bincount(flat_e, length=N_EXPERTS)
    starts = jnp.cumsum(counts) - counts
    padded = (counts + EXPERT_BLOCK - 1) // EXPERT_BLOCK * EXPERT_BLOCK
    padded_ends = jnp.cumsum(padded)
    padded_starts = padded_ends - padded
    dest = padded_starts[sorted_e] + (jnp.arange(n_assign) - starts[sorted_e])
    n_rows = (-(-n_assign // EXPERT_BLOCK) + N_EXPERTS) * EXPERT_BLOCK
    row_tok = jnp.full((n_rows,), T, jnp.int32).at[dest].set(flat_tok[order])
    row_w = jnp.zeros((n_rows,), h.dtype).at[dest].set(flat_w[order].astype(h.dtype))
    block_start = jnp.arange(n_rows // EXPERT_BLOCK) * EXPERT_BLOCK
    block_expert = jnp.minimum(jnp.searchsorted(padded_ends, block_start, side='right'),
                               N_EXPERTS - 1)
    h_pad = jnp.concatenate([h, jnp.zeros((1, D), h.dtype)], axis=0)

    def expert_block(args):
        tok, e = args
        rows = h_pad[tok]
        return (jax.nn.silu(rows @ w_gate[e]) * (rows @ w_up[e])) @ w_down[e]

    y = lax.map(expert_block, (row_tok.reshape(-1, EXPERT_BLOCK), block_expert))
    y = y.reshape(n_rows, D) * row_w[:, None]
    return jnp.zeros((T + 1, D), h.dtype).at[row_tok].add(y)[:T]


def setup_inputs(seed: int = 0) -> dict:
    key = jax.random.key(seed)
    ks = jax.random.split(key, 24)

    def nrm(k, shape, fan_in):
        return jax.random.normal(k, shape, jnp.float32) * fan_in ** -0.5

    def gain(k, shape):
        return 1.0 + 0.02 * jax.random.normal(k, shape, jnp.float32)

    L = DEPTH
    x = jax.random.normal(ks[0], (BATCH, SEQ, D_MODEL), jnp.float32)
    positions = jnp.broadcast_to(jnp.arange(SEQ, dtype=jnp.int32), (BATCH, SEQ))
    return {
        "x": x,
        "positions": positions,
        "attn_norm": gain(ks[1], (L, D_MODEL)),
        "w_in": nrm(ks[2], (L, D_MODEL, IN_WIDTH), D_MODEL),
        "da_lambda": 0.1 * jax.random.normal(ks[3], (L, 4, DA_HEAD_DIM), jnp.float32),
        "da_subln": gain(ks[4], (L, DA_V_DIM)),
        "mla_q_norm": gain(ks[5], (L, MLA_Q_RANK)),
        "mla_w_uq": nrm(ks[6], (L, MLA_Q_RANK, MLA_HEADS * (MLA_NOPE + MLA_ROPE)), MLA_Q_RANK),
        "mla_kv_norm": gain(ks[7], (L, MLA_KV_RANK)),
        "mla_w_ukv": nrm(ks[8], (L, MLA_KV_RANK, MLA_HEADS * (MLA_NOPE + MLA_V)), MLA_KV_RANK),
        "w_branch_a": nrm(ks[9], (L, DA_WIDTH, D_MODEL), DA_WIDTH),
        "w_branch_b": nrm(ks[10], (L, MLA_WIDTH, D_MODEL), MLA_WIDTH),
        "w_out": nrm(ks[11], (L, D_MODEL, D_MODEL), D_MODEL),
        "ffn_norm": gain(ks[12], (L, D_MODEL)),
        "w_group": nrm(ks[13], (L, D_MODEL, N_GROUPS), D_MODEL),
        "b_group": 0.01 * jax.random.normal(ks[14], (L, N_GROUPS), jnp.float32),
        "w_router": nrm(ks[15], (L, D_MODEL, N_EXPERTS), D_MODEL),
        "b_router": 0.01 * jax.random.normal(ks[16], (L, N_EXPERTS), jnp.float32),
        "w_exp_gate": nrm(ks[17], (L, N_EXPERTS, D_MODEL, EXPERT_FF), D_MODEL),
        "w_exp_up": nrm(ks[18], (L, N_EXPERTS, D_MODEL, EXPERT_FF), D_MODEL),
        "w_exp_down": nrm(ks[19], (L, N_EXPERTS, EXPERT_FF, D_MODEL), EXPERT_FF),
        "final_norm": gain(ks[20], (D_MODEL,)),
    }


def reference(x, positions, attn_norm, w_in, da_lambda, da_subln, mla_q_norm, mla_w_uq,
              mla_kv_norm, mla_w_ukv, w_branch_a, w_branch_b, w_out, ffn_norm, w_group,
              b_group, w_router, b_router, w_exp_gate, w_exp_up, w_exp_down, final_norm):
    B, S, _ = x.shape
    split_points = np.cumsum(IN_SIZES)[:-1]
    for l in range(DEPTH):
        h = rms_norm(x, attn_norm[l])
        proj = h @ w_in[l]
        qa, ka, va, cq, ckv, krope, ga, gb = jnp.split(proj, split_points, axis=-1)

        lam_init = 0.8 - 0.6 * math.exp(-0.3 * l)
        qa = rope(qa.reshape(B, S, 2 * DA_HEADS, DA_HEAD_DIM), positions)
        ka = rope(ka.reshape(B, S, 2 * DA_HEADS, DA_HEAD_DIM), positions)
        qa = qa.reshape(B, S, DA_HEADS, 2, DA_HEAD_DIM)
        ka = ka.reshape(B, S, DA_HEADS, 2, DA_HEAD_DIM)
        va = va.reshape(B, S, DA_HEADS, DA_V_DIM)
        lq1, lk1, lq2, lk2 = da_lambda[l].astype(jnp.float32)
        lam = jnp.exp(jnp.sum(lq1 * lk1)) - jnp.exp(jnp.sum(lq2 * lk2)) + lam_init
        oa = diff_attention(qa, ka, va, lam)
        oa = rms_norm(oa, da_subln[l]) * (1.0 - lam_init)
        ya = oa.reshape(B, S, DA_WIDTH) @ w_branch_a[l]

        qb = (rms_norm(cq, mla_q_norm[l]) @ mla_w_uq[l]).reshape(B, S, MLA_HEADS, MLA_NOPE + MLA_ROPE)
        q_nope, q_rot = jnp.split(qb, [MLA_NOPE], axis=-1)
        q_rot = rope(q_rot, positions)
        kv = (rms_norm(ckv, mla_kv_norm[l]) @ mla_w_ukv[l]).reshape(B, S, MLA_HEADS, MLA_NOPE + MLA_V)
        k_nope, vb = jnp.split(kv, [MLA_NOPE], axis=-1)
        k_rot = rope(krope[:, :, None, :], positions)
        q_full = jnp.concatenate([q_nope, q_rot], axis=-1)
        k_full = jnp.concatenate([k_nope, jnp.broadcast_to(k_rot, (B, S, MLA_HEADS, MLA_ROPE))], axis=-1)
        ob = softmax_attention(q_full, k_full, vb, (MLA_NOPE + MLA_ROPE) ** -0.5)
        yb = ob.reshape(B, S, MLA_WIDTH) @ w_branch_b[l]

        mixed = jax.nn.sigmoid(ga) * ya + jax.nn.sigmoid(gb) * yb
        x = x + mixed @ w_out[l]

        hf = rms_norm(x, ffn_norm[l]).reshape(B * S, D_MODEL)
        expert_id, comb_w = hierarchical_route(hf, w_group[l], b_group[l], w_router[l], b_router[l])
        moe = routed_experts(hf, expert_id, comb_w, w_exp_gate[l], w_exp_up[l], w_exp_down[l])
        x = x + moe.reshape(B, S, D_MODEL)
    return rms_norm(x, final_norm)
```

```python
import functools
import math

import jax
import jax.numpy as jnp
from jax import lax
from jax.experimental import pallas as pl
from jax.experimental.pallas import tpu as pltpu

f32 = jnp.float32
bf16 = jnp.bfloat16
i32 = jnp.int32

D_MODEL = 2048
BATCH = 4
SEQ = 2048
N_TOK = BATCH * SEQ
HEADS = 8
HEAD_W = 128
DA_HALF = 64
MLA_Q_RANK = 512
MLA_KV_RANK = 256
MLA_NOPE = 64
MLA_ROPE = 32
N_GROUPS = 4
EXPERTS_PER_GROUP = 8
N_EXPERTS = N_GROUPS * EXPERTS_PER_GROUP
TOP_K = 2
EXPERT_FF = 512
ROPE_THETA = 10000.0
NORM_EPS = 1e-6
NEG_INF = -1e30
LAM_INIT = 0.8 - 0.6 * math.exp(-0.3 * 0)
LOG2E = math.log2(math.e)

LANES = 128
VMEM_LIMIT = 56 * 1024 * 1024

COL_Q, COL_K, COL_V = 0, 1024, 2048
COL_CQ, COL_CKV = 3072, 3584
COL_KR = 3840
COL_GA, COL_GB = 4096, 6144
PROJ_W = 8192
KR_LANE = MLA_NOPE

TM_PROJ, TN_PROJ = 1024, 1024
TM_PREP = 512
TQ = 256
TM_MERGE = 256
TM_ROUTE = 512
ROW_BLK = 128
N_ROWS = (N_TOK * TOP_K // ROW_BLK + N_EXPERTS) * ROW_BLK
N_BLKS = N_ROWS // ROW_BLK
TT_DISP = 512
TM_COMB = 256

Q_SCALE_A = (DA_HALF ** -0.5) * LOG2E
Q_SCALE_B = ((MLA_NOPE + MLA_ROPE) ** -0.5) * LOG2E


def _cparams(sem):
    return pltpu.CompilerParams(dimension_semantics=sem, vmem_limit_bytes=VMEM_LIMIT)


def _rms(x, g):
    return x * lax.rsqrt(jnp.mean(x * x, axis=-1, keepdims=True) + NORM_EPS) * g


def _rope_lanes(blk, cos, sin, first_half, half):
    up = pltpu.roll(blk, LANES - half, 1)
    down = pltpu.roll(blk, half, 1)
    return blk * cos + jnp.where(first_half, up, down) * sin


def _rope_tables_kernel(pos_ref, inv_ref, sgn_ref, cos_ref, sin_ref):
    ang = pos_ref[...].astype(f32) * inv_ref[...]
    cos_ref[...] = jnp.cos(ang)
    sin_ref[...] = jnp.sin(ang) * sgn_ref[...]


def _rope_tables(positions):
    inv_a = ROPE_THETA ** (-jnp.arange(0, DA_HALF, 2, dtype=f32) / DA_HALF)
    inv_b = ROPE_THETA ** (-jnp.arange(0, MLA_ROPE, 2, dtype=f32) / MLA_ROPE)
    pat_a = jnp.tile(inv_a, 4)
    sgn_a = jnp.tile(jnp.concatenate([-jnp.ones(32, f32), jnp.ones(32, f32)]), 2)
    z = lambda n: jnp.zeros((n,), f32)
    pat_b = jnp.concatenate([z(KR_LANE), inv_b, inv_b, z(LANES - KR_LANE - MLA_ROPE)])
    sgn_b = jnp.concatenate([z(KR_LANE), -jnp.ones(16, f32), jnp.ones(16, f32),
                             z(LANES - KR_LANE - MLA_ROPE)])
    inv = jnp.concatenate([pat_a, pat_b])[None, :]
    sgn = jnp.concatenate([sgn_a, sgn_b])[None, :]
    pos = positions.reshape(N_TOK, 1)
    tm = 1024
    return pl.pallas_call(
        _rope_tables_kernel,
        grid=(N_TOK // tm,),
        in_specs=[pl.BlockSpec((tm, 1), lambda i: (i, 0)),
                  pl.BlockSpec((1, 2 * LANES), lambda i: (0, 0)),
                  pl.BlockSpec((1, 2 * LANES), lambda i: (0, 0))],
        out_specs=[pl.BlockSpec((tm, 2 * LANES), lambda i: (i, 0)),
                   pl.BlockSpec((tm, 2 * LANES), lambda i: (i, 0))],
        out_shape=[jax.ShapeDtypeStruct((N_TOK, 2 * LANES), f32)] * 2,
        compiler_params=_cparams(("arbitrary",)),
        name="rope_tables",
    )(pos, inv, sgn)


def _in_proj_kernel(x_ref, g_ref, w_ref, cos_ref, sin_ref, o_ref, h_scr):
    n = pl.program_id(1)

    @pl.when(n == 0)
    def _():
        h_scr[...] = _rms(x_ref[...], g_ref[...]).astype(bf16)

    acc = jnp.dot(h_scr[...], w_ref[...], preferred_element_type=f32)
    n_rope = COL_V // TN_PROJ
    n_q = COL_K // TN_PROJ

    @pl.when(n < n_rope)
    def _():
        c = jnp.where(n < n_q, Q_SCALE_A, 1.0).astype(f32)
        cos = cos_ref[...] * c
        sin = sin_ref[...] * c
        lane = lax.broadcasted_iota(i32, (TM_PROJ, LANES), 1)
        first = (lane % DA_HALF) < (DA_HALF // 2)
        for j in range(TN_PROJ // LANES):
            blk = acc[:, j * LANES:(j + 1) * LANES]
            o_ref[:, j * LANES:(j + 1) * LANES] = _rope_lanes(
                blk, cos, sin, first, DA_HALF // 2).astype(bf16)

    @pl.when(n >= n_rope)
    def _():
        o_ref[...] = acc.astype(bf16)


def _in_proj(x2d, g, w_pad, cos_t, sin_t):
    return pl.pallas_call(
        _in_proj_kernel,
        grid=(N_TOK // TM_PROJ, PROJ_W // TN_PROJ),
        in_specs=[pl.BlockSpec((TM_PROJ, D_MODEL), lambda i, n: (i, 0)),
                  pl.BlockSpec((1, D_MODEL), lambda i, n: (0, 0)),
                  pl.BlockSpec((D_MODEL, TN_PROJ), lambda i, n: (0, n)),
                  pl.BlockSpec((TM_PROJ, LANES), lambda i, n: (i, 0)),
                  pl.BlockSpec((TM_PROJ, LANES), lambda i, n: (i, 0))],
        out_specs=pl.BlockSpec((TM_PROJ, TN_PROJ), lambda i, n: (i, n)),
        out_shape=jax.ShapeDtypeStruct((N_TOK, PROJ_W), bf16),
        scratch_shapes=[pltpu.VMEM((TM_PROJ, D_MODEL), bf16)],
        compiler_params=_cparams(("arbitrary", "arbitrary")),
        name="in_proj",
    )(x2d, g, w_pad, cos_t, sin_t)


def _mla_prep_kernel(cq_ref, ckv_ref, kr_ref, gq_ref, gkv_ref, wuq_ref, wkv_ref, cos_ref, sin_ref,
                     q_ref, k_ref, v_ref):
    cqn = _rms(cq_ref[...].astype(f32), gq_ref[...]).astype(bf16)
    q = jnp.dot(cqn, wuq_ref[...], preferred_element_type=f32)
    ckvn = _rms(ckv_ref[...].astype(f32), gkv_ref[...]).astype(bf16)
    kv = jnp.dot(ckvn, wkv_ref[...], preferred_element_type=f32)
    cos = cos_ref[...]
    sin = sin_ref[...]
    lane = lax.broadcasted_iota(i32, (TM_PREP, LANES), 1)
    first = lane < (KR_LANE + MLA_ROPE // 2)
    rope = lambda b: _rope_lanes(b, cos, sin, first, MLA_ROPE // 2)
    k_rot = rope(kr_ref[...].astype(f32))
    for h in range(HEADS):
        sl = slice(h * HEAD_W, (h + 1) * HEAD_W)
        q_ref[:, sl] = (rope(q[:, sl]) * Q_SCALE_B).astype(bf16)
        k_ref[:, sl] = (kv[:, sl] + k_rot).astype(bf16)
    v_ref[...] = kv[:, HEADS * HEAD_W:].astype(bf16)


def _mla_prep(proj, gq, gkv, wuq_p, wkv_p, cos_t, sin_t):
    tm = TM_PREP
    w = HEADS * HEAD_W
    return pl.pallas_call(
        _mla_prep_kernel,
        grid=(N_TOK // tm,),
        in_specs=[pl.BlockSpec((tm, MLA_Q_RANK), lambda i: (i, COL_CQ // MLA_Q_RANK)),
                  pl.BlockSpec((tm, MLA_KV_RANK), lambda i: (i, COL_CKV // MLA_KV_RANK)),
                  pl.BlockSpec((tm, LANES), lambda i: (i, COL_KR // LANES)),
                  pl.BlockSpec((1, MLA_Q_RANK), lambda i: (0, 0)),
                  pl.BlockSpec((1, MLA_KV_RANK), lambda i: (0, 0)),
                  pl.BlockSpec((MLA_Q_RANK, w), lambda i: (0, 0)),
                  pl.BlockSpec((MLA_KV_RANK, 2 * w), lambda i: (0, 0)),
                  pl.BlockSpec((tm, LANES), lambda i: (i, 1)),
                  pl.BlockSpec((tm, LANES), lambda i: (i, 1))],
        out_specs=[pl.BlockSpec((tm, w), lambda i: (i, 0))] * 3,
        out_shape=[jax.ShapeDtypeStruct((N_TOK, w), bf16)] * 3,
        compiler_params=_cparams(("arbitrary",)),
        name="mla_prep",
    )(proj, proj, proj, gq, gkv, wuq_p, wkv_p, cos_t, sin_t)


def _attn_kernel(*refs, ncomp):
    if ncomp == 2:
        q_ref, k_ref, v_ref, lam_ref, g_ref, o_ref = refs
    else:
        q_ref, k_ref, v_ref, o_ref = refs
    qi = pl.program_id(2)
    q = q_ref[...]
    if ncomp == 2:
        lane = lax.broadcasted_iota(i32, (TQ, HEAD_W), 1)
        zero = jnp.zeros_like(q)
        qs = [jnp.where(lane < DA_HALF, q, zero), jnp.where(lane >= DA_HALF, q, zero)]
    else:
        qs = [q]

    def step(j, carry, masked):
        off = pl.multiple_of(j * TQ, TQ)
        k = k_ref[pl.ds(off, TQ), :]
        v = v_ref[pl.ds(off, TQ), :]
        out = []
        for c in range(ncomp):
            m, l, acc = carry[c]
            s = lax.dot_general(qs[c], k, (((1,), (1,)), ((), ())), preferred_element_type=f32)
            if masked:
                row = lax.broadcasted_iota(i32, (TQ, TQ), 0)
                col = lax.broadcasted_iota(i32, (TQ, TQ), 1)
                s = jnp.where(col <= row, s, NEG_INF)
            m_new = jnp.maximum(m, jnp.max(s, axis=-1, keepdims=True))
            alpha = jnp.exp2(m - m_new)
            p = jnp.exp2(s - m_new)
            l_new = alpha * l + jnp.sum(p, axis=-1, keepdims=True)
            acc_new = alpha * acc + jnp.dot(p.astype(bf16), v, preferred_element_type=f32)
            out.append((m_new, l_new, acc_new))
        return tuple(out)

    init = tuple((jnp.full((TQ, 1), NEG_INF, f32), jnp.zeros((TQ, 1), f32),
                  jnp.zeros((TQ, HEAD_W), f32)) for _ in range(ncomp))
    carry = lax.fori_loop(0, qi, lambda j, c: step(j, c, False), init)
    carry = step(qi, carry, True)

    if ncomp == 2:
        ll = lam_ref[...]
        s1 = jnp.sum(ll[0:1, :] * ll[1:2, :], axis=-1, keepdims=True)
        s2 = jnp.sum(ll[2:3, :] * ll[3:4, :], axis=-1, keepdims=True)
        lam = jnp.exp(s1) - jnp.exp(s2) + LAM_INIT
        (_, l0, a0), (_, l1, a1) = carry
        o = a0 / l0 - lam * (a1 / l1)
        o_ref[...] = (_rms(o, g_ref[...]) * (1.0 - LAM_INIT)).astype(bf16)
    else:
        (_, l0, a0), = carry
        o_ref[...] = (a0 / l0).astype(bf16)


def _attention(q_arr, k_arr, v_arr, q_blk0, k_blk0, v_blk0, extra=()):
    ncomp = 2 if extra else 1
    nq = SEQ // TQ
    in_specs = [pl.BlockSpec((TQ, HEAD_W), lambda b, h, i: (b * nq + i, q_blk0 + h)),
                pl.BlockSpec((SEQ, HEAD_W), lambda b, h, i: (b, k_blk0 + h)),
                pl.BlockSpec((SEQ, HEAD_W), lambda b, h, i: (b, v_blk0 + h))]
    for a in extra:
        in_specs.append(pl.BlockSpec(a.shape, lambda b, h, i: (0, 0)))
    return pl.pallas_call(
        functools.partial(_attn_kernel, ncomp=ncomp),
        grid=(BATCH, HEADS, nq),
        in_specs=in_specs,
        out_specs=pl.BlockSpec((TQ, HEAD_W), lambda b, h, i: (b * nq + i, h)),
        out_shape=jax.ShapeDtypeStruct((N_TOK, HEADS * HEAD_W), bf16),
        compiler_params=_cparams(("arbitrary", "arbitrary", "arbitrary")),
        name="diff_attn" if extra else "mla_attn",
    )(q_arr, k_arr, v_arr, *extra)


def _merge_kernel(oa_ref, ob_ref, ga_ref, gb_ref, x_ref, wa_ref, wb_ref, wo_ref, o_ref):
    ya = jnp.dot(oa_ref[...], wa_ref[...], preferred_element_type=f32)
    yb = jnp.dot(ob_ref[...], wb_ref[...], preferred_element_type=f32)
    mixed = (jax.nn.sigmoid(ga_ref[...].astype(f32)) * ya
             + jax.nn.sigmoid(gb_ref[...].astype(f32)) * yb)
    o_ref[...] = x_ref[...] + jnp.dot(mixed.astype(bf16), wo_ref[...], preferred_element_type=f32)


def _merge(oa, ob, proj, x2d, wa, wb, wo):
    tm = TM_MERGE
    w = HEADS * HEAD_W
    const = lambda shape: pl.BlockSpec(shape, lambda i: (0, 0))
    return pl.pallas_call(
        _merge_kernel,
        grid=(N_TOK // tm,),
        in_specs=[pl.BlockSpec((tm, w), lambda i: (i, 0)),
                  pl.BlockSpec((tm, w), lambda i: (i, 0)),
                  pl.BlockSpec((tm, D_MODEL), lambda i: (i, COL_GA // D_MODEL)),
                  pl.BlockSpec((tm, D_MODEL), lambda i: (i, COL_GB // D_MODEL)),
                  pl.BlockSpec((tm, D_MODEL), lambda i: (i, 0)),
                  const((w, D_MODEL)), const((w, D_MODEL)), const((D_MODEL, D_MODEL))],
        out_specs=pl.BlockSpec((tm, D_MODEL), lambda i: (i, 0)),
        out_shape=jax.ShapeDtypeStruct((N_TOK, D_MODEL), f32),
        compiler_params=_cparams(("arbitrary",)),
        name="merge_out",
    )(oa, ob, proj, proj, x2d, wa, wb, wo)


RT_GROUP_ROW = 0
RT_EXPERT_ROW = 8


def _router_kernel(x_ref, g_ref, w_ref, b_ref, hf_ref, eid_ref, cw_ref):
    hf = _rms(x_ref[...], g_ref[...])
    hf_ref[...] = hf
    logits = jnp.dot(hf, w_ref[...], preferred_element_type=f32,
                     precision=lax.Precision.HIGHEST) + b_ref[...]
    lt = logits.T
    tm = lt.shape[1]
    ridx = lax.broadcasted_iota(i32, (EXPERTS_PER_GROUP, tm), 0).astype(f32)

    def top1(v):
        vmax = jnp.max(v, axis=0, keepdims=True)
        idx = jnp.min(jnp.where(v == vmax, ridx, float(EXPERTS_PER_GROUP)), axis=0, keepdims=True)
        return vmax, idx

    gl = lt[RT_GROUP_ROW:RT_GROUP_ROW + 8, :]
    gmax, gsel = top1(gl)
    p_g = 1.0 / jnp.sum(jnp.exp(gl - gmax), axis=0, keepdims=True)
    el = jnp.zeros((EXPERTS_PER_GROUP, tm), f32)
    for g in range(N_GROUPS):
        r0 = RT_EXPERT_ROW + g * EXPERTS_PER_GROUP
        el = jnp.where(gsel == float(g), lt[r0:r0 + EXPERTS_PER_GROUP, :], el)
    e1, i1 = top1(el)
    e2, i2 = top1(jnp.where(ridx == i1, -jnp.inf, el))
    r = jnp.exp(e2 - e1)
    w1 = 1.0 / (1.0 + r)
    base = gsel * EXPERTS_PER_GROUP
    eid_ref[0:1, :] = (base + i1).astype(i32)
    eid_ref[1:2, :] = (base + i2).astype(i32)
    cw_ref[0:1, :] = p_g * w1
    cw_ref[1:2, :] = p_g * (r * w1)


def _router(x1, g, w_rt, b_rt):
    tm = TM_ROUTE
    return pl.pallas_call(
        _router_kernel,
        grid=(N_TOK // tm,),
        in_specs=[pl.BlockSpec((tm, D_MODEL), lambda i: (i, 0)),
                  pl.BlockSpec((1, D_MODEL), lambda i: (0, 0)),
                  pl.BlockSpec((D_MODEL, LANES), lambda i: (0, 0)),
                  pl.BlockSpec((1, LANES), lambda i: (0, 0))],
        out_specs=[pl.BlockSpec((tm, D_MODEL), lambda i: (i, 0)),
                   pl.BlockSpec((TOP_K, tm), lambda i: (0, i)),
                   pl.BlockSpec((TOP_K, tm), lambda i: (0, i))],
        out_shape=[jax.ShapeDtypeStruct((N_TOK, D_MODEL), f32),
                   jax.ShapeDtypeStruct((TOP_K, N_TOK), i32),
                   jax.ShapeDtypeStruct((TOP_K, N_TOK), f32)],
        compiler_params=_cparams(("arbitrary",)),
        name="router",
    )(x1, g, w_rt, b_rt)


PLAN_N = 128


def _plan_kernel(eid_ref, dest_ref, be_ref, ends_ref):
    eid = eid_ref[...]
    r = lax.broadcasted_iota(i32, (PLAN_N, PLAN_N), 0)
    c = lax.broadcasted_iota(i32, (PLAN_N, PLAN_N), 1)
    upper = (r < c).astype(bf16)
    lower = (c < r).astype(bf16)
    ones = jnp.ones((PLAN_N, PLAN_N), bf16)
    blk_start = (lax.broadcasted_iota(i32, (1, 2 * LANES), 1) * ROW_BLK).astype(f32)
    lane = lax.broadcasted_iota(i32, (1, LANES), 1)
    dest = jnp.zeros((PLAN_N, PLAN_N), f32)
    blk_e = jnp.zeros((1, 2 * LANES), f32)
    ends = jnp.zeros((1, LANES), f32)
    start = jnp.zeros((1, 1), f32)
    for e in range(N_EXPERTS):
        m = (eid == e).astype(f32)
        mb = m.astype(bf16)
        within = jnp.dot(mb, upper, preferred_element_type=f32)
        rowtot = jnp.dot(mb, ones, preferred_element_type=f32)
        rowpre = jnp.dot(lower, rowtot.astype(bf16), preferred_element_type=f32)
        cnt = jnp.sum(jnp.sum(m, axis=1, keepdims=True), axis=0, keepdims=True)
        dest = dest + m * (start + within + rowpre)
        start = start + jnp.floor((cnt + (ROW_BLK - 1)) * (1.0 / ROW_BLK)) * ROW_BLK
        blk_e = blk_e + (start <= blk_start).astype(f32)
        ends = jnp.where(lane == e, start, ends)
    dest_ref[...] = dest.astype(i32)
    be_ref[...] = jnp.minimum(blk_e, N_EXPERTS - 1).astype(i32)
    ends_ref[...] = ends.astype(i32)


def _plan(eid):
    whole = lambda shape: pl.BlockSpec(shape, lambda: (0, 0))
    return pl.pallas_call(
        _plan_kernel,
        in_specs=[whole((PLAN_N, PLAN_N))],
        out_specs=[whole((PLAN_N, PLAN_N)), whole((1, 2 * LANES)), whole((1, LANES))],
        out_shape=[jax.ShapeDtypeStruct((PLAN_N, PLAN_N), i32),
                   jax.ShapeDtypeStruct((1, 2 * LANES), i32),
                   jax.ShapeDtypeStruct((1, LANES), i32)],
        name="moe_plan",
    )(eid.reshape(PLAN_N, PLAN_N))


def _dispatch_kernel(ends_ref, dest_ref, hf_hbm, xs_hbm, zbuf, sem):
    i = pl.program_id(0)

    def tail_copy(e):
        pend = ends_ref[e]
        prev = jnp.where(e > 0, ends_ref[jnp.maximum(e - 1, 0)], 0)
        row0 = pl.multiple_of(jnp.maximum(pend - ROW_BLK, 0), ROW_BLK)
        return pend > prev, pltpu.make_async_copy(zbuf, xs_hbm.at[pl.ds(row0, ROW_BLK), :], sem)

    @pl.when(i == 0)
    def _():
        zbuf[...] = jnp.zeros_like(zbuf)

        def zstart(e, _):
            used, cp = tail_copy(e)

            @pl.when(used)
            def _():
                cp.start()
            return 0

        def zwait(e, _):
            used, cp = tail_copy(e)

            @pl.when(used)
            def _():
                cp.wait()
            return 0

        def unused_copy(b):
            row0 = pl.multiple_of(b * ROW_BLK, ROW_BLK)
            return pltpu.make_async_copy(zbuf, xs_hbm.at[pl.ds(row0, ROW_BLK), :], sem)

        def ustart(b, _):
            unused_copy(b).start()
            return 0

        def uwait(b, _):
            unused_copy(b).wait()
            return 0

        n_used = ends_ref[N_EXPERTS - 1] // ROW_BLK
        lax.fori_loop(0, N_EXPERTS, zstart, 0)
        lax.fori_loop(n_used, N_BLKS, ustart, 0)
        lax.fori_loop(0, N_EXPERTS, zwait, 0)
        lax.fori_loop(n_used, N_BLKS, uwait, 0)

    base = i * TT_DISP

    def row_copy(t, k):
        d = dest_ref[k, base + t]
        return pltpu.make_async_copy(hf_hbm.at[pl.ds(base + t, 1), :], xs_hbm.at[pl.ds(d, 1), :], sem)

    def issue(t, _):
        for k in range(TOP_K):
            row_copy(t, k).start()
        return 0

    def drain(t, _):
        for k in range(TOP_K):
            row_copy(t, k).wait()
        return 0

    lax.fori_loop(0, TT_DISP, issue, 0)
    lax.fori_loop(0, TT_DISP, drain, 0)


def _dispatch(ends, dest, hf):
    return pl.pallas_call(
        _dispatch_kernel,
        grid_spec=pltpu.PrefetchScalarGridSpec(
            num_scalar_prefetch=2,
            grid=(N_TOK // TT_DISP,),
            in_specs=[pl.BlockSpec(memory_space=pl.ANY)],
            out_specs=pl.BlockSpec(memory_space=pl.ANY),
            scratch_shapes=[pltpu.VMEM((ROW_BLK, D_MODEL), f32), pltpu.SemaphoreType.DMA(())]),
        out_shape=jax.ShapeDtypeStruct((N_ROWS, D_MODEL), f32),
        compiler_params=_cparams(("arbitrary",)),
        name="moe_dispatch",
    )(ends, dest, hf)


def _ffn_kernel(be_ref, nu_ref, x_ref, wg_ref, wu_ref, wd_ref, y_ref, wg_s, wu_s, wd_s):
    b = pl.program_id(0)
    e = be_ref[b]
    prev = be_ref[jnp.maximum(b - 1, 0)]

    @pl.when((b == 0) | (e != prev))
    def _():
        wg_s[...] = wg_ref[0].astype(bf16)
        wu_s[...] = wu_ref[0].astype(bf16)
        wd_s[...] = wd_ref[0].astype(bf16)

    @pl.when(b < nu_ref[0])
    def _():
        x = x_ref[...].astype(bf16)
        h1 = jnp.dot(x, wg_s[...], preferred_element_type=f32)
        h2 = jnp.dot(x, wu_s[...], preferred_element_type=f32)
        a = (h1 * jax.nn.sigmoid(h1) * h2).astype(bf16)
        y_ref[...] = jnp.dot(a, wd_s[...], preferred_element_type=f32)

    @pl.when(b >= nu_ref[0])
    def _():
        y_ref[...] = jnp.zeros_like(y_ref)


def _expert_ffn(blk_e, n_used, xs, wg, wu, wd):
    def xmap(b, be, nu):
        return (jnp.minimum(b, jnp.maximum(nu[0] - 1, 0)), 0)

    def wmap(b, be, nu):
        return (be[b], 0, 0)

    return pl.pallas_call(
        _ffn_kernel,
        grid_spec=pltpu.PrefetchScalarGridSpec(
            num_scalar_prefetch=2,
            grid=(N_BLKS,),
            in_specs=[pl.BlockSpec((ROW_BLK, D_MODEL), xmap),
                      pl.BlockSpec((1, D_MODEL, EXPERT_FF), wmap),
                      pl.BlockSpec((1, D_MODEL, EXPERT_FF), wmap),
                      pl.BlockSpec((1, EXPERT_FF, D_MODEL), wmap)],
            out_specs=pl.BlockSpec((ROW_BLK, D_MODEL), lambda b, be, nu: (b, 0)),
            scratch_shapes=[pltpu.VMEM((D_MODEL, EXPERT_FF), bf16),
                            pltpu.VMEM((D_MODEL, EXPERT_FF), bf16),
                            pltpu.VMEM((EXPERT_FF, D_MODEL), bf16)]),
        out_shape=jax.ShapeDtypeStruct((N_ROWS, D_MODEL), f32),
        compiler_params=_cparams(("arbitrary",)),
        name="expert_ffn",
    )(blk_e, n_used, xs, wg, wu, wd)


def _combine_kernel(dest_ref, x_ref, cw_ref, g_ref, y_hbm, o_ref, ybuf, sem):
    i = pl.program_id(0)
    base = i * TM_COMB

    def row_copy(t, k):
        d = dest_ref[k, base + t]
        return pltpu.make_async_copy(y_hbm.at[pl.ds(d, 1), :], ybuf.at[k, pl.ds(t, 1), :], sem)

    def issue(t, _):
        for k in range(TOP_K):
            row_copy(t, k).start()
        return 0

    def drain(t, _):
        for k in range(TOP_K):
            row_copy(t, k).wait()
        return 0

    lax.fori_loop(0, TM_COMB, issue, 0)
    lax.fori_loop(0, TM_COMB, drain, 0)
    w = cw_ref[...]
    x = x_ref[...] + w[:, 0:1] * ybuf[0] + w[:, 1:2] * ybuf[1]
    o_ref[...] = _rms(x, g_ref[...])


def _combine(dest, x1, cw_t, g, y):
    tm = TM_COMB
    return pl.pallas_call(
        _combine_kernel,
        grid_spec=pltpu.PrefetchScalarGridSpec(
            num_scalar_prefetch=1,
            grid=(N_TOK // tm,),
            in_specs=[pl.BlockSpec((tm, D_MODEL), lambda i, d: (i, 0)),
                      pl.BlockSpec((tm, TOP_K), lambda i, d: (i, 0)),
                      pl.BlockSpec((1, D_MODEL), lambda i, d: (0, 0)),
                      pl.BlockSpec(memory_space=pl.ANY)],
            out_specs=pl.BlockSpec((tm, D_MODEL), lambda i, d: (i, 0)),
            scratch_shapes=[pltpu.VMEM((TOP_K, tm, D_MODEL), f32), pltpu.SemaphoreType.DMA(())]),
        out_shape=jax.ShapeDtypeStruct((N_TOK, D_MODEL), f32),
        compiler_params=_cparams(("arbitrary",)),
        name="moe_combine",
    )(dest, x1, cw_t, g, y)


def _pad_cols(w, n):
    return jnp.pad(w, ((0, 0), (0, n - w.shape[1])))


def _prep_w_in(w_in):
    d = w_in.shape[0]
    main = w_in[:, :COL_KR]
    kr = w_in[:, COL_KR:COL_KR + MLA_ROPE]
    gates = w_in[:, COL_KR + MLA_ROPE:]
    z = lambda n: jnp.zeros((d, n), w_in.dtype)
    return jnp.concatenate(
        [main, z(KR_LANE), kr, z(COL_GA - COL_KR - KR_LANE - MLA_ROPE), gates], axis=1).astype(bf16)


def _prep_w_uq(w_uq):
    w = w_uq.reshape(MLA_Q_RANK, HEADS, MLA_NOPE + MLA_ROPE)
    w = jnp.pad(w, ((0, 0), (0, 0), (0, HEAD_W - MLA_NOPE - MLA_ROPE)))
    return w.reshape(MLA_Q_RANK, HEADS * HEAD_W).astype(bf16)


def _prep_w_ukv(w_ukv):
    w = w_ukv.reshape(MLA_KV_RANK, HEADS, MLA_NOPE + HEAD_W)
    wk = jnp.pad(w[:, :, :MLA_NOPE], ((0, 0), (0, 0), (0, HEAD_W - MLA_NOPE)))
    wv = w[:, :, MLA_NOPE:]
    return jnp.concatenate([wk.reshape(MLA_KV_RANK, -1), wv.reshape(MLA_KV_RANK, -1)], axis=1).astype(bf16)


def _prep_router(w_group, b_group, w_router, b_router):
    z = lambda n: jnp.zeros((D_MODEL, n), f32)
    w = jnp.concatenate([w_group, z(RT_EXPERT_ROW - N_GROUPS), w_router,
                         z(LANES - RT_EXPERT_ROW - N_EXPERTS)], axis=1)
    b = jnp.concatenate([b_group, jnp.full((RT_EXPERT_ROW - N_GROUPS,), NEG_INF, f32), b_router,
                         jnp.zeros((LANES - RT_EXPERT_ROW - N_EXPERTS,), f32)])[None, :]
    return w, b


def kernel(x, positions, attn_norm, w_in, da_lambda, da_subln, mla_q_norm, mla_w_uq, mla_kv_norm, mla_w_ukv, w_branch_a, w_branch_b, w_out, ffn_norm, w_group, b_group, w_router, b_router, w_exp_gate, w_exp_up, w_exp_down, final_norm):
    x2d = x.reshape(N_TOK, D_MODEL)
    cos_t, sin_t = _rope_tables(positions)

    proj = _in_proj(x2d, attn_norm[0][None, :], _prep_w_in(w_in[0]), cos_t, sin_t)
    qb, kb, vb = _mla_prep(proj, mla_q_norm[0][None, :], mla_kv_norm[0][None, :],
                           _prep_w_uq(mla_w_uq[0]), _prep_w_ukv(mla_w_ukv[0]), cos_t, sin_t)
    oa = _attention(proj, proj, proj, COL_Q // HEAD_W, COL_K // HEAD_W, COL_V // HEAD_W,
                    extra=(da_lambda[0], da_subln[0][None, :]))
    ob = _attention(qb, kb, vb, 0, 0, 0)
    x1 = _merge(oa, ob, proj, x2d, w_branch_a[0].astype(bf16), w_branch_b[0].astype(bf16),
                w_out[0].astype(bf16))

    w_rt, b_rt = _prep_router(w_group[0], b_group[0], w_router[0], b_router[0])
    hf, eid, cw = _router(x1, ffn_norm[0][None, :], w_rt, b_rt)
    dest, blk_e, ends = _plan(eid)
    dest = dest.reshape(TOP_K, N_TOK)
    ends = ends[0, :N_EXPERTS]
    n_used = ends[N_EXPERTS - 1:] // ROW_BLK
    xs = _dispatch(ends, dest, hf)
    y = _expert_ffn(blk_e[0, :N_BLKS], n_used, xs, w_exp_gate[0], w_exp_up[0], w_exp_down[0])
    out = _combine(dest, x1, cw.T, final_norm[None, :], y)
    return out.reshape(BATCH, SEQ, D_MODEL)
```

```python
import functools
import math

import jax
import jax.numpy as jnp
from jax import lax
from jax.experimental import pallas as pl
from jax.experimental.pallas import tpu as pltpu

f32 = jnp.float32
bf16 = jnp.bfloat16
i32 = jnp.int32

D_MODEL = 2048
BATCH = 4
SEQ = 2048
N_TOK = BATCH * SEQ
HEADS = 8
HEAD_W = 128
DA_HALF = 64
MLA_Q_RANK = 512
MLA_KV_RANK = 256
MLA_NOPE = 64
MLA_ROPE = 32
N_GROUPS = 4
EXPERTS_PER_GROUP = 8
N_EXPERTS = N_GROUPS * EXPERTS_PER_GROUP
TOP_K = 2
EXPERT_FF = 512
ROPE_THETA = 10000.0
NORM_EPS = 1e-6
NEG_INF = -1e30
LAM_INIT = 0.8 - 0.6 * math.exp(-0.3 * 0)
LOG2E = math.log2(math.e)

LANES = 128
VMEM_LIMIT = 56 * 1024 * 1024

COL_Q, COL_K, COL_V = 0, 1024, 2048
COL_CQ, COL_CKV = 3072, 3584
COL_KR = 3840
COL_GA, COL_GB = 4096, 6144
PROJ_W = 8192
KR_LANE = MLA_NOPE

TM_PROJ, TN_PROJ = 1024, 1024
TM_PREP = 512
TQ = 256
TM_MERGE = 256
ROW_BLK = 128
N_ROWS = (N_TOK * TOP_K // ROW_BLK + N_EXPERTS) * ROW_BLK
N_BLKS = N_ROWS // ROW_BLK
TT_DISP = 1024
TM_COMB = 256

Q_SCALE_A = (DA_HALF ** -0.5) * LOG2E
Q_SCALE_B = ((MLA_NOPE + MLA_ROPE) ** -0.5) * LOG2E


def _cparams(sem, **kw):
    return pltpu.CompilerParams(dimension_semantics=sem, vmem_limit_bytes=VMEM_LIMIT, **kw)


def _rms(x, g):
    return x * lax.rsqrt(jnp.mean(x * x, axis=-1, keepdims=True) + NORM_EPS) * g


SLAB = D_MODEL // LANES


ISSUE_UNROLL = 8


def _slab_rows(row0, nrows):
    return pl.ds(pl.multiple_of(row0 * SLAB, SLAB), nrows * SLAB)


def _slab_store(ref, idx, val):
    rows = val.shape[0]
    for j in range(SLAB):
        ref[idx + (pl.ds(j, rows, stride=SLAB), slice(None))] = val[:, j * LANES:(j + 1) * LANES]


def _slab_chunk(ref, idx, rows, j):
    return ref[idx + (pl.ds(j, rows, stride=SLAB), slice(None))]


def _slab_load(ref, idx, rows):
    return jnp.concatenate([_slab_chunk(ref, idx, rows, j) for j in range(SLAB)], axis=1)


def _rope_lanes(blk, cos, sin, first_half, half):
    up = pltpu.roll(blk, LANES - half, 1)
    down = pltpu.roll(blk, half, 1)
    return blk * cos + jnp.where(first_half, up, down) * sin


ROPE_NA = DA_HALF // 2
ROPE_NB = MLA_ROPE // 2


def _rope_tables_kernel(pos_ref, inv_ref, cos_ref, sin_ref):
    half = pos_ref.shape[0] // 2
    lane = lax.broadcasted_iota(i32, (half, LANES), 1)
    pos = jnp.where(lane < LANES // 2, pos_ref[:half, :], pos_ref[half:, :]).astype(f32)
    ang = pos * inv_ref[...]
    c = jnp.cos(ang)
    s = jnp.sin(ang)
    one = lambda n: jnp.ones((half, n), f32)
    zero = lambda n: jnp.zeros((half, n), f32)
    tail = LANES - KR_LANE - MLA_ROPE
    for part in range(2):
        o = part * (LANES // 2)
        rows = slice(part * half, (part + 1) * half)
        ca, sa = c[:, o:o + ROPE_NA], s[:, o:o + ROPE_NA]
        cb, sb = c[:, o + ROPE_NA:o + ROPE_NA + ROPE_NB], s[:, o + ROPE_NA:o + ROPE_NA + ROPE_NB]
        cos_ref[rows, :] = jnp.concatenate([ca, ca, ca, ca, one(KR_LANE), cb, cb, one(tail)], axis=1)
        sin_ref[rows, :] = jnp.concatenate([-sa, -sa, sa, sa, zero(KR_LANE), -sb, sb, zero(tail)],
                                           axis=1)


def _rope_tables(positions):
    inv_a = ROPE_THETA ** (-jnp.arange(0, DA_HALF, 2, dtype=f32) / DA_HALF)
    inv_b = ROPE_THETA ** (-jnp.arange(0, MLA_ROPE, 2, dtype=f32) / MLA_ROPE)
    inv = jnp.concatenate([inv_a, inv_b, jnp.zeros((LANES // 2 - ROPE_NA - ROPE_NB,), f32)])
    inv = jnp.tile(inv, 2)[None, :]
    pos = positions.reshape(N_TOK, 1)
    tm = 1024
    return pl.pallas_call(
        _rope_tables_kernel,
        grid=(N_TOK // tm,),
        in_specs=[pl.BlockSpec((tm, 1), lambda i: (i, 0)),
                  pl.BlockSpec((1, LANES), lambda i: (0, 0))],
        out_specs=[pl.BlockSpec((tm, 2 * LANES), lambda i: (i, 0)),
                   pl.BlockSpec((tm, 2 * LANES), lambda i: (i, 0))],
        out_shape=[jax.ShapeDtypeStruct((N_TOK, 2 * LANES), f32)] * 2,
        compiler_params=_cparams(("arbitrary",)),
        name="rope_tables",
    )(pos, inv)


def _in_proj_kernel(x_ref, g_ref, w_ref, cos_ref, sin_ref, o_ref, h_scr):
    n = pl.program_id(1)

    @pl.when(n == 0)
    def _():
        h_scr[...] = _rms(x_ref[...], g_ref[...]).astype(bf16)

    acc = lax.dot_general(h_scr[...], w_ref[...], (((1,), (1,)), ((), ())),
                          preferred_element_type=f32)
    n_rope = COL_V // TN_PROJ
    n_q = COL_K // TN_PROJ

    @pl.when(n < n_rope)
    def _():
        c = jnp.where(n < n_q, Q_SCALE_A, 1.0).astype(f32)
        cos = cos_ref[...] * c
        sin = sin_ref[...] * c
        for j in range(TN_PROJ // LANES):
            blk = acc[:, j * LANES:(j + 1) * LANES]
            rot = pltpu.roll(blk, LANES // 2, 1)
            o_ref[:, j * LANES:(j + 1) * LANES] = (blk * cos + rot * sin).astype(bf16)

    @pl.when(n >= n_rope)
    def _():
        o_ref[...] = acc.astype(bf16)


def _in_proj(x2d, g, w_pad, cos_t, sin_t):
    return pl.pallas_call(
        _in_proj_kernel,
        grid=(N_TOK // TM_PROJ, PROJ_W // TN_PROJ),
        in_specs=[pl.BlockSpec((TM_PROJ, D_MODEL), lambda i, n: (i, 0)),
                  pl.BlockSpec((1, D_MODEL), lambda i, n: (0, 0)),
                  pl.BlockSpec((TN_PROJ, D_MODEL), lambda i, n: (n, 0)),
                  pl.BlockSpec((TM_PROJ, LANES), lambda i, n: (i, 0)),
                  pl.BlockSpec((TM_PROJ, LANES), lambda i, n: (i, 0))],
        out_specs=pl.BlockSpec((TM_PROJ, TN_PROJ), lambda i, n: (i, n)),
        out_shape=jax.ShapeDtypeStruct((N_TOK, PROJ_W), bf16),
        scratch_shapes=[pltpu.VMEM((TM_PROJ, D_MODEL), bf16)],
        compiler_params=_cparams(("arbitrary", "arbitrary")),
        name="in_proj",
    )(x2d, g, w_pad, cos_t, sin_t)


def _mla_prep_kernel(cq_ref, ckv_ref, kr_ref, gq_ref, gkv_ref, wuq_ref, wkv_ref, cos_ref, sin_ref,
                     q_ref, k_ref, v_ref):
    cqn = _rms(cq_ref[...].astype(f32), gq_ref[...]).astype(bf16)
    q = jnp.dot(cqn, wuq_ref[...], preferred_element_type=f32)
    ckvn = _rms(ckv_ref[...].astype(f32), gkv_ref[...]).astype(bf16)
    kv = jnp.dot(ckvn, wkv_ref[...], preferred_element_type=f32)
    cos = cos_ref[...]
    sin = sin_ref[...]
    lane = lax.broadcasted_iota(i32, (TM_PREP, LANES), 1)
    first = lane < (KR_LANE + MLA_ROPE // 2)
    rope = lambda b: _rope_lanes(b, cos, sin, first, MLA_ROPE // 2)
    k_rot = rope(kr_ref[...].astype(f32))
    for h in range(HEADS):
        sl = slice(h * HEAD_W, (h + 1) * HEAD_W)
        q_ref[:, sl] = (rope(q[:, sl]) * Q_SCALE_B).astype(bf16)
        k_ref[:, sl] = (kv[:, sl] + k_rot).astype(bf16)
    v_ref[...] = kv[:, HEADS * HEAD_W:].astype(bf16)


def _mla_prep(proj, gq, gkv, wuq_p, wkv_p, cos_t, sin_t):
    tm = TM_PREP
    w = HEADS * HEAD_W
    return pl.pallas_call(
        _mla_prep_kernel,
        grid=(N_TOK // tm,),
        in_specs=[pl.BlockSpec((tm, MLA_Q_RANK), lambda i: (i, COL_CQ // MLA_Q_RANK)),
                  pl.BlockSpec((tm, MLA_KV_RANK), lambda i: (i, COL_CKV // MLA_KV_RANK)),
                  pl.BlockSpec((tm, LANES), lambda i: (i, COL_KR // LANES)),
                  pl.BlockSpec((1, MLA_Q_RANK), lambda i: (0, 0)),
                  pl.BlockSpec((1, MLA_KV_RANK), lambda i: (0, 0)),
                  pl.BlockSpec((MLA_Q_RANK, w), lambda i: (0, 0)),
                  pl.BlockSpec((MLA_KV_RANK, 2 * w), lambda i: (0, 0)),
                  pl.BlockSpec((tm, LANES), lambda i: (i, 1)),
                  pl.BlockSpec((tm, LANES), lambda i: (i, 1))],
        out_specs=[pl.BlockSpec((tm, w), lambda i: (i, 0))] * 3,
        out_shape=[jax.ShapeDtypeStruct((N_TOK, w), bf16)] * 3,
        compiler_params=_cparams(("arbitrary",)),
        name="mla_prep",
    )(proj, proj, proj, gq, gkv, wuq_p, wkv_p, cos_t, sin_t)


def _attn_kernel(*refs, ncomp, heads):
    if ncomp == 2:
        q_ref, k_ref, v_ref, lam_ref, g_ref, o_ref, acc_scr, vt_scr = refs
    else:
        q_ref, k_ref, v_ref, o_ref, acc_scr, vt_scr = refs
    qi = pl.program_id(2)

    @pl.when(qi == 0)
    def _():
        for j in range(SEQ // TQ):
            for h in range(heads):
                vt_scr[j, h] = v_ref[j * TQ:(j + 1) * TQ, h * HEAD_W:(h + 1) * HEAD_W].T

    lane = lax.broadcasted_iota(i32, (TQ, HEAD_W), 1)
    qs = []
    for h in range(heads):
        q = q_ref[:, h * HEAD_W:(h + 1) * HEAD_W]
        if ncomp == 2:
            zero = jnp.zeros_like(q)
            in_a = (lane % DA_HALF) < (DA_HALF // 2)
            qs += [jnp.where(in_a, q, zero), jnp.where(in_a, zero, q)]
        else:
            qs.append(q)

    def step(j, carry, masked):
        off = pl.multiple_of(j * TQ, TQ)
        hsl = lambda n: slice((n // ncomp) * HEAD_W, (n // ncomp + 1) * HEAD_W)
        ss = []
        for n, qc in enumerate(qs):
            k = k_ref[pl.ds(off, TQ), hsl(n)]
            s = lax.dot_general(k, qc, (((1,), (1,)), ((), ())), preferred_element_type=f32)
            if masked:
                key = lax.broadcasted_iota(i32, (TQ, TQ), 0)
                qry = lax.broadcasted_iota(i32, (TQ, TQ), 1)
                s = jnp.where(key <= qry, s, NEG_INF)
            ss.append(s)
        stats = []
        for n, s in enumerate(ss):
            m, l = carry[n]
            m_new = jnp.maximum(m, jnp.max(s, axis=0, keepdims=True))
            alpha = jnp.exp2(m - m_new)
            p = jnp.exp2(s - m_new)
            l_new = alpha * l + jnp.sum(p, axis=0, keepdims=True)
            stats.append((m_new, l_new, alpha, p.astype(bf16)))
        out = []
        for n, (m_new, l_new, alpha, p) in enumerate(stats):
            pv = jnp.dot(vt_scr[j, n // ncomp], p, preferred_element_type=f32)
            acc_scr[n] = alpha * acc_scr[n] + pv
            out.append((m_new, l_new))
        return tuple(out)

    acc_scr[...] = jnp.zeros_like(acc_scr)
    init = tuple((jnp.full((1, TQ), NEG_INF, f32), jnp.zeros((1, TQ), f32)) for _ in qs)
    carry = lax.fori_loop(0, qi, lambda j, c: step(j, c, False), init)
    carry = step(qi, carry, True)
    carry = tuple((m, l, acc_scr[n]) for n, (m, l) in enumerate(carry))

    if ncomp == 2:
        ll = lam_ref[...]
        s1 = jnp.sum(ll[0:1, :] * ll[1:2, :], axis=-1, keepdims=True)
        s2 = jnp.sum(ll[2:3, :] * ll[3:4, :], axis=-1, keepdims=True)
        lam = jnp.exp(s1) - jnp.exp(s2) + LAM_INIT
    for h in range(heads):
        if ncomp == 2:
            (_, l0, a0), (_, l1, a1) = carry[2 * h], carry[2 * h + 1]
            o = a0 / l0 - lam * (a1 / l1)
            ms = jnp.mean(o * o, axis=0, keepdims=True)
            o = o * lax.rsqrt(ms + NORM_EPS) * g_ref[...] * (1.0 - LAM_INIT)
        else:
            _, l0, a0 = carry[h]
            o = a0 / l0
        o_ref[:, h * HEAD_W:(h + 1) * HEAD_W] = o.T.astype(bf16)


def _attention(q_arr, k_arr, v_arr, q_blk0, k_blk0, v_blk0, heads, extra=()):
    ncomp = 2 if extra else 1
    nq = SEQ // TQ
    w = heads * HEAD_W
    qb, kb, vb = q_blk0 // heads, k_blk0 // heads, v_blk0 // heads
    in_specs = [pl.BlockSpec((TQ, w), lambda b, h, i: (b * nq + i, qb + h)),
                pl.BlockSpec((SEQ, w), lambda b, h, i: (b, kb + h)),
                pl.BlockSpec((SEQ, w), lambda b, h, i: (b, vb + h))]
    for a in extra:
        in_specs.append(pl.BlockSpec(a.shape, lambda b, h, i: (0, 0)))
    return pl.pallas_call(
        functools.partial(_attn_kernel, ncomp=ncomp, heads=heads),
        grid=(BATCH, HEADS // heads, nq),
        in_specs=in_specs,
        out_specs=pl.BlockSpec((TQ, w), lambda b, h, i: (b * nq + i, h)),
        out_shape=jax.ShapeDtypeStruct((N_TOK, HEADS * HEAD_W), bf16),
        scratch_shapes=[pltpu.VMEM((heads * ncomp, HEAD_W, TQ), f32),
                        pltpu.VMEM((SEQ // TQ, heads, HEAD_W, TQ), bf16)],
        compiler_params=_cparams(("arbitrary", "arbitrary", "arbitrary")),
        name="diff_attn" if extra else "mla_attn",
    )(q_arr, k_arr, v_arr, *extra)


RT_GROUP_ROW = 0
RT_EXPERT_ROW = 8


def _route(hf, w_ref, b_ref, eid_ref, cw_ref):
    hf_hi = hf.astype(bf16)
    hf_lo = (hf - hf_hi.astype(f32)).astype(bf16)
    a = jnp.dot(hf_hi, w_ref[...], preferred_element_type=f32)
    b = jnp.dot(hf_lo, w_ref[:, :LANES], preferred_element_type=f32)
    logits = a[:, :LANES] + (a[:, LANES:] + b) + b_ref[...]
    lt = logits.T
    tm = lt.shape[1]
    ridx = lax.broadcasted_iota(i32, (EXPERTS_PER_GROUP, tm), 0).astype(f32)

    def top1(v):
        vmax = jnp.max(v, axis=0, keepdims=True)
        idx = jnp.min(jnp.where(v == vmax, ridx, float(EXPERTS_PER_GROUP)), axis=0, keepdims=True)
        return vmax, idx

    gl = lt[RT_GROUP_ROW:RT_GROUP_ROW + 8, :]
    gmax, gsel = top1(gl)
    p_g = 1.0 / jnp.sum(jnp.exp(gl - gmax), axis=0, keepdims=True)
    el = jnp.zeros((EXPERTS_PER_GROUP, tm), f32)
    for g in range(N_GROUPS):
        r0 = RT_EXPERT_ROW + g * EXPERTS_PER_GROUP
        el = jnp.where(gsel == float(g), lt[r0:r0 + EXPERTS_PER_GROUP, :], el)
    e1, i1 = top1(el)
    e2, i2 = top1(jnp.where(ridx == i1, -jnp.inf, el))
    r = jnp.exp(e2 - e1)
    w1 = 1.0 / (1.0 + r)
    base = gsel * EXPERTS_PER_GROUP
    eid_ref[0:1, :] = (base + i1).astype(i32)
    eid_ref[1:2, :] = (base + i2).astype(i32)
    cw_ref[0:1, :] = p_g * w1
    cw_ref[1:2, :] = p_g * (r * w1)


def _merge_kernel(oa_ref, ob_ref, ga_ref, gb_ref, x_ref, wa_ref, wb_ref, wo_ref, gf_ref, wr_ref, br_ref,
                  x1_ref, hf_ref, eid_ref, cw_ref):
    ya = jnp.dot(oa_ref[...], wa_ref[...], preferred_element_type=f32)
    yb = jnp.dot(ob_ref[...], wb_ref[...], preferred_element_type=f32)
    mixed = (jax.nn.sigmoid(ga_ref[...].astype(f32)) * ya
             + jax.nn.sigmoid(gb_ref[...].astype(f32)) * yb)
    x1 = x_ref[...] + jnp.dot(mixed.astype(bf16), wo_ref[...], preferred_element_type=f32)
    x1_ref[...] = x1
    hf = _rms(x1, gf_ref[...])
    _slab_store(hf_ref, (), hf)
    _route(hf, wr_ref, br_ref, eid_ref, cw_ref)


def _merge(oa, ob, proj, x2d, wa, wb, wo, gf, w_rt, b_rt):
    tm = TM_MERGE
    w = HEADS * HEAD_W
    const = lambda shape: pl.BlockSpec(shape, lambda i: (0, 0), pipeline_mode=pl.Buffered(1))
    return pl.pallas_call(
        _merge_kernel,
        grid=(N_TOK // tm,),
        in_specs=[pl.BlockSpec((tm, w), lambda i: (i, 0)),
                  pl.BlockSpec((tm, w), lambda i: (i, 0)),
                  pl.BlockSpec((tm, D_MODEL), lambda i: (i, COL_GA // D_MODEL)),
                  pl.BlockSpec((tm, D_MODEL), lambda i: (i, COL_GB // D_MODEL)),
                  pl.BlockSpec((tm, D_MODEL), lambda i: (i, 0)),
                  const((w, D_MODEL)), const((w, D_MODEL)), const((D_MODEL, D_MODEL)),
                  const((1, D_MODEL)), const((D_MODEL, 2 * LANES)), const((1, LANES))],
        out_specs=[pl.BlockSpec((tm, D_MODEL), lambda i: (i, 0)),
                   pl.BlockSpec((tm * SLAB, LANES), lambda i: (i, 0)),
                   pl.BlockSpec((TOP_K, tm), lambda i: (0, i)),
                   pl.BlockSpec((TOP_K, tm), lambda i: (0, i))],
        out_shape=[jax.ShapeDtypeStruct((N_TOK, D_MODEL), f32),
                   jax.ShapeDtypeStruct((N_TOK * SLAB, LANES), f32),
                   jax.ShapeDtypeStruct((TOP_K, N_TOK), i32),
                   jax.ShapeDtypeStruct((TOP_K, N_TOK), f32)],
        compiler_params=_cparams(("arbitrary",)),
        name="merge_route",
    )(oa, ob, proj, proj, x2d, wa, wb, wo, gf, w_rt, b_rt)


PLAN_N = 128


def _plan_kernel(eid_ref, dest_ref, ends_ref):
    eid = eid_ref[...]
    r = lax.broadcasted_iota(i32, (PLAN_N, PLAN_N), 0)
    c = lax.broadcasted_iota(i32, (PLAN_N, PLAN_N), 1)
    upper = (r < c).astype(bf16)
    lower = (c < r).astype(bf16)
    ones = jnp.ones((PLAN_N, PLAN_N), bf16)
    lane = lax.broadcasted_iota(i32, (1, LANES), 1)
    dest = jnp.zeros((PLAN_N, PLAN_N), f32)
    ends = jnp.zeros((1, LANES), f32)
    start = jnp.zeros((1, 1), f32)
    for e in range(N_EXPERTS):
        m = (eid == e).astype(f32)
        mb = m.astype(bf16)
        within = jnp.dot(mb, upper, preferred_element_type=f32)
        rowtot = jnp.dot(mb, ones, preferred_element_type=f32)
        rowpre = jnp.dot(lower, rowtot.astype(bf16), preferred_element_type=f32)
        cnt = jnp.sum(jnp.sum(m, axis=1, keepdims=True), axis=0, keepdims=True)
        dest = dest + m * (start + within + rowpre)
        start = start + jnp.floor((cnt + (ROW_BLK - 1)) * (1.0 / ROW_BLK)) * ROW_BLK
        ends = jnp.where(lane == e, start, ends)
    dest_ref[...] = dest.astype(i32)
    ends_ref[...] = ends.astype(i32)


def _plan(eid):
    whole = lambda shape: pl.BlockSpec(shape, lambda: (0, 0))
    return pl.pallas_call(
        _plan_kernel,
        in_specs=[whole((PLAN_N, PLAN_N))],
        out_specs=[whole((PLAN_N, PLAN_N)), whole((1, LANES))],
        out_shape=[jax.ShapeDtypeStruct((PLAN_N, PLAN_N), i32),
                   jax.ShapeDtypeStruct((1, LANES), i32)],
        name="moe_plan",
    )(eid.reshape(PLAN_N, PLAN_N))


def _dispatch_kernel(ends_ref, dest_ref, hf_ref, xs_hbm, zbuf, sem):
    i = pl.program_id(0)

    def tail_copy(e):
        pend = ends_ref[e]
        prev = jnp.where(e > 0, ends_ref[jnp.maximum(e - 1, 0)], 0)
        row0 = pl.multiple_of(jnp.maximum(pend - ROW_BLK, 0), ROW_BLK)
        return pend > prev, pltpu.make_async_copy(zbuf, xs_hbm.at[_slab_rows(row0, ROW_BLK), :], sem)

    @pl.when(i == 0)
    def _():
        zbuf[...] = jnp.zeros_like(zbuf)

        def zstart(e, _):
            used, cp = tail_copy(e)

            @pl.when(used)
            def _():
                cp.start()
            return 0

        def zwait(e, _):
            used, cp = tail_copy(e)

            @pl.when(used)
            def _():
                cp.wait()
            return 0

        def unused_copy(b):
            row0 = pl.multiple_of(b * ROW_BLK, ROW_BLK)
            return pltpu.make_async_copy(zbuf, xs_hbm.at[_slab_rows(row0, ROW_BLK), :], sem)

        def ustart(b, _):
            unused_copy(b).start()
            return 0

        def uwait(b, _):
            unused_copy(b).wait()
            return 0

        n_used = ends_ref[N_EXPERTS - 1] // ROW_BLK
        lax.fori_loop(0, N_EXPERTS, zstart, 0)
        lax.fori_loop(n_used, N_BLKS, ustart, 0)
        lax.fori_loop(0, N_EXPERTS, zwait, 0)
        lax.fori_loop(n_used, N_BLKS, uwait, 0)

    base = i * TT_DISP

    def row_copy(t, k):
        d = dest_ref[k, base + t]
        return pltpu.make_async_copy(hf_ref.at[_slab_rows(t, 1), :], xs_hbm.at[_slab_rows(d, 1), :], sem)

    def issue(t, _):
        for k in range(TOP_K):
            row_copy(t, k).start()
        return 0

    def drain(t, _):
        for k in range(TOP_K):
            row_copy(t, k).wait()
        return 0

    lax.fori_loop(0, TT_DISP, issue, 0, unroll=ISSUE_UNROLL)
    lax.fori_loop(0, TT_DISP, drain, 0, unroll=ISSUE_UNROLL)


def _dispatch(ends, dest, hf):
    return pl.pallas_call(
        _dispatch_kernel,
        grid_spec=pltpu.PrefetchScalarGridSpec(
            num_scalar_prefetch=2,
            grid=(N_TOK // TT_DISP,),
            in_specs=[pl.BlockSpec((TT_DISP * SLAB, LANES), lambda i, e, d: (i, 0))],
            out_specs=pl.BlockSpec(memory_space=pl.ANY),
            scratch_shapes=[pltpu.VMEM((ROW_BLK * SLAB, LANES), f32),
                            pltpu.SemaphoreType.DMA(())]),
        out_shape=jax.ShapeDtypeStruct((N_ROWS * SLAB, LANES), f32),
        compiler_params=_cparams(("arbitrary",)),
        name="moe_dispatch",
    )(ends, dest, hf)


W_RING = 3
X_RING = 6
Y_RING = 3


def _ffn_kernel(ends_ref, wg_hbm, wu_hbm, wd_hbm, xs_hbm, y_hbm,
                wg_f, wu_f, wd_f, wg_s, wu_s, wd_s, xbuf, ybuf, sem_w, sem_x, sem_y):
    n_used = ends_ref[N_EXPERTS - 1] // ROW_BLK

    def w_copies(ex):
        slot = ex % W_RING
        return [pltpu.make_async_copy(hbm.at[ex], buf.at[slot], sem_w.at[i, slot])
                for i, (hbm, buf) in enumerate(((wg_hbm, wg_f), (wu_hbm, wu_f), (wd_hbm, wd_f)))]

    def advance(e_from, e_to):
        def one(ex, _):
            for cp in w_copies(ex):
                cp.wait()

            @pl.when(ex + (W_RING - 1) < N_EXPERTS)
            def _():
                for cp in w_copies(ex + (W_RING - 1)):
                    cp.start()
            return 0

        lax.fori_loop(e_from + 1, e_to + 1, one, 0)

    def x_copy(b):
        slot = b % X_RING
        return pltpu.make_async_copy(xs_hbm.at[_slab_rows(b * ROW_BLK, ROW_BLK), :], xbuf.at[slot],
                                     sem_x.at[slot])

    def y_copy(b):
        slot = b % Y_RING
        return pltpu.make_async_copy(ybuf.at[slot], y_hbm.at[_slab_rows(b * ROW_BLK, ROW_BLK), :],
                                     sem_y.at[slot])

    for ex in range(W_RING - 1):
        for cp in w_copies(ex):
            cp.start()
    for b0 in range(X_RING - 1):
        @pl.when(b0 < n_used)
        def _():
            x_copy(b0).start()

    def body(b, e):
        @pl.when(b + (X_RING - 1) < n_used)
        def _():
            x_copy(b + (X_RING - 1)).start()

        e_new = lax.while_loop(lambda ex: b * ROW_BLK >= ends_ref[ex], lambda ex: ex + 1,
                               jnp.maximum(e, 0))

        @pl.when(e_new != e)
        def _():
            advance(e, e_new)
            wslot = e_new % W_RING
            wg_s[...] = wg_f[wslot].astype(bf16)
            wu_s[...] = wu_f[wslot].astype(bf16)
            wd_s[...] = wd_f[wslot].astype(bf16)

        x_copy(b).wait()

        @pl.when(b >= Y_RING)
        def _():
            y_copy(b - Y_RING).wait()

        x = _slab_load(xbuf, (b % X_RING,), ROW_BLK).astype(bf16)
        h1 = jnp.dot(x, wg_s[...], preferred_element_type=f32)
        h2 = jnp.dot(x, wu_s[...], preferred_element_type=f32)
        a = (h1 * jax.nn.sigmoid(h1) * h2).astype(bf16)
        _slab_store(ybuf, (b % Y_RING,), jnp.dot(a, wd_s[...], preferred_element_type=f32))
        y_copy(b).start()
        return e_new

    e_last = lax.fori_loop(0, n_used, body, jnp.int32(-1))
    advance(e_last, jnp.int32(N_EXPERTS - 1))

    for back in range(Y_RING, 0, -1):
        @pl.when(n_used >= back)
        def _():
            y_copy(n_used - back).wait()

    ybuf[0] = jnp.zeros((ROW_BLK * SLAB, LANES), f32)

    def unused_copy(b):
        return pltpu.make_async_copy(ybuf.at[0], y_hbm.at[_slab_rows(b * ROW_BLK, ROW_BLK), :],
                                     sem_y.at[0])

    def ustart(b, _):
        unused_copy(b).start()
        return 0

    def uwait(b, _):
        unused_copy(b).wait()
        return 0

    lax.fori_loop(n_used, N_BLKS, ustart, 0)
    lax.fori_loop(n_used, N_BLKS, uwait, 0)


def _expert_ffn(ends, xs, wg, wu, wd):
    return pl.pallas_call(
        _ffn_kernel,
        grid_spec=pltpu.PrefetchScalarGridSpec(
            num_scalar_prefetch=1,
            grid=(1,),
            in_specs=[pl.BlockSpec(memory_space=pl.ANY)] * 4,
            out_specs=pl.BlockSpec(memory_space=pl.ANY),
            scratch_shapes=[pltpu.VMEM((W_RING, D_MODEL, EXPERT_FF), f32),
                            pltpu.VMEM((W_RING, D_MODEL, EXPERT_FF), f32),
                            pltpu.VMEM((W_RING, EXPERT_FF, D_MODEL), f32),
                            pltpu.VMEM((D_MODEL, EXPERT_FF), bf16),
                            pltpu.VMEM((D_MODEL, EXPERT_FF), bf16),
                            pltpu.VMEM((EXPERT_FF, D_MODEL), bf16),
                            pltpu.VMEM((X_RING, ROW_BLK * SLAB, LANES), f32),
                            pltpu.VMEM((Y_RING, ROW_BLK * SLAB, LANES), f32),
                            pltpu.SemaphoreType.DMA((3, W_RING)),
                            pltpu.SemaphoreType.DMA((X_RING,)),
                            pltpu.SemaphoreType.DMA((Y_RING,))]),
        out_shape=jax.ShapeDtypeStruct((N_ROWS * SLAB, LANES), f32),
        compiler_params=_cparams(("arbitrary",)),
        name="expert_ffn",
    )(ends, wg, wu, wd, xs)


def _combine_kernel(dest_ref, x_ref, cw_ref, g_ref, y_hbm, o_ref, ybuf, sem):
    i = pl.program_id(0)
    n_steps = pl.num_programs(0)

    def row_copy(step, t, k):
        slot = step % 2
        d = dest_ref[k, step * TM_COMB + t]
        return pltpu.make_async_copy(y_hbm.at[_slab_rows(d, 1), :],
                                     ybuf.at[slot, k, _slab_rows(t, 1), :], sem.at[slot])

    def issue(step):
        def one(t, _):
            for k in range(TOP_K):
                row_copy(step, t, k).start()
            return 0
        lax.fori_loop(0, TM_COMB, one, 0, unroll=ISSUE_UNROLL)

    def drain(step):
        def one(t, _):
            for k in range(TOP_K):
                row_copy(step, t, k).wait()
            return 0
        lax.fori_loop(0, TM_COMB, one, 0, unroll=ISSUE_UNROLL)

    @pl.when(i == 0)
    def _():
        issue(i)

    @pl.when(i + 1 < n_steps)
    def _():
        issue(i + 1)

    drain(i)
    slot = i % 2
    w = cw_ref[...]
    x = (x_ref[...] + w[:, 0:1] * _slab_load(ybuf, (slot, 0), TM_COMB)
         + w[:, 1:2] * _slab_load(ybuf, (slot, 1), TM_COMB))
    o_ref[...] = _rms(x, g_ref[...])


def _combine(dest, x1, cw_t, g, y):
    tm = TM_COMB
    return pl.pallas_call(
        _combine_kernel,
        grid_spec=pltpu.PrefetchScalarGridSpec(
            num_scalar_prefetch=1,
            grid=(N_TOK // tm,),
            in_specs=[pl.BlockSpec((tm, D_MODEL), lambda i, d: (i, 0)),
                      pl.BlockSpec((tm, TOP_K), lambda i, d: (i, 0)),
                      pl.BlockSpec((1, D_MODEL), lambda i, d: (0, 0)),
                      pl.BlockSpec(memory_space=pl.ANY)],
            out_specs=pl.BlockSpec((tm, D_MODEL), lambda i, d: (i, 0)),
            scratch_shapes=[pltpu.VMEM((2, TOP_K, tm * SLAB, LANES), f32),
                            pltpu.SemaphoreType.DMA((2,))]),
        out_shape=jax.ShapeDtypeStruct((N_TOK, D_MODEL), f32),
        compiler_params=_cparams(("arbitrary",)),
        name="moe_combine",
    )(dest, x1, cw_t, g, y)


TR_RELAYOUT = 256
KR_TILE = COL_KR // TR_RELAYOUT


def _w_in_relayout_kernel(w_ref, o_ref):
    i = pl.program_id(0)

    @pl.when(i < COL_V // TR_RELAYOUT)
    def _():
        q = ROPE_NA
        for g in range(TR_RELAYOUT // LANES):
            r = g * LANES
            o_ref[r:r + q, :] = w_ref[r:r + q, :].astype(bf16)
            o_ref[r + q:r + 2 * q, :] = w_ref[r + 2 * q:r + 3 * q, :].astype(bf16)
            o_ref[r + 2 * q:r + 3 * q, :] = w_ref[r + q:r + 2 * q, :].astype(bf16)
            o_ref[r + 3 * q:r + 4 * q, :] = w_ref[r + 3 * q:r + 4 * q, :].astype(bf16)

    @pl.when((i >= COL_V // TR_RELAYOUT) & (i != KR_TILE))
    def _():
        o_ref[...] = w_ref[...].astype(bf16)

    @pl.when(i == KR_TILE)
    def _():
        o_ref[...] = jnp.zeros_like(o_ref)
        o_ref[KR_LANE:KR_LANE + MLA_ROPE, :] = w_ref[0:MLA_ROPE, :].astype(bf16)


def _prep_w_in(w_in_t):
    n_src, d = w_in_t.shape
    gate_src = COL_KR + MLA_ROPE

    gate_tile = COL_GA // TR_RELAYOUT
    sub = 8
    shift = (COL_GA - gate_src) // sub

    def src_row(i):
        return ((TR_RELAYOUT // sub) * i - shift * (i // gate_tile)) * sub

    return pl.pallas_call(
        _w_in_relayout_kernel,
        grid=(PROJ_W // TR_RELAYOUT,),
        in_specs=[pl.BlockSpec((pl.Element(TR_RELAYOUT), pl.Element(d)), lambda i: (src_row(i), 0))],
        out_specs=pl.BlockSpec((TR_RELAYOUT, d), lambda i: (i, 0)),
        out_shape=jax.ShapeDtypeStruct((PROJ_W, d), bf16),
        compiler_params=_cparams(("arbitrary",)),
        name="w_in_relayout",
    )(w_in_t)


def _prep_w_uq(w_uq):
    w = w_uq.reshape(MLA_Q_RANK, HEADS, MLA_NOPE + MLA_ROPE)
    w = jnp.pad(w, ((0, 0), (0, 0), (0, HEAD_W - MLA_NOPE - MLA_ROPE)))
    return w.reshape(MLA_Q_RANK, HEADS * HEAD_W).astype(bf16)


def _prep_w_ukv(w_ukv):
    w = w_ukv.reshape(MLA_KV_RANK, HEADS, MLA_NOPE + HEAD_W)
    wk = jnp.pad(w[:, :, :MLA_NOPE], ((0, 0), (0, 0), (0, HEAD_W - MLA_NOPE)))
    wv = w[:, :, MLA_NOPE:]
    return jnp.concatenate([wk.reshape(MLA_KV_RANK, -1), wv.reshape(MLA_KV_RANK, -1)], axis=1).astype(bf16)


def _prep_router(w_group, b_group, w_router, b_router):
    z = lambda n: jnp.zeros((D_MODEL, n), f32)
    w = jnp.concatenate([w_group, z(RT_EXPERT_ROW - N_GROUPS), w_router,
                         z(LANES - RT_EXPERT_ROW - N_EXPERTS)], axis=1)
    b = jnp.concatenate([b_group, jnp.full((RT_EXPERT_ROW - N_GROUPS,), NEG_INF, f32), b_router,
                         jnp.zeros((LANES - RT_EXPERT_ROW - N_EXPERTS,), f32)])[None, :]
    w_hi = w.astype(bf16)
    w_lo = (w - w_hi.astype(f32)).astype(bf16)
    return jnp.concatenate([w_hi, w_lo], axis=1), b


def kernel(x, positions, attn_norm, w_in, da_lambda, da_subln, mla_q_norm, mla_w_uq, mla_kv_norm, mla_w_ukv, w_branch_a, w_branch_b, w_out, ffn_norm, w_group, b_group, w_router, b_router, w_exp_gate, w_exp_up, w_exp_down, final_norm):
    x2d = x.reshape(N_TOK, D_MODEL)
    cos_t, sin_t = _rope_tables(positions)

    proj = _in_proj(x2d, attn_norm[0][None, :], _prep_w_in(w_in[0].T), cos_t, sin_t)
    qb, kb, vb = _mla_prep(proj, mla_q_norm[0][None, :], mla_kv_norm[0][None, :],
                           _prep_w_uq(mla_w_uq[0]), _prep_w_ukv(mla_w_ukv[0]), cos_t, sin_t)
    oa = _attention(proj, proj, proj, COL_Q // HEAD_W, COL_K // HEAD_W, COL_V // HEAD_W,
                    heads=8, extra=(da_lambda[0], da_subln[0][:, None]))
    ob = _attention(qb, kb, vb, 0, 0, 0, heads=8)
    w_rt, b_rt = _prep_router(w_group[0], b_group[0], w_router[0], b_router[0])
    x1, hf, eid, cw = _merge(oa, ob, proj, x2d, w_branch_a[0].astype(bf16), w_branch_b[0].astype(bf16),
                             w_out[0].astype(bf16), ffn_norm[0][None, :], w_rt, b_rt)

    dest, ends = _plan(eid)
    dest = dest.reshape(TOP_K, N_TOK)
    ends = ends[0, :N_EXPERTS]
    xs = _dispatch(ends, dest, hf)
    y = _expert_ffn(ends, xs, w_exp_gate[0], w_exp_up[0], w_exp_down[0])
    out = _combine(dest, x1, cw.T, final_norm[None, :], y)
    return out.reshape(BATCH, SEQ, D_MODEL)
```

```python
import functools
import math

import jax
import jax.numpy as jnp
from jax import lax
from jax.experimental import pallas as pl
from jax.experimental.pallas import tpu as pltpu

f32 = jnp.float32
bf16 = jnp.bfloat16
i32 = jnp.int32

D_MODEL = 2048
BATCH = 4
SEQ = 2048
N_TOK = BATCH * SEQ
HEADS = 8
HEAD_W = 128
DA_HALF = 64
MLA_Q_RANK = 512
MLA_KV_RANK = 256
MLA_NOPE = 64
MLA_ROPE = 32
N_GROUPS = 4
EXPERTS_PER_GROUP = 8
N_EXPERTS = N_GROUPS * EXPERTS_PER_GROUP
TOP_K = 2
EXPERT_FF = 512
ROPE_THETA = 10000.0
NORM_EPS = 1e-6
NEG_INF = -1e30
LAM_INIT = 0.8 - 0.6 * math.exp(-0.3 * 0)
LOG2E = math.log2(math.e)

LANES = 128
VMEM_LIMIT = 56 * 1024 * 1024

COL_Q, COL_K, COL_V = 0, 1024, 2048
COL_CQ, COL_CKV = 3072, 3584
COL_KR = 3840
COL_GA, COL_GB = 4096, 6144
PROJ_W = 8192
KR_LANE = MLA_NOPE

TM_PROJ, TN_PROJ = 1024, 1024
TM_PREP = 512
TQ = 256
TM_MERGE = 256
ROW_BLK = 128
N_ROWS = (N_TOK * TOP_K // ROW_BLK + N_EXPERTS) * ROW_BLK
N_BLKS = N_ROWS // ROW_BLK
TT_DISP = 1024
TM_COMB = 256

Q_SCALE_A = (DA_HALF ** -0.5) * LOG2E
Q_SCALE_B = ((MLA_NOPE + MLA_ROPE) ** -0.5) * LOG2E


def _cparams(sem, **kw):
    return pltpu.CompilerParams(dimension_semantics=sem, vmem_limit_bytes=VMEM_LIMIT, **kw)


def _rms(x, g):
    return x * lax.rsqrt(jnp.mean(x * x, axis=-1, keepdims=True) + NORM_EPS) * g


SLAB = D_MODEL // LANES


ISSUE_UNROLL = 8


def _slab_rows(row0, nrows):
    return pl.ds(pl.multiple_of(row0 * SLAB, SLAB), nrows * SLAB)


def _slab_store(ref, idx, val):
    rows = val.shape[0]
    for j in range(SLAB):
        ref[idx + (pl.ds(j, rows, stride=SLAB), slice(None))] = val[:, j * LANES:(j + 1) * LANES]


def _slab_chunk(ref, idx, rows, j):
    return ref[idx + (pl.ds(j, rows, stride=SLAB), slice(None))]


def _slab_load(ref, idx, rows):
    return jnp.concatenate([_slab_chunk(ref, idx, rows, j) for j in range(SLAB)], axis=1)


def _rope_lanes(blk, cos, sin, first_half, half):
    up = pltpu.roll(blk, LANES - half, 1)
    down = pltpu.roll(blk, half, 1)
    return blk * cos + jnp.where(first_half, up, down) * sin


ROPE_NA = DA_HALF // 2
ROPE_NB = MLA_ROPE // 2


def _rope_tables_kernel(pos_ref, inv_ref, cos_ref, sin_ref):
    half = pos_ref.shape[0] // 2
    lane = lax.broadcasted_iota(i32, (half, LANES), 1)
    pos = jnp.where(lane < LANES // 2, pos_ref[:half, :], pos_ref[half:, :]).astype(f32)
    ang = pos * inv_ref[...]
    c = jnp.cos(ang)
    s = jnp.sin(ang)
    one = lambda n: jnp.ones((half, n), f32)
    zero = lambda n: jnp.zeros((half, n), f32)
    tail = LANES - KR_LANE - MLA_ROPE
    for part in range(2):
        o = part * (LANES // 2)
        rows = slice(part * half, (part + 1) * half)
        ca, sa = c[:, o:o + ROPE_NA], s[:, o:o + ROPE_NA]
        cb, sb = c[:, o + ROPE_NA:o + ROPE_NA + ROPE_NB], s[:, o + ROPE_NA:o + ROPE_NA + ROPE_NB]
        cos_ref[rows, :] = jnp.concatenate([ca, ca, ca, ca, one(KR_LANE), cb, cb, one(tail)], axis=1)
        sin_ref[rows, :] = jnp.concatenate([-sa, -sa, sa, sa, zero(KR_LANE), -sb, sb, zero(tail)],
                                           axis=1)


def _rope_tables(positions):
    inv_a = ROPE_THETA ** (-jnp.arange(0, DA_HALF, 2, dtype=f32) / DA_HALF)
    inv_b = ROPE_THETA ** (-jnp.arange(0, MLA_ROPE, 2, dtype=f32) / MLA_ROPE)
    inv = jnp.concatenate([inv_a, inv_b, jnp.zeros((LANES // 2 - ROPE_NA - ROPE_NB,), f32)])
    inv = jnp.tile(inv, 2)[None, :]
    pos = positions.reshape(N_TOK, 1)
    tm = 1024
    return pl.pallas_call(
        _rope_tables_kernel,
        grid=(N_TOK // tm,),
        in_specs=[pl.BlockSpec((tm, 1), lambda i: (i, 0)),
                  pl.BlockSpec((1, LANES), lambda i: (0, 0))],
        out_specs=[pl.BlockSpec((tm, 2 * LANES), lambda i: (i, 0)),
                   pl.BlockSpec((tm, 2 * LANES), lambda i: (i, 0))],
        out_shape=[jax.ShapeDtypeStruct((N_TOK, 2 * LANES), f32)] * 2,
        compiler_params=_cparams(("arbitrary",)),
        name="rope_tables",
    )(pos, inv)


def _in_proj_kernel(x_ref, g_ref, w_ref, cos_ref, sin_ref, o_ref, h_scr):
    n = pl.program_id(1)

    @pl.when(n == 0)
    def _():
        h_scr[...] = _rms(x_ref[...], g_ref[...]).astype(bf16)

    acc = lax.dot_general(h_scr[...], w_ref[...], (((1,), (1,)), ((), ())),
                          preferred_element_type=f32)
    n_rope = COL_V // TN_PROJ
    n_q = COL_K // TN_PROJ

    @pl.when(n < n_rope)
    def _():
        c = jnp.where(n < n_q, Q_SCALE_A, 1.0).astype(f32)
        cos = cos_ref[...] * c
        sin = sin_ref[...] * c
        for j in range(TN_PROJ // LANES):
            blk = acc[:, j * LANES:(j + 1) * LANES]
            rot = pltpu.roll(blk, LANES // 2, 1)
            o_ref[:, j * LANES:(j + 1) * LANES] = (blk * cos + rot * sin).astype(bf16)

    @pl.when(n >= n_rope)
    def _():
        o_ref[...] = acc.astype(bf16)


def _in_proj(x2d, g, w_pad, cos_t, sin_t):
    return pl.pallas_call(
        _in_proj_kernel,
        grid=(N_TOK // TM_PROJ, PROJ_W // TN_PROJ),
        in_specs=[pl.BlockSpec((TM_PROJ, D_MODEL), lambda i, n: (i, 0)),
                  pl.BlockSpec((1, D_MODEL), lambda i, n: (0, 0)),
                  pl.BlockSpec((TN_PROJ, D_MODEL), lambda i, n: (n, 0)),
                  pl.BlockSpec((TM_PROJ, LANES), lambda i, n: (i, 0)),
                  pl.BlockSpec((TM_PROJ, LANES), lambda i, n: (i, 0))],
        out_specs=pl.BlockSpec((TM_PROJ, TN_PROJ), lambda i, n: (i, n)),
        out_shape=jax.ShapeDtypeStruct((N_TOK, PROJ_W), bf16),
        scratch_shapes=[pltpu.VMEM((TM_PROJ, D_MODEL), bf16)],
        compiler_params=_cparams(("arbitrary", "arbitrary")),
        name="in_proj",
    )(x2d, g, w_pad, cos_t, sin_t)


def _mla_prep_kernel(cq_ref, ckv_ref, kr_ref, gq_ref, gkv_ref, wuq_ref, wkv_ref, cos_ref, sin_ref,
                     q_ref, k_ref, v_ref):
    cqn = _rms(cq_ref[...].astype(f32), gq_ref[...]).astype(bf16)
    q = jnp.dot(cqn, wuq_ref[...], preferred_element_type=f32)
    ckvn = _rms(ckv_ref[...].astype(f32), gkv_ref[...]).astype(bf16)
    kv = jnp.dot(ckvn, wkv_ref[...], preferred_element_type=f32)
    cos = cos_ref[...]
    sin = sin_ref[...]
    lane = lax.broadcasted_iota(i32, (TM_PREP, LANES), 1)
    first = lane < (KR_LANE + MLA_ROPE // 2)
    rope = lambda b: _rope_lanes(b, cos, sin, first, MLA_ROPE // 2)
    k_rot = rope(kr_ref[...].astype(f32))
    for h in range(HEADS):
        sl = slice(h * HEAD_W, (h + 1) * HEAD_W)
        q_ref[:, sl] = (rope(q[:, sl]) * Q_SCALE_B).astype(bf16)
        k_ref[:, sl] = (kv[:, sl] + k_rot).astype(bf16)
    v_ref[...] = kv[:, HEADS * HEAD_W:].astype(bf16)


def _mla_prep(proj, gq, gkv, wuq_p, wkv_p, cos_t, sin_t):
    tm = TM_PREP
    w = HEADS * HEAD_W
    return pl.pallas_call(
        _mla_prep_kernel,
        grid=(N_TOK // tm,),
        in_specs=[pl.BlockSpec((tm, MLA_Q_RANK), lambda i: (i, COL_CQ // MLA_Q_RANK)),
                  pl.BlockSpec((tm, MLA_KV_RANK), lambda i: (i, COL_CKV // MLA_KV_RANK)),
                  pl.BlockSpec((tm, LANES), lambda i: (i, COL_KR // LANES)),
                  pl.BlockSpec((1, MLA_Q_RANK), lambda i: (0, 0)),
                  pl.BlockSpec((1, MLA_KV_RANK), lambda i: (0, 0)),
                  pl.BlockSpec((MLA_Q_RANK, w), lambda i: (0, 0)),
                  pl.BlockSpec((MLA_KV_RANK, 2 * w), lambda i: (0, 0)),
                  pl.BlockSpec((tm, LANES), lambda i: (i, 1)),
                  pl.BlockSpec((tm, LANES), lambda i: (i, 1))],
        out_specs=[pl.BlockSpec((tm, w), lambda i: (i, 0))] * 3,
        out_shape=[jax.ShapeDtypeStruct((N_TOK, w), bf16)] * 3,
        compiler_params=_cparams(("arbitrary",)),
        name="mla_prep",
    )(proj, proj, proj, gq, gkv, wuq_p, wkv_p, cos_t, sin_t)


def _attn_kernel(*refs, ncomp, heads):
    if ncomp == 2:
        q_ref, k_ref, v_ref, lam_ref, g_ref, o_ref, acc_scr, vt_scr = refs
    else:
        q_ref, k_ref, v_ref, o_ref, acc_scr, vt_scr = refs
    qi = pl.program_id(2)

    @pl.when(qi == 0)
    def _():
        for j in range(SEQ // TQ):
            for h in range(heads):
                vt_scr[j, h] = v_ref[j * TQ:(j + 1) * TQ, h * HEAD_W:(h + 1) * HEAD_W].T

    lane = lax.broadcasted_iota(i32, (TQ, HEAD_W), 1)
    qs = []
    for h in range(heads):
        q = q_ref[:, h * HEAD_W:(h + 1) * HEAD_W]
        if ncomp == 2:
            zero = jnp.zeros_like(q)
            in_a = (lane % DA_HALF) < (DA_HALF // 2)
            qs += [jnp.where(in_a, q, zero), jnp.where(in_a, zero, q)]
        else:
            qs.append(q)

    def step(j, carry, masked):
        off = pl.multiple_of(j * TQ, TQ)
        hsl = lambda n: slice((n // ncomp) * HEAD_W, (n // ncomp + 1) * HEAD_W)
        ss = []
        for n, qc in enumerate(qs):
            k = k_ref[pl.ds(off, TQ), hsl(n)]
            s = lax.dot_general(k, qc, (((1,), (1,)), ((), ())), preferred_element_type=f32)
            if masked:
                key = lax.broadcasted_iota(i32, (TQ, TQ), 0)
                qry = lax.broadcasted_iota(i32, (TQ, TQ), 1)
                s = jnp.where(key <= qry, s, NEG_INF)
            ss.append(s)
        stats = []
        for n, s in enumerate(ss):
            m, l = carry[n]
            m_new = jnp.maximum(m, jnp.max(s, axis=0, keepdims=True))
            alpha = jnp.exp2(m - m_new)
            p = jnp.exp2(s - m_new)
            l_new = alpha * l + jnp.sum(p, axis=0, keepdims=True)
            stats.append((m_new, l_new, alpha, p.astype(bf16)))
        out = []
        for n, (m_new, l_new, alpha, p) in enumerate(stats):
            pv = jnp.dot(vt_scr[j, n // ncomp], p, preferred_element_type=f32)
            acc_scr[n] = alpha * acc_scr[n] + pv
            out.append((m_new, l_new))
        return tuple(out)

    acc_scr[...] = jnp.zeros_like(acc_scr)
    init = tuple((jnp.full((1, TQ), NEG_INF, f32), jnp.zeros((1, TQ), f32)) for _ in qs)
    carry = lax.fori_loop(0, qi, lambda j, c: step(j, c, False), init)
    carry = step(qi, carry, True)
    carry = tuple((m, l, acc_scr[n]) for n, (m, l) in enumerate(carry))

    if ncomp == 2:
        ll = lam_ref[...]
        s1 = jnp.sum(ll[0:1, :] * ll[1:2, :], axis=-1, keepdims=True)
        s2 = jnp.sum(ll[2:3, :] * ll[3:4, :], axis=-1, keepdims=True)
        lam = jnp.exp(s1) - jnp.exp(s2) + LAM_INIT
    for h in range(heads):
        if ncomp == 2:
            (_, l0, a0), (_, l1, a1) = carry[2 * h], carry[2 * h + 1]
            o = a0 / l0 - lam * (a1 / l1)
            ms = jnp.mean(o * o, axis=0, keepdims=True)
            o = o * lax.rsqrt(ms + NORM_EPS) * g_ref[...] * (1.0 - LAM_INIT)
        else:
            _, l0, a0 = carry[h]
            o = a0 / l0
        o_ref[:, h * HEAD_W:(h + 1) * HEAD_W] = o.T.astype(bf16)


def _attention(q_arr, k_arr, v_arr, q_blk0, k_blk0, v_blk0, heads, extra=()):
    ncomp = 2 if extra else 1
    nq = SEQ // TQ
    w = heads * HEAD_W
    qb, kb, vb = q_blk0 // heads, k_blk0 // heads, v_blk0 // heads
    in_specs = [pl.BlockSpec((TQ, w), lambda b, h, i: (b * nq + i, qb + h)),
                pl.BlockSpec((SEQ, w), lambda b, h, i: (b, kb + h)),
                pl.BlockSpec((SEQ, w), lambda b, h, i: (b, vb + h))]
    for a in extra:
        in_specs.append(pl.BlockSpec(a.shape, lambda b, h, i: (0, 0)))
    return pl.pallas_call(
        functools.partial(_attn_kernel, ncomp=ncomp, heads=heads),
        grid=(BATCH, HEADS // heads, nq),
        in_specs=in_specs,
        out_specs=pl.BlockSpec((TQ, w), lambda b, h, i: (b * nq + i, h)),
        out_shape=jax.ShapeDtypeStruct((N_TOK, HEADS * HEAD_W), bf16),
        scratch_shapes=[pltpu.VMEM((heads * ncomp, HEAD_W, TQ), f32),
                        pltpu.VMEM((SEQ // TQ, heads, HEAD_W, TQ), bf16)],
        compiler_params=_cparams(("arbitrary", "arbitrary", "arbitrary")),
        name="diff_attn" if extra else "mla_attn",
    )(q_arr, k_arr, v_arr, *extra)


RT_GROUP_ROW = 0
RT_EXPERT_ROW = 8


def _route(hf, w_ref, b_ref, eid_ref, cw_ref):
    hf_hi = hf.astype(bf16)
    hf_lo = (hf - hf_hi.astype(f32)).astype(bf16)
    a = jnp.dot(hf_hi, w_ref[...], preferred_element_type=f32)
    b = jnp.dot(hf_lo, w_ref[:, :LANES], preferred_element_type=f32)
    logits = a[:, :LANES] + (a[:, LANES:] + b) + b_ref[...]
    lt = logits.T
    tm = lt.shape[1]
    ridx = lax.broadcasted_iota(i32, (EXPERTS_PER_GROUP, tm), 0).astype(f32)

    def top1(v):
        vmax = jnp.max(v, axis=0, keepdims=True)
        idx = jnp.min(jnp.where(v == vmax, ridx, float(EXPERTS_PER_GROUP)), axis=0, keepdims=True)
        return vmax, idx

    gl = lt[RT_GROUP_ROW:RT_GROUP_ROW + 8, :]
    gmax, gsel = top1(gl)
    p_g = 1.0 / jnp.sum(jnp.exp(gl - gmax), axis=0, keepdims=True)
    el = jnp.zeros((EXPERTS_PER_GROUP, tm), f32)
    for g in range(N_GROUPS):
        r0 = RT_EXPERT_ROW + g * EXPERTS_PER_GROUP
        el = jnp.where(gsel == float(g), lt[r0:r0 + EXPERTS_PER_GROUP, :], el)
    e1, i1 = top1(el)
    e2, i2 = top1(jnp.where(ridx == i1, -jnp.inf, el))
    r = jnp.exp(e2 - e1)
    w1 = 1.0 / (1.0 + r)
    base = gsel * EXPERTS_PER_GROUP
    eid_ref[0:1, :] = (base + i1).astype(i32)
    eid_ref[1:2, :] = (base + i2).astype(i32)
    cw_ref[0:1, :] = p_g * w1
    cw_ref[1:2, :] = p_g * (r * w1)


def _merge_kernel(oa_ref, ob_ref, ga_ref, gb_ref, x_ref, wa_ref, wb_ref, wo_ref, gf_ref, wr_ref, br_ref,
                  x1_ref, hf_ref, eid_ref, cw_ref):
    ya = jnp.dot(oa_ref[...], wa_ref[...], preferred_element_type=f32)
    yb = jnp.dot(ob_ref[...], wb_ref[...], preferred_element_type=f32)
    mixed = (jax.nn.sigmoid(ga_ref[...].astype(f32)) * ya
             + jax.nn.sigmoid(gb_ref[...].astype(f32)) * yb)
    x1 = x_ref[...] + jnp.dot(mixed.astype(bf16), wo_ref[...], preferred_element_type=f32)
    x1_ref[...] = x1
    hf = _rms(x1, gf_ref[...])
    _slab_store(hf_ref, (), hf)
    _route(hf, wr_ref, br_ref, eid_ref, cw_ref)


def _merge(oa, ob, proj, x2d, wa, wb, wo, gf, w_rt, b_rt):
    tm = TM_MERGE
    w = HEADS * HEAD_W
    const = lambda shape: pl.BlockSpec(shape, lambda i: (0, 0), pipeline_mode=pl.Buffered(1))
    return pl.pallas_call(
        _merge_kernel,
        grid=(N_TOK // tm,),
        in_specs=[pl.BlockSpec((tm, w), lambda i: (i, 0)),
                  pl.BlockSpec((tm, w), lambda i: (i, 0)),
                  pl.BlockSpec((tm, D_MODEL), lambda i: (i, COL_GA // D_MODEL)),
                  pl.BlockSpec((tm, D_MODEL), lambda i: (i, COL_GB // D_MODEL)),
                  pl.BlockSpec((tm, D_MODEL), lambda i: (i, 0)),
                  const((w, D_MODEL)), const((w, D_MODEL)), const((D_MODEL, D_MODEL)),
                  const((1, D_MODEL)), const((D_MODEL, 2 * LANES)), const((1, LANES))],
        out_specs=[pl.BlockSpec((tm, D_MODEL), lambda i: (i, 0)),
                   pl.BlockSpec((tm * SLAB, LANES), lambda i: (i, 0)),
                   pl.BlockSpec((TOP_K, tm), lambda i: (0, i)),
                   pl.BlockSpec((TOP_K, tm), lambda i: (0, i))],
        out_shape=[jax.ShapeDtypeStruct((N_TOK, D_MODEL), f32),
                   jax.ShapeDtypeStruct((N_TOK * SLAB, LANES), f32),
                   jax.ShapeDtypeStruct((TOP_K, N_TOK), i32),
                   jax.ShapeDtypeStruct((TOP_K, N_TOK), f32)],
        compiler_params=_cparams(("arbitrary",)),
        name="merge_route",
    )(oa, ob, proj, proj, x2d, wa, wb, wo, gf, w_rt, b_rt)


PLAN_N = 128


def _plan_kernel(eid_ref, dest_ref, ends_ref):
    eid = eid_ref[...]
    r = lax.broadcasted_iota(i32, (PLAN_N, PLAN_N), 0)
    c = lax.broadcasted_iota(i32, (PLAN_N, PLAN_N), 1)
    upper = (r < c).astype(bf16)
    lower = (c < r).astype(bf16)
    ones = jnp.ones((PLAN_N, PLAN_N), bf16)
    lane = lax.broadcasted_iota(i32, (1, LANES), 1)
    dest = jnp.zeros((PLAN_N, PLAN_N), f32)
    ends = jnp.zeros((1, LANES), f32)
    start = jnp.zeros((1, 1), f32)
    for e in range(N_EXPERTS):
        m = (eid == e).astype(f32)
        mb = m.astype(bf16)
        within = jnp.dot(mb, upper, preferred_element_type=f32)
        rowtot = jnp.dot(mb, ones, preferred_element_type=f32)
        rowpre = jnp.dot(lower, rowtot.astype(bf16), preferred_element_type=f32)
        cnt = jnp.sum(jnp.sum(m, axis=1, keepdims=True), axis=0, keepdims=True)
        dest = dest + m * (start + within + rowpre)
        start = start + jnp.floor((cnt + (ROW_BLK - 1)) * (1.0 / ROW_BLK)) * ROW_BLK
        ends = jnp.where(lane == e, start, ends)
    dest_ref[...] = dest.astype(i32)
    ends_ref[...] = ends.astype(i32)


def _plan(eid):
    whole = lambda shape: pl.BlockSpec(shape, lambda: (0, 0))
    return pl.pallas_call(
        _plan_kernel,
        in_specs=[whole((PLAN_N, PLAN_N))],
        out_specs=[whole((PLAN_N, PLAN_N)), whole((1, LANES))],
        out_shape=[jax.ShapeDtypeStruct((PLAN_N, PLAN_N), i32),
                   jax.ShapeDtypeStruct((1, LANES), i32)],
        name="moe_plan",
    )(eid.reshape(PLAN_N, PLAN_N))


def _dispatch_kernel(ends_ref, dest_ref, hf_ref, xs_hbm, zbuf, sem):
    i = pl.program_id(0)

    def tail_copy(e):
        pend = ends_ref[e]
        prev = jnp.where(e > 0, ends_ref[jnp.maximum(e - 1, 0)], 0)
        row0 = pl.multiple_of(jnp.maximum(pend - ROW_BLK, 0), ROW_BLK)
        return pend > prev, pltpu.make_async_copy(zbuf, xs_hbm.at[_slab_rows(row0, ROW_BLK), :], sem)

    @pl.when(i == 0)
    def _():
        zbuf[...] = jnp.zeros_like(zbuf)

        def zstart(e, _):
            used, cp = tail_copy(e)

            @pl.when(used)
            def _():
                cp.start()
            return 0

        def zwait(e, _):
            used, cp = tail_copy(e)

            @pl.when(used)
            def _():
                cp.wait()
            return 0

        def unused_copy(b):
            row0 = pl.multiple_of(b * ROW_BLK, ROW_BLK)
            return pltpu.make_async_copy(zbuf, xs_hbm.at[_slab_rows(row0, ROW_BLK), :], sem)

        def ustart(b, _):
            unused_copy(b).start()
            return 0

        def uwait(b, _):
            unused_copy(b).wait()
            return 0

        n_used = ends_ref[N_EXPERTS - 1] // ROW_BLK
        lax.fori_loop(0, N_EXPERTS, zstart, 0)
        lax.fori_loop(n_used, N_BLKS, ustart, 0)
        lax.fori_loop(0, N_EXPERTS, zwait, 0)
        lax.fori_loop(n_used, N_BLKS, uwait, 0)

    base = i * TT_DISP

    def row_copy(t, k):
        d = dest_ref[k, base + t]
        return pltpu.make_async_copy(hf_ref.at[_slab_rows(t, 1), :], xs_hbm.at[_slab_rows(d, 1), :], sem)

    def issue(t, _):
        for k in range(TOP_K):
            row_copy(t, k).start(priority=k)
        return 0

    def drain(t, _):
        for k in range(TOP_K):
            row_copy(t, k).wait()
        return 0

    lax.fori_loop(0, TT_DISP, issue, 0, unroll=ISSUE_UNROLL)
    lax.fori_loop(0, TT_DISP, drain, 0, unroll=ISSUE_UNROLL)


def _dispatch(ends, dest, hf):
    return pl.pallas_call(
        _dispatch_kernel,
        grid_spec=pltpu.PrefetchScalarGridSpec(
            num_scalar_prefetch=2,
            grid=(N_TOK // TT_DISP,),
            in_specs=[pl.BlockSpec((TT_DISP * SLAB, LANES), lambda i, e, d: (i, 0))],
            out_specs=pl.BlockSpec(memory_space=pl.ANY),
            scratch_shapes=[pltpu.VMEM((ROW_BLK * SLAB, LANES), f32),
                            pltpu.SemaphoreType.DMA(())]),
        out_shape=jax.ShapeDtypeStruct((N_ROWS * SLAB, LANES), f32),
        compiler_params=_cparams(("arbitrary",)),
        name="moe_dispatch",
    )(ends, dest, hf)


W_RING = 3
X_RING = 6
Y_RING = 3


def _ffn_kernel(ends_ref, wg_hbm, wu_hbm, wd_hbm, xs_hbm, y_hbm,
                wg_f, wu_f, wd_f, wg_s, wu_s, wd_s, xbuf, ybuf, sem_w, sem_x, sem_y):
    n_used = ends_ref[N_EXPERTS - 1] // ROW_BLK

    def w_copies(ex):
        slot = ex % W_RING
        return [pltpu.make_async_copy(hbm.at[ex], buf.at[slot], sem_w.at[i, slot])
                for i, (hbm, buf) in enumerate(((wg_hbm, wg_f), (wu_hbm, wu_f), (wd_hbm, wd_f)))]

    def advance(e_from, e_to):
        def one(ex, _):
            for cp in w_copies(ex):
                cp.wait()

            @pl.when(ex + (W_RING - 1) < N_EXPERTS)
            def _():
                for cp in w_copies(ex + (W_RING - 1)):
                    cp.start()
            return 0

        lax.fori_loop(e_from + 1, e_to + 1, one, 0)

    def x_copy(b):
        slot = b % X_RING
        return pltpu.make_async_copy(xs_hbm.at[_slab_rows(b * ROW_BLK, ROW_BLK), :], xbuf.at[slot],
                                     sem_x.at[slot])

    def y_copy(b):
        slot = b % Y_RING
        return pltpu.make_async_copy(ybuf.at[slot], y_hbm.at[_slab_rows(b * ROW_BLK, ROW_BLK), :],
                                     sem_y.at[slot])

    for ex in range(W_RING - 1):
        for cp in w_copies(ex):
            cp.start()
    for b0 in range(X_RING - 1):
        @pl.when(b0 < n_used)
        def _():
            x_copy(b0).start()

    def body(b, e):
        @pl.when(b + (X_RING - 1) < n_used)
        def _():
            x_copy(b + (X_RING - 1)).start()

        e_new = lax.while_loop(lambda ex: b * ROW_BLK >= ends_ref[ex], lambda ex: ex + 1,
                               jnp.maximum(e, 0))

        @pl.when(e_new != e)
        def _():
            advance(e, e_new)
            wslot = e_new % W_RING
            wg_s[...] = wg_f[wslot].astype(bf16)
            wu_s[...] = wu_f[wslot].astype(bf16)
            wd_s[...] = wd_f[wslot].astype(bf16)

        x_copy(b).wait()

        @pl.when(b >= Y_RING)
        def _():
            y_copy(b - Y_RING).wait()

        x = _slab_load(xbuf, (b % X_RING,), ROW_BLK).astype(bf16)
        h1 = jnp.dot(x, wg_s[...], preferred_element_type=f32)
        h2 = jnp.dot(x, wu_s[...], preferred_element_type=f32)
        a = (h1 * jax.nn.sigmoid(h1) * h2).astype(bf16)
        _slab_store(ybuf, (b % Y_RING,), jnp.dot(a, wd_s[...], preferred_element_type=f32))
        y_copy(b).start()
        return e_new

    e_last = lax.fori_loop(0, n_used, body, jnp.int32(-1))
    advance(e_last, jnp.int32(N_EXPERTS - 1))

    for back in range(Y_RING, 0, -1):
        @pl.when(n_used >= back)
        def _():
            y_copy(n_used - back).wait()

    ybuf[0] = jnp.zeros((ROW_BLK * SLAB, LANES), f32)

    def unused_copy(b):
        return pltpu.make_async_copy(ybuf.at[0], y_hbm.at[_slab_rows(b * ROW_BLK, ROW_BLK), :],
                                     sem_y.at[0])

    def ustart(b, _):
        unused_copy(b).start()
        return 0

    def uwait(b, _):
        unused_copy(b).wait()
        return 0

    lax.fori_loop(n_used, N_BLKS, ustart, 0)
    lax.fori_loop(n_used, N_BLKS, uwait, 0)


def _expert_ffn(ends, xs, wg, wu, wd):
    return pl.pallas_call(
        _ffn_kernel,
        grid_spec=pltpu.PrefetchScalarGridSpec(
            num_scalar_prefetch=1,
            grid=(1,),
            in_specs=[pl.BlockSpec(memory_space=pl.ANY)] * 4,
            out_specs=pl.BlockSpec(memory_space=pl.ANY),
            scratch_shapes=[pltpu.VMEM((W_RING, D_MODEL, EXPERT_FF), f32),
                            pltpu.VMEM((W_RING, D_MODEL, EXPERT_FF), f32),
                            pltpu.VMEM((W_RING, EXPERT_FF, D_MODEL), f32),
                            pltpu.VMEM((D_MODEL, EXPERT_FF), bf16),
                            pltpu.VMEM((D_MODEL, EXPERT_FF), bf16),
                            pltpu.VMEM((EXPERT_FF, D_MODEL), bf16),
                            pltpu.VMEM((X_RING, ROW_BLK * SLAB, LANES), f32),
                            pltpu.VMEM((Y_RING, ROW_BLK * SLAB, LANES), f32),
                            pltpu.SemaphoreType.DMA((3, W_RING)),
                            pltpu.SemaphoreType.DMA((X_RING,)),
                            pltpu.SemaphoreType.DMA((Y_RING,))]),
        out_shape=jax.ShapeDtypeStruct((N_ROWS * SLAB, LANES), f32),
        compiler_params=_cparams(("arbitrary",)),
        name="expert_ffn",
    )(ends, wg, wu, wd, xs)


def _combine_kernel(dest_ref, x_ref, cw_ref, g_ref, y_hbm, o_ref, ybuf, sem):
    i = pl.program_id(0)
    n_steps = pl.num_programs(0)

    def row_copy(step, t, k):
        slot = step % 2
        d = dest_ref[k, step * TM_COMB + t]
        return pltpu.make_async_copy(y_hbm.at[_slab_rows(d, 1), :],
                                     ybuf.at[slot, k, _slab_rows(t, 1), :], sem.at[slot])

    def issue(step):
        def one(t, _):
            for k in range(TOP_K):
                row_copy(step, t, k).start()
            return 0
        lax.fori_loop(0, TM_COMB, one, 0, unroll=ISSUE_UNROLL)

    def drain(step):
        def one(t, _):
            for k in range(TOP_K):
                row_copy(step, t, k).wait()
            return 0
        lax.fori_loop(0, TM_COMB, one, 0, unroll=ISSUE_UNROLL)

    @pl.when(i == 0)
    def _():
        issue(i)

    @pl.when(i + 1 < n_steps)
    def _():
        issue(i + 1)

    drain(i)
    slot = i % 2
    w = cw_ref[...]
    x = (x_ref[...] + w[:, 0:1] * _slab_load(ybuf, (slot, 0), TM_COMB)
         + w[:, 1:2] * _slab_load(ybuf, (slot, 1), TM_COMB))
    o_ref[...] = _rms(x, g_ref[...])


def _combine(dest, x1, cw_t, g, y):
    tm = TM_COMB
    return pl.pallas_call(
        _combine_kernel,
        grid_spec=pltpu.PrefetchScalarGridSpec(
            num_scalar_prefetch=1,
            grid=(N_TOK // tm,),
            in_specs=[pl.BlockSpec((tm, D_MODEL), lambda i, d: (i, 0)),
                      pl.BlockSpec((tm, TOP_K), lambda i, d: (i, 0)),
                      pl.BlockSpec((1, D_MODEL), lambda i, d: (0, 0)),
                      pl.BlockSpec(memory_space=pl.ANY)],
            out_specs=pl.BlockSpec((tm, D_MODEL), lambda i, d: (i, 0)),
            scratch_shapes=[pltpu.VMEM((2, TOP_K, tm * SLAB, LANES), f32),
                            pltpu.SemaphoreType.DMA((2,))]),
        out_shape=jax.ShapeDtypeStruct((N_TOK, D_MODEL), f32),
        compiler_params=_cparams(("arbitrary",)),
        name="moe_combine",
    )(dest, x1, cw_t, g, y)


TR_RELAYOUT = 256
KR_TILE = COL_KR // TR_RELAYOUT


def _w_in_relayout_kernel(w_ref, o_ref):
    i = pl.program_id(0)

    @pl.when(i < COL_V // TR_RELAYOUT)
    def _():
        q = ROPE_NA
        for g in range(TR_RELAYOUT // LANES):
            r = g * LANES
            o_ref[r:r + q, :] = w_ref[r:r + q, :].astype(bf16)
            o_ref[r + q:r + 2 * q, :] = w_ref[r + 2 * q:r + 3 * q, :].astype(bf16)
            o_ref[r + 2 * q:r + 3 * q, :] = w_ref[r + q:r + 2 * q, :].astype(bf16)
            o_ref[r + 3 * q:r + 4 * q, :] = w_ref[r + 3 * q:r + 4 * q, :].astype(bf16)

    @pl.when((i >= COL_V // TR_RELAYOUT) & (i != KR_TILE))
    def _():
        o_ref[...] = w_ref[...].astype(bf16)

    @pl.when(i == KR_TILE)
    def _():
        o_ref[...] = jnp.zeros_like(o_ref)
        o_ref[KR_LANE:KR_LANE + MLA_ROPE, :] = w_ref[0:MLA_ROPE, :].astype(bf16)


def _prep_w_in(w_in_t):
    n_src, d = w_in_t.shape
    gate_src = COL_KR + MLA_ROPE

    gate_tile = COL_GA // TR_RELAYOUT
    sub = 8
    shift = (COL_GA - gate_src) // sub

    def src_row(i):
        return ((TR_RELAYOUT // sub) * i - shift * (i // gate_tile)) * sub

    return pl.pallas_call(
        _w_in_relayout_kernel,
        grid=(PROJ_W // TR_RELAYOUT,),
        in_specs=[pl.BlockSpec((pl.Element(TR_RELAYOUT), pl.Element(d)), lambda i: (src_row(i), 0))],
        out_specs=pl.BlockSpec((TR_RELAYOUT, d), lambda i: (i, 0)),
        out_shape=jax.ShapeDtypeStruct((PROJ_W, d), bf16),
        compiler_params=_cparams(("arbitrary",)),
        name="w_in_relayout",
    )(w_in_t)


def _prep_w_uq(w_uq):
    w = w_uq.reshape(MLA_Q_RANK, HEADS, MLA_NOPE + MLA_ROPE)
    w = jnp.pad(w, ((0, 0), (0, 0), (0, HEAD_W - MLA_NOPE - MLA_ROPE)))
    return w.reshape(MLA_Q_RANK, HEADS * HEAD_W).astype(bf16)


def _prep_w_ukv(w_ukv):
    w = w_ukv.reshape(MLA_KV_RANK, HEADS, MLA_NOPE + HEAD_W)
    wk = jnp.pad(w[:, :, :MLA_NOPE], ((0, 0), (0, 0), (0, HEAD_W - MLA_NOPE)))
    wv = w[:, :, MLA_NOPE:]
    return jnp.concatenate([wk.reshape(MLA_KV_RANK, -1), wv.reshape(MLA_KV_RANK, -1)], axis=1).astype(bf16)


def _prep_router(w_group, b_group, w_router, b_router):
    z = lambda n: jnp.zeros((D_MODEL, n), f32)
    w = jnp.concatenate([w_group, z(RT_EXPERT_ROW - N_GROUPS), w_router,
                         z(LANES - RT_EXPERT_ROW - N_EXPERTS)], axis=1)
    b = jnp.concatenate([b_group, jnp.full((RT_EXPERT_ROW - N_GROUPS,), NEG_INF, f32), b_router,
                         jnp.zeros((LANES - RT_EXPERT_ROW - N_EXPERTS,), f32)])[None, :]
    w_hi = w.astype(bf16)
    w_lo = (w - w_hi.astype(f32)).astype(bf16)
    return jnp.concatenate([w_hi, w_lo], axis=1), b


def kernel(x, positions, attn_norm, w_in, da_lambda, da_subln, mla_q_norm, mla_w_uq, mla_kv_norm, mla_w_ukv, w_branch_a, w_branch_b, w_out, ffn_norm, w_group, b_group, w_router, b_router, w_exp_gate, w_exp_up, w_exp_down, final_norm):
    x2d = x.reshape(N_TOK, D_MODEL)
    cos_t, sin_t = _rope_tables(positions)

    proj = _in_proj(x2d, attn_norm[0][None, :], _prep_w_in(w_in[0].T), cos_t, sin_t)
    qb, kb, vb = _mla_prep(proj, mla_q_norm[0][None, :], mla_kv_norm[0][None, :],
                           _prep_w_uq(mla_w_uq[0]), _prep_w_ukv(mla_w_ukv[0]), cos_t, sin_t)
    oa = _attention(proj, proj, proj, COL_Q // HEAD_W, COL_K // HEAD_W, COL_V // HEAD_W,
                    heads=8, extra=(da_lambda[0], da_subln[0][:, None]))
    ob = _attention(qb, kb, vb, 0, 0, 0, heads=8)
    w_rt, b_rt = _prep_router(w_group[0], b_group[0], w_router[0], b_router[0])
    x1, hf, eid, cw = _merge(oa, ob, proj, x2d, w_branch_a[0].astype(bf16), w_branch_b[0].astype(bf16),
                             w_out[0].astype(bf16), ffn_norm[0][None, :], w_rt, b_rt)

    dest, ends = _plan(eid)
    dest = dest.reshape(TOP_K, N_TOK)
    ends = ends[0, :N_EXPERTS]
    xs = _dispatch(ends, dest, hf)
    y = _expert_ffn(ends, xs, w_exp_gate[0], w_exp_up[0], w_exp_down[0])
    out = _combine(dest, x1, cw.T, final_norm[None, :], y)
    return out.reshape(BATCH, SEQ, D_MODEL)
```

```python
import functools
import math

import jax
import jax.numpy as jnp
from jax import lax
from jax.experimental import pallas as pl
from jax.experimental.pallas import tpu as pltpu

f32 = jnp.float32
bf16 = jnp.bfloat16
i32 = jnp.int32

D_MODEL = 2048
BATCH = 4
SEQ = 2048
N_TOK = BATCH * SEQ
HEADS = 8
HEAD_W = 128
DA_HALF = 64
MLA_Q_RANK = 512
MLA_KV_RANK = 256
MLA_NOPE = 64
MLA_ROPE = 32
N_GROUPS = 4
EXPERTS_PER_GROUP = 8
N_EXPERTS = N_GROUPS * EXPERTS_PER_GROUP
TOP_K = 2
EXPERT_FF = 512
ROPE_THETA = 10000.0
NORM_EPS = 1e-6
NEG_INF = -1e30
LAM_INIT = 0.8 - 0.6 * math.exp(-0.3 * 0)
LOG2E = math.log2(math.e)

LANES = 128
VMEM_LIMIT = 56 * 1024 * 1024

COL_Q, COL_K, COL_V = 0, 1024, 2048
COL_CQ, COL_CKV = 3072, 3584
COL_KR = 3840
COL_GA, COL_GB = 4096, 6144
PROJ_W = 8192
KR_LANE = MLA_NOPE

TM_PROJ, TN_PROJ = 1024, 1024
TM_PREP = 512
TQ = 256
TM_MERGE = 256
ROW_BLK = 128
N_ROWS = (N_TOK * TOP_K // ROW_BLK + N_EXPERTS) * ROW_BLK
N_BLKS = N_ROWS // ROW_BLK
TT_DISP = 1024
TM_COMB = 256

Q_SCALE_A = (DA_HALF ** -0.5) * LOG2E
Q_SCALE_B = ((MLA_NOPE + MLA_ROPE) ** -0.5) * LOG2E


def _cparams(sem, **kw):
    return pltpu.CompilerParams(dimension_semantics=sem, vmem_limit_bytes=VMEM_LIMIT, **kw)


def _rms(x, g):
    return x * lax.rsqrt(jnp.mean(x * x, axis=-1, keepdims=True) + NORM_EPS) * g


SLAB = D_MODEL // LANES


ISSUE_UNROLL = 8


def _slab_rows(row0, nrows):
    return pl.ds(pl.multiple_of(row0 * SLAB, SLAB), nrows * SLAB)


def _slab_store(ref, idx, val):
    rows = val.shape[0]
    for j in range(SLAB):
        ref[idx + (pl.ds(j, rows, stride=SLAB), slice(None))] = val[:, j * LANES:(j + 1) * LANES]


def _slab_chunk(ref, idx, rows, j):
    return ref[idx + (pl.ds(j, rows, stride=SLAB), slice(None))]


def _slab_load(ref, idx, rows):
    return jnp.concatenate([_slab_chunk(ref, idx, rows, j) for j in range(SLAB)], axis=1)


def _rope_lanes(blk, cos, sin, first_half, half):
    up = pltpu.roll(blk, LANES - half, 1)
    down = pltpu.roll(blk, half, 1)
    return blk * cos + jnp.where(first_half, up, down) * sin


ROPE_NA = DA_HALF // 2
ROPE_NB = MLA_ROPE // 2


def _rope_tables_kernel(pos_ref, inv_ref, cos_ref, sin_ref):
    half = pos_ref.shape[0] // 2
    lane = lax.broadcasted_iota(i32, (half, LANES), 1)
    pos = jnp.where(lane < LANES // 2, pos_ref[:half, :], pos_ref[half:, :]).astype(f32)
    ang = pos * inv_ref[...]
    c = jnp.cos(ang)
    s = jnp.sin(ang)
    one = lambda n: jnp.ones((half, n), f32)
    zero = lambda n: jnp.zeros((half, n), f32)
    tail = LANES - KR_LANE - MLA_ROPE
    for part in range(2):
        o = part * (LANES // 2)
        rows = slice(part * half, (part + 1) * half)
        ca, sa = c[:, o:o + ROPE_NA], s[:, o:o + ROPE_NA]
        cb, sb = c[:, o + ROPE_NA:o + ROPE_NA + ROPE_NB], s[:, o + ROPE_NA:o + ROPE_NA + ROPE_NB]
        cos_ref[rows, :] = jnp.concatenate([ca, ca, ca, ca, one(KR_LANE), cb, cb, one(tail)], axis=1)
        sin_ref[rows, :] = jnp.concatenate([-sa, -sa, sa, sa, zero(KR_LANE), -sb, sb, zero(tail)],
                                           axis=1)


def _rope_tables(positions):
    inv_a = ROPE_THETA ** (-jnp.arange(0, DA_HALF, 2, dtype=f32) / DA_HALF)
    inv_b = ROPE_THETA ** (-jnp.arange(0, MLA_ROPE, 2, dtype=f32) / MLA_ROPE)
    inv = jnp.concatenate([inv_a, inv_b, jnp.zeros((LANES // 2 - ROPE_NA - ROPE_NB,), f32)])
    inv = jnp.tile(inv, 2)[None, :]
    pos = positions.reshape(N_TOK, 1)
    tm = 1024
    return pl.pallas_call(
        _rope_tables_kernel,
        grid=(N_TOK // tm,),
        in_specs=[pl.BlockSpec((tm, 1), lambda i: (i, 0)),
                  pl.BlockSpec((1, LANES), lambda i: (0, 0))],
        out_specs=[pl.BlockSpec((tm, 2 * LANES), lambda i: (i, 0)),
                   pl.BlockSpec((tm, 2 * LANES), lambda i: (i, 0))],
        out_shape=[jax.ShapeDtypeStruct((N_TOK, 2 * LANES), f32)] * 2,
        compiler_params=_cparams(("arbitrary",)),
        name="rope_tables",
    )(pos, inv)


def _in_proj_kernel(x_ref, g_ref, w_ref, cos_ref, sin_ref, o_ref, h_scr):
    n = pl.program_id(1)

    @pl.when(n == 0)
    def _():
        h_scr[...] = _rms(x_ref[...], g_ref[...]).astype(bf16)

    acc = lax.dot_general(h_scr[...], w_ref[...], (((1,), (1,)), ((), ())),
                          preferred_element_type=f32)
    n_rope = COL_V // TN_PROJ
    n_q = COL_K // TN_PROJ

    @pl.when(n < n_rope)
    def _():
        c = jnp.where(n < n_q, Q_SCALE_A, 1.0).astype(f32)
        cos = cos_ref[...] * c
        sin = sin_ref[...] * c
        for j in range(TN_PROJ // LANES):
            blk = acc[:, j * LANES:(j + 1) * LANES]
            rot = pltpu.roll(blk, LANES // 2, 1)
            o_ref[:, j * LANES:(j + 1) * LANES] = (blk * cos + rot * sin).astype(bf16)

    @pl.when(n >= n_rope)
    def _():
        o_ref[...] = acc.astype(bf16)


def _in_proj(x2d, g, w_pad, cos_t, sin_t):
    return pl.pallas_call(
        _in_proj_kernel,
        grid=(N_TOK // TM_PROJ, PROJ_W // TN_PROJ),
        in_specs=[pl.BlockSpec((TM_PROJ, D_MODEL), lambda i, n: (i, 0)),
                  pl.BlockSpec((1, D_MODEL), lambda i, n: (0, 0)),
                  pl.BlockSpec((TN_PROJ, D_MODEL), lambda i, n: (n, 0)),
                  pl.BlockSpec((TM_PROJ, LANES), lambda i, n: (i, 0)),
                  pl.BlockSpec((TM_PROJ, LANES), lambda i, n: (i, 0))],
        out_specs=pl.BlockSpec((TM_PROJ, TN_PROJ), lambda i, n: (i, n)),
        out_shape=jax.ShapeDtypeStruct((N_TOK, PROJ_W), bf16),
        scratch_shapes=[pltpu.VMEM((TM_PROJ, D_MODEL), bf16)],
        compiler_params=_cparams(("arbitrary", "arbitrary")),
        name="in_proj",
    )(x2d, g, w_pad, cos_t, sin_t)


def _mla_prep_kernel(cq_ref, ckv_ref, kr_ref, gq_ref, gkv_ref, wuq_ref, wkv_ref, cos_ref, sin_ref,
                     q_ref, k_ref, v_ref):
    cqn = _rms(cq_ref[...].astype(f32), gq_ref[...]).astype(bf16)
    q = jnp.dot(cqn, wuq_ref[...], preferred_element_type=f32)
    ckvn = _rms(ckv_ref[...].astype(f32), gkv_ref[...]).astype(bf16)
    kv = jnp.dot(ckvn, wkv_ref[...], preferred_element_type=f32)
    cos = cos_ref[...]
    sin = sin_ref[...]
    lane = lax.broadcasted_iota(i32, (TM_PREP, LANES), 1)
    first = lane < (KR_LANE + MLA_ROPE // 2)
    rope = lambda b: _rope_lanes(b, cos, sin, first, MLA_ROPE // 2)
    k_rot = rope(kr_ref[...].astype(f32))
    for h in range(HEADS):
        sl = slice(h * HEAD_W, (h + 1) * HEAD_W)
        q_ref[:, sl] = (rope(q[:, sl]) * Q_SCALE_B).astype(bf16)
        k_ref[:, sl] = (kv[:, sl] + k_rot).astype(bf16)
    v_ref[...] = kv[:, HEADS * HEAD_W:].astype(bf16)


def _mla_prep(proj, gq, gkv, wuq_p, wkv_p, cos_t, sin_t):
    tm = TM_PREP
    w = HEADS * HEAD_W
    return pl.pallas_call(
        _mla_prep_kernel,
        grid=(N_TOK // tm,),
        in_specs=[pl.BlockSpec((tm, MLA_Q_RANK), lambda i: (i, COL_CQ // MLA_Q_RANK)),
                  pl.BlockSpec((tm, MLA_KV_RANK), lambda i: (i, COL_CKV // MLA_KV_RANK)),
                  pl.BlockSpec((tm, LANES), lambda i: (i, COL_KR // LANES)),
                  pl.BlockSpec((1, MLA_Q_RANK), lambda i: (0, 0)),
                  pl.BlockSpec((1, MLA_KV_RANK), lambda i: (0, 0)),
                  pl.BlockSpec((MLA_Q_RANK, w), lambda i: (0, 0)),
                  pl.BlockSpec((MLA_KV_RANK, 2 * w), lambda i: (0, 0)),
                  pl.BlockSpec((tm, LANES), lambda i: (i, 1)),
                  pl.BlockSpec((tm, LANES), lambda i: (i, 1))],
        out_specs=[pl.BlockSpec((tm, w), lambda i: (i, 0))] * 3,
        out_shape=[jax.ShapeDtypeStruct((N_TOK, w), bf16)] * 3,
        compiler_params=_cparams(("arbitrary",)),
        name="mla_prep",
    )(proj, proj, proj, gq, gkv, wuq_p, wkv_p, cos_t, sin_t)


def _attn_kernel(*refs, ncomp, heads):
    if ncomp == 2:
        q_ref, k_ref, v_ref, lam_ref, g_ref, o_ref, acc_scr, vt_scr = refs
    else:
        q_ref, k_ref, v_ref, o_ref, acc_scr, vt_scr = refs
    qi = pl.program_id(2)

    @pl.when(qi == 0)
    def _():
        for j in range(SEQ // TQ):
            for h in range(heads):
                vt_scr[j, h] = v_ref[j * TQ:(j + 1) * TQ, h * HEAD_W:(h + 1) * HEAD_W].T

    lane = lax.broadcasted_iota(i32, (TQ, HEAD_W), 1)
    qs = []
    for h in range(heads):
        q = q_ref[:, h * HEAD_W:(h + 1) * HEAD_W]
        if ncomp == 2:
            zero = jnp.zeros_like(q)
            in_a = (lane % DA_HALF) < (DA_HALF // 2)
            qs += [jnp.where(in_a, q, zero), jnp.where(in_a, zero, q)]
        else:
            qs.append(q)

    def step(j, carry, masked):
        off = pl.multiple_of(j * TQ, TQ)
        hsl = lambda n: slice((n // ncomp) * HEAD_W, (n // ncomp + 1) * HEAD_W)
        ss = []
        for n, qc in enumerate(qs):
            k = k_ref[pl.ds(off, TQ), hsl(n)]
            s = lax.dot_general(k, qc, (((1,), (1,)), ((), ())), preferred_element_type=f32)
            if masked:
                key = lax.broadcasted_iota(i32, (TQ, TQ), 0)
                qry = lax.broadcasted_iota(i32, (TQ, TQ), 1)
                s = jnp.where(key <= qry, s, NEG_INF)
            ss.append(s)
        stats = []
        for n, s in enumerate(ss):
            m, l = carry[n]
            m_new = jnp.maximum(m, jnp.max(s, axis=0, keepdims=True))
            alpha = jnp.exp2(m - m_new)
            p = jnp.exp2(s - m_new)
            l_new = alpha * l + jnp.sum(p, axis=0, keepdims=True)
            stats.append((m_new, l_new, alpha, p.astype(bf16)))
        out = []
        for n, (m_new, l_new, alpha, p) in enumerate(stats):
            pv = jnp.dot(vt_scr[j, n // ncomp], p, preferred_element_type=f32)
            acc_scr[n] = alpha * acc_scr[n] + pv
            out.append((m_new, l_new))
        return tuple(out)

    acc_scr[...] = jnp.zeros_like(acc_scr)
    init = tuple((jnp.full((1, TQ), NEG_INF, f32), jnp.zeros((1, TQ), f32)) for _ in qs)
    carry = lax.fori_loop(0, qi, lambda j, c: step(j, c, False), init)
    carry = step(qi, carry, True)
    carry = tuple((m, l, acc_scr[n]) for n, (m, l) in enumerate(carry))

    if ncomp == 2:
        ll = lam_ref[...]
        s1 = jnp.sum(ll[0:1, :] * ll[1:2, :], axis=-1, keepdims=True)
        s2 = jnp.sum(ll[2:3, :] * ll[3:4, :], axis=-1, keepdims=True)
        lam = jnp.exp(s1) - jnp.exp(s2) + LAM_INIT
    for h in range(heads):
        if ncomp == 2:
            (_, l0, a0), (_, l1, a1) = carry[2 * h], carry[2 * h + 1]
            o = a0 / l0 - lam * (a1 / l1)
            ms = jnp.mean(o * o, axis=0, keepdims=True)
            o = o * lax.rsqrt(ms + NORM_EPS) * g_ref[...] * (1.0 - LAM_INIT)
        else:
            _, l0, a0 = carry[h]
            o = a0 / l0
        o_ref[:, h * HEAD_W:(h + 1) * HEAD_W] = o.T.astype(bf16)


def _attention(q_arr, k_arr, v_arr, q_blk0, k_blk0, v_blk0, heads, extra=()):
    ncomp = 2 if extra else 1
    nq = SEQ // TQ
    w = heads * HEAD_W
    qb, kb, vb = q_blk0 // heads, k_blk0 // heads, v_blk0 // heads
    in_specs = [pl.BlockSpec((TQ, w), lambda b, h, i: (b * nq + i, qb + h)),
                pl.BlockSpec((SEQ, w), lambda b, h, i: (b, kb + h)),
                pl.BlockSpec((SEQ, w), lambda b, h, i: (b, vb + h))]
    for a in extra:
        in_specs.append(pl.BlockSpec(a.shape, lambda b, h, i: (0, 0)))
    return pl.pallas_call(
        functools.partial(_attn_kernel, ncomp=ncomp, heads=heads),
        grid=(BATCH, HEADS // heads, nq),
        in_specs=in_specs,
        out_specs=pl.BlockSpec((TQ, w), lambda b, h, i: (b * nq + i, h)),
        out_shape=jax.ShapeDtypeStruct((N_TOK, HEADS * HEAD_W), bf16),
        scratch_shapes=[pltpu.VMEM((heads * ncomp, HEAD_W, TQ), f32),
                        pltpu.VMEM((SEQ // TQ, heads, HEAD_W, TQ), bf16)],
        compiler_params=_cparams(("arbitrary", "arbitrary", "arbitrary")),
        name="diff_attn" if extra else "mla_attn",
    )(q_arr, k_arr, v_arr, *extra)


RT_GROUP_ROW = 0
RT_EXPERT_ROW = 8


def _route(hf, w_ref, b_ref, eid_ref, cw_ref):
    hf_hi = hf.astype(bf16)
    hf_lo = (hf - hf_hi.astype(f32)).astype(bf16)
    a = jnp.dot(hf_hi, w_ref[...], preferred_element_type=f32)
    b = jnp.dot(hf_lo, w_ref[:, :LANES], preferred_element_type=f32)
    logits = a[:, :LANES] + (a[:, LANES:] + b) + b_ref[...]
    lt = logits.T
    tm = lt.shape[1]
    ridx = lax.broadcasted_iota(i32, (EXPERTS_PER_GROUP, tm), 0).astype(f32)

    def top1(v):
        vmax = jnp.max(v, axis=0, keepdims=True)
        idx = jnp.min(jnp.where(v == vmax, ridx, float(EXPERTS_PER_GROUP)), axis=0, keepdims=True)
        return vmax, idx

    gl = lt[RT_GROUP_ROW:RT_GROUP_ROW + 8, :]
    gmax, gsel = top1(gl)
    p_g = 1.0 / jnp.sum(jnp.exp(gl - gmax), axis=0, keepdims=True)
    el = jnp.zeros((EXPERTS_PER_GROUP, tm), f32)
    for g in range(N_GROUPS):
        r0 = RT_EXPERT_ROW + g * EXPERTS_PER_GROUP
        el = jnp.where(gsel == float(g), lt[r0:r0 + EXPERTS_PER_GROUP, :], el)
    e1, i1 = top1(el)
    e2, i2 = top1(jnp.where(ridx == i1, -jnp.inf, el))
    r = jnp.exp(e2 - e1)
    w1 = 1.0 / (1.0 + r)
    base = gsel * EXPERTS_PER_GROUP
    eid_ref[0:1, :] = (base + i1).astype(i32)
    eid_ref[1:2, :] = (base + i2).astype(i32)
    cw_ref[0:1, :] = p_g * w1
    cw_ref[1:2, :] = p_g * (r * w1)


def _merge_kernel(oa_ref, ob_ref, ga_ref, gb_ref, x_ref, wa_ref, wb_ref, wo_ref, gf_ref, wr_ref, br_ref,
                  x1_ref, hf_ref, eid_ref, cw_ref):
    ya = jnp.dot(oa_ref[...], wa_ref[...], preferred_element_type=f32)
    yb = jnp.dot(ob_ref[...], wb_ref[...], preferred_element_type=f32)
    mixed = (jax.nn.sigmoid(ga_ref[...].astype(f32)) * ya
             + jax.nn.sigmoid(gb_ref[...].astype(f32)) * yb)
    x1 = x_ref[...] + jnp.dot(mixed.astype(bf16), wo_ref[...], preferred_element_type=f32)
    x1_ref[...] = x1
    hf = _rms(x1, gf_ref[...])
    _slab_store(hf_ref, (), hf)
    _route(hf, wr_ref, br_ref, eid_ref, cw_ref)


def _merge(oa, ob, proj, x2d, wa, wb, wo, gf, w_rt, b_rt):
    tm = TM_MERGE
    w = HEADS * HEAD_W
    const = lambda shape: pl.BlockSpec(shape, lambda i: (0, 0), pipeline_mode=pl.Buffered(1))
    return pl.pallas_call(
        _merge_kernel,
        grid=(N_TOK // tm,),
        in_specs=[pl.BlockSpec((tm, w), lambda i: (i, 0)),
                  pl.BlockSpec((tm, w), lambda i: (i, 0)),
                  pl.BlockSpec((tm, D_MODEL), lambda i: (i, COL_GA // D_MODEL)),
                  pl.BlockSpec((tm, D_MODEL), lambda i: (i, COL_GB // D_MODEL)),
                  pl.BlockSpec((tm, D_MODEL), lambda i: (i, 0)),
                  const((w, D_MODEL)), const((w, D_MODEL)), const((D_MODEL, D_MODEL)),
                  const((1, D_MODEL)), const((D_MODEL, 2 * LANES)), const((1, LANES))],
        out_specs=[pl.BlockSpec((tm, D_MODEL), lambda i: (i, 0)),
                   pl.BlockSpec((tm * SLAB, LANES), lambda i: (i, 0)),
                   pl.BlockSpec((TOP_K, tm), lambda i: (0, i)),
                   pl.BlockSpec((TOP_K, tm), lambda i: (0, i))],
        out_shape=[jax.ShapeDtypeStruct((N_TOK, D_MODEL), f32),
                   jax.ShapeDtypeStruct((N_TOK * SLAB, LANES), f32),
                   jax.ShapeDtypeStruct((TOP_K, N_TOK), i32),
                   jax.ShapeDtypeStruct((TOP_K, N_TOK), f32)],
        compiler_params=_cparams(("arbitrary",)),
        name="merge_route",
    )(oa, ob, proj, proj, x2d, wa, wb, wo, gf, w_rt, b_rt)


PLAN_N = 128


def _plan_kernel(eid_ref, dest_ref, ends_ref):
    eid = eid_ref[...]
    r = lax.broadcasted_iota(i32, (PLAN_N, PLAN_N), 0)
    c = lax.broadcasted_iota(i32, (PLAN_N, PLAN_N), 1)
    upper = (r < c).astype(bf16)
    lower = (c < r).astype(bf16)
    ones = jnp.ones((PLAN_N, PLAN_N), bf16)
    lane = lax.broadcasted_iota(i32, (1, LANES), 1)
    dest = jnp.zeros((PLAN_N, PLAN_N), f32)
    ends = jnp.zeros((1, LANES), f32)
    start = jnp.zeros((1, 1), f32)
    for e in range(N_EXPERTS):
        m = (eid == e).astype(f32)
        mb = m.astype(bf16)
        within = jnp.dot(mb, upper, preferred_element_type=f32)
        rowtot = jnp.dot(mb, ones, preferred_element_type=f32)
        rowpre = jnp.dot(lower, rowtot.astype(bf16), preferred_element_type=f32)
        cnt = jnp.sum(jnp.sum(m, axis=1, keepdims=True), axis=0, keepdims=True)
        dest = dest + m * (start + within + rowpre)
        start = start + jnp.floor((cnt + (ROW_BLK - 1)) * (1.0 / ROW_BLK)) * ROW_BLK
        ends = jnp.where(lane == e, start, ends)
    dest_ref[...] = dest.astype(i32)
    ends_ref[...] = ends.astype(i32)


def _plan(eid):
    whole = lambda shape: pl.BlockSpec(shape, lambda: (0, 0))
    return pl.pallas_call(
        _plan_kernel,
        in_specs=[whole((PLAN_N, PLAN_N))],
        out_specs=[whole((PLAN_N, PLAN_N)), whole((1, LANES))],
        out_shape=[jax.ShapeDtypeStruct((PLAN_N, PLAN_N), i32),
                   jax.ShapeDtypeStruct((1, LANES), i32)],
        name="moe_plan",
    )(eid.reshape(PLAN_N, PLAN_N))


def _dispatch_kernel(ends_ref, dest_ref, hf_ref, xs_hbm, zbuf, sem):
    i = pl.program_id(0)

    def tail_copy(e):
        pend = ends_ref[e]
        prev = jnp.where(e > 0, ends_ref[jnp.maximum(e - 1, 0)], 0)
        row0 = pl.multiple_of(jnp.maximum(pend - ROW_BLK, 0), ROW_BLK)
        return pend > prev, pltpu.make_async_copy(zbuf, xs_hbm.at[_slab_rows(row0, ROW_BLK), :], sem)

    @pl.when(i == 0)
    def _():
        zbuf[...] = jnp.zeros_like(zbuf)

        def zstart(e, _):
            used, cp = tail_copy(e)

            @pl.when(used)
            def _():
                cp.start()
            return 0

        def zwait(e, _):
            used, cp = tail_copy(e)

            @pl.when(used)
            def _():
                cp.wait()
            return 0

        def unused_copy(b):
            row0 = pl.multiple_of(b * ROW_BLK, ROW_BLK)
            return pltpu.make_async_copy(zbuf, xs_hbm.at[_slab_rows(row0, ROW_BLK), :], sem)

        def ustart(b, _):
            unused_copy(b).start()
            return 0

        def uwait(b, _):
            unused_copy(b).wait()
            return 0

        n_used = ends_ref[N_EXPERTS - 1] // ROW_BLK
        lax.fori_loop(0, N_EXPERTS, zstart, 0)
        lax.fori_loop(n_used, N_BLKS, ustart, 0)
        lax.fori_loop(0, N_EXPERTS, zwait, 0)
        lax.fori_loop(n_used, N_BLKS, uwait, 0)

    base = i * TT_DISP

    def row_copy(t, k):
        d = dest_ref[k, base + t]
        return pltpu.make_async_copy(hf_ref.at[_slab_rows(t, 1), :], xs_hbm.at[_slab_rows(d, 1), :], sem)

    def issue(t, _):
        for k in range(TOP_K):
            row_copy(t, k).start(priority=k)
        return 0

    def drain(t, _):
        for k in range(TOP_K):
            row_copy(t, k).wait()
        return 0

    lax.fori_loop(0, TT_DISP, issue, 0, unroll=ISSUE_UNROLL)
    lax.fori_loop(0, TT_DISP, drain, 0, unroll=ISSUE_UNROLL)


def _dispatch(ends, dest, hf):
    return pl.pallas_call(
        _dispatch_kernel,
        grid_spec=pltpu.PrefetchScalarGridSpec(
            num_scalar_prefetch=2,
            grid=(N_TOK // TT_DISP,),
            in_specs=[pl.BlockSpec((TT_DISP * SLAB, LANES), lambda i, e, d: (i, 0))],
            out_specs=pl.BlockSpec(memory_space=pl.ANY),
            scratch_shapes=[pltpu.VMEM((ROW_BLK * SLAB, LANES), f32),
                            pltpu.SemaphoreType.DMA(())]),
        out_shape=jax.ShapeDtypeStruct((N_ROWS * SLAB, LANES), f32),
        compiler_params=_cparams(("arbitrary",)),
        name="moe_dispatch",
    )(ends, dest, hf)


W_RING = 2
X_RING = 12
Y_RING = 4


def _ffn_kernel(ends_ref, wg_hbm, wu_hbm, wd_hbm, xs_hbm, y_hbm,
                wg_f, wu_f, wd_f, wg_s, wu_s, wd_s, xbuf, ybuf, sem_w, sem_x, sem_y):
    n_used = ends_ref[N_EXPERTS - 1] // ROW_BLK

    def w_copies(ex):
        slot = ex % W_RING
        return [pltpu.make_async_copy(hbm.at[ex], buf.at[slot], sem_w.at[i, slot])
                for i, (hbm, buf) in enumerate(((wg_hbm, wg_f), (wu_hbm, wu_f), (wd_hbm, wd_f)))]

    def advance(e_from, e_to):
        def one(ex, _):
            for cp in w_copies(ex):
                cp.wait()

            @pl.when(ex + (W_RING - 1) < N_EXPERTS)
            def _():
                for cp in w_copies(ex + (W_RING - 1)):
                    cp.start()
            return 0

        lax.fori_loop(e_from + 1, e_to + 1, one, 0)

    def x_copy(b):
        slot = b % X_RING
        return pltpu.make_async_copy(xs_hbm.at[_slab_rows(b * ROW_BLK, ROW_BLK), :], xbuf.at[slot],
                                     sem_x.at[slot])

    def y_copy(b):
        slot = b % Y_RING
        return pltpu.make_async_copy(ybuf.at[slot], y_hbm.at[_slab_rows(b * ROW_BLK, ROW_BLK), :],
                                     sem_y.at[slot])

    for ex in range(W_RING - 1):
        for cp in w_copies(ex):
            cp.start()
    for b0 in range(X_RING - 1):
        @pl.when(b0 < n_used)
        def _():
            x_copy(b0).start()

    def body(b, e):
        @pl.when(b + (X_RING - 1) < n_used)
        def _():
            x_copy(b + (X_RING - 1)).start()

        e_new = lax.while_loop(lambda ex: b * ROW_BLK >= ends_ref[ex], lambda ex: ex + 1,
                               jnp.maximum(e, 0))

        @pl.when(e_new != e)
        def _():
            advance(e, e_new)
            wslot = e_new % W_RING
            wg_s[...] = wg_f[wslot].astype(bf16)
            wu_s[...] = wu_f[wslot].astype(bf16)
            wd_s[...] = wd_f[wslot].astype(bf16)

        x_copy(b).wait()

        @pl.when(b >= Y_RING)
        def _():
            y_copy(b - Y_RING).wait()

        x = _slab_load(xbuf, (b % X_RING,), ROW_BLK).astype(bf16)
        h1 = jnp.dot(x, wg_s[...], preferred_element_type=f32)
        h2 = jnp.dot(x, wu_s[...], preferred_element_type=f32)
        a = (h1 * jax.nn.sigmoid(h1) * h2).astype(bf16)
        _slab_store(ybuf, (b % Y_RING,), jnp.dot(a, wd_s[...], preferred_element_type=f32))
        y_copy(b).start()
        return e_new

    e_last = lax.fori_loop(0, n_used, body, jnp.int32(-1))
    advance(e_last, jnp.int32(N_EXPERTS - 1))

    for back in range(Y_RING, 0, -1):
        @pl.when(n_used >= back)
        def _():
            y_copy(n_used - back).wait()

    ybuf[0] = jnp.zeros((ROW_BLK * SLAB, LANES), f32)

    def unused_copy(b):
        return pltpu.make_async_copy(ybuf.at[0], y_hbm.at[_slab_rows(b * ROW_BLK, ROW_BLK), :],
                                     sem_y.at[0])

    def ustart(b, _):
        unused_copy(b).start()
        return 0

    def uwait(b, _):
        unused_copy(b).wait()
        return 0

    lax.fori_loop(n_used, N_BLKS, ustart, 0)
    lax.fori_loop(n_used, N_BLKS, uwait, 0)


def _expert_ffn(ends, xs, wg, wu, wd):
    return pl.pallas_call(
        _ffn_kernel,
        grid_spec=pltpu.PrefetchScalarGridSpec(
            num_scalar_prefetch=1,
            grid=(1,),
            in_specs=[pl.BlockSpec(memory_space=pl.ANY)] * 4,
            out_specs=pl.BlockSpec(memory_space=pl.ANY),
            scratch_shapes=[pltpu.VMEM((W_RING, D_MODEL, EXPERT_FF), f32),
                            pltpu.VMEM((W_RING, D_MODEL, EXPERT_FF), f32),
                            pltpu.VMEM((W_RING, EXPERT_FF, D_MODEL), f32),
                            pltpu.VMEM((D_MODEL, EXPERT_FF), bf16),
                            pltpu.VMEM((D_MODEL, EXPERT_FF), bf16),
                            pltpu.VMEM((EXPERT_FF, D_MODEL), bf16),
                            pltpu.VMEM((X_RING, ROW_BLK * SLAB, LANES), f32),
                            pltpu.VMEM((Y_RING, ROW_BLK * SLAB, LANES), f32),
                            pltpu.SemaphoreType.DMA((3, W_RING)),
                            pltpu.SemaphoreType.DMA((X_RING,)),
                            pltpu.SemaphoreType.DMA((Y_RING,))]),
        out_shape=jax.ShapeDtypeStruct((N_ROWS * SLAB, LANES), f32),
        compiler_params=_cparams(("arbitrary",)),
        name="expert_ffn",
    )(ends, wg, wu, wd, xs)


def _combine_kernel(dest_ref, x_ref, cw_ref, g_ref, y_hbm, o_ref, ybuf, sem):
    i = pl.program_id(0)
    n_steps = pl.num_programs(0)

    def row_copy(step, t, k):
        slot = step % 2
        d = dest_ref[k, step * TM_COMB + t]
        return pltpu.make_async_copy(y_hbm.at[_slab_rows(d, 1), :],
                                     ybuf.at[slot, k, _slab_rows(t, 1), :], sem.at[slot])

    def issue(step):
        def one(t, _):
            for k in range(TOP_K):
                row_copy(step, t, k).start()
            return 0
        lax.fori_loop(0, TM_COMB, one, 0, unroll=ISSUE_UNROLL)

    def drain(step):
        def one(t, _):
            for k in range(TOP_K):
                row_copy(step, t, k).wait()
            return 0
        lax.fori_loop(0, TM_COMB, one, 0, unroll=ISSUE_UNROLL)

    @pl.when(i == 0)
    def _():
        issue(i)

    @pl.when(i + 1 < n_steps)
    def _():
        issue(i + 1)

    drain(i)
    slot = i % 2
    w = cw_ref[...]
    x = (x_ref[...] + w[:, 0:1] * _slab_load(ybuf, (slot, 0), TM_COMB)
         + w[:, 1:2] * _slab_load(ybuf, (slot, 1), TM_COMB))
    o_ref[...] = _rms(x, g_ref[...])


def _combine(dest, x1, cw_t, g, y):
    tm = TM_COMB
    return pl.pallas_call(
        _combine_kernel,
        grid_spec=pltpu.PrefetchScalarGridSpec(
            num_scalar_prefetch=1,
            grid=(N_TOK // tm,),
            in_specs=[pl.BlockSpec((tm, D_MODEL), lambda i, d: (i, 0)),
                      pl.BlockSpec((tm, TOP_K), lambda i, d: (i, 0)),
                      pl.BlockSpec((1, D_MODEL), lambda i, d: (0, 0)),
                      pl.BlockSpec(memory_space=pl.ANY)],
            out_specs=pl.BlockSpec((tm, D_MODEL), lambda i, d: (i, 0)),
            scratch_shapes=[pltpu.VMEM((2, TOP_K, tm * SLAB, LANES), f32),
                            pltpu.SemaphoreType.DMA((2,))]),
        out_shape=jax.ShapeDtypeStruct((N_TOK, D_MODEL), f32),
        compiler_params=_cparams(("arbitrary",)),
        name="moe_combine",
    )(dest, x1, cw_t, g, y)


TR_RELAYOUT = 256
KR_TILE = COL_KR // TR_RELAYOUT


def _w_in_relayout_kernel(w_ref, o_ref):
    i = pl.program_id(0)

    @pl.when(i < COL_V // TR_RELAYOUT)
    def _():
        q = ROPE_NA
        for g in range(TR_RELAYOUT // LANES):
            r = g * LANES
            o_ref[r:r + q, :] = w_ref[r:r + q, :].astype(bf16)
            o_ref[r + q:r + 2 * q, :] = w_ref[r + 2 * q:r + 3 * q, :].astype(bf16)
            o_ref[r + 2 * q:r + 3 * q, :] = w_ref[r + q:r + 2 * q, :].astype(bf16)
            o_ref[r + 3 * q:r + 4 * q, :] = w_ref[r + 3 * q:r + 4 * q, :].astype(bf16)

    @pl.when((i >= COL_V // TR_RELAYOUT) & (i != KR_TILE))
    def _():
        o_ref[...] = w_ref[...].astype(bf16)

    @pl.when(i == KR_TILE)
    def _():
        o_ref[...] = jnp.zeros_like(o_ref)
        o_ref[KR_LANE:KR_LANE + MLA_ROPE, :] = w_ref[0:MLA_ROPE, :].astype(bf16)


def _prep_w_in(w_in_t):
    n_src, d = w_in_t.shape
    gate_src = COL_KR + MLA_ROPE

    gate_tile = COL_GA // TR_RELAYOUT
    sub = 8
    shift = (COL_GA - gate_src) // sub

    def src_row(i):
        return ((TR_RELAYOUT // sub) * i - shift * (i // gate_tile)) * sub

    return pl.pallas_call(
        _w_in_relayout_kernel,
        grid=(PROJ_W // TR_RELAYOUT,),
        in_specs=[pl.BlockSpec((pl.Element(TR_RELAYOUT), pl.Element(d)), lambda i: (src_row(i), 0))],
        out_specs=pl.BlockSpec((TR_RELAYOUT, d), lambda i: (i, 0)),
        out_shape=jax.ShapeDtypeStruct((PROJ_W, d), bf16),
        compiler_params=_cparams(("arbitrary",)),
        name="w_in_relayout",
    )(w_in_t)


def _prep_w_uq(w_uq):
    w = w_uq.reshape(MLA_Q_RANK, HEADS, MLA_NOPE + MLA_ROPE)
    w = jnp.pad(w, ((0, 0), (0, 0), (0, HEAD_W - MLA_NOPE - MLA_ROPE)))
    return w.reshape(MLA_Q_RANK, HEADS * HEAD_W).astype(bf16)


def _prep_w_ukv(w_ukv):
    w = w_ukv.reshape(MLA_KV_RANK, HEADS, MLA_NOPE + HEAD_W)
    wk = jnp.pad(w[:, :, :MLA_NOPE], ((0, 0), (0, 0), (0, HEAD_W - MLA_NOPE)))
    wv = w[:, :, MLA_NOPE:]
    return jnp.concatenate([wk.reshape(MLA_KV_RANK, -1), wv.reshape(MLA_KV_RANK, -1)], axis=1).astype(bf16)


def _prep_router(w_group, b_group, w_router, b_router):
    z = lambda n: jnp.zeros((D_MODEL, n), f32)
    w = jnp.concatenate([w_group, z(RT_EXPERT_ROW - N_GROUPS), w_router,
                         z(LANES - RT_EXPERT_ROW - N_EXPERTS)], axis=1)
    b = jnp.concatenate([b_group, jnp.full((RT_EXPERT_ROW - N_GROUPS,), NEG_INF, f32), b_router,
                         jnp.zeros((LANES - RT_EXPERT_ROW - N_EXPERTS,), f32)])[None, :]
    w_hi = w.astype(bf16)
    w_lo = (w - w_hi.astype(f32)).astype(bf16)
    return jnp.concatenate([w_hi, w_lo], axis=1), b


def kernel(x, positions, attn_norm, w_in, da_lambda, da_subln, mla_q_norm, mla_w_uq, mla_kv_norm, mla_w_ukv, w_branch_a, w_branch_b, w_out, ffn_norm, w_group, b_group, w_router, b_router, w_exp_gate, w_exp_up, w_exp_down, final_norm):
    x2d = x.reshape(N_TOK, D_MODEL)
    cos_t, sin_t = _rope_tables(positions)

    proj = _in_proj(x2d, attn_norm[0][None, :], _prep_w_in(w_in[0].T), cos_t, sin_t)
    qb, kb, vb = _mla_prep(proj, mla_q_norm[0][None, :], mla_kv_norm[0][None, :],
                           _prep_w_uq(mla_w_uq[0]), _prep_w_ukv(mla_w_ukv[0]), cos_t, sin_t)
    oa = _attention(proj, proj, proj, COL_Q // HEAD_W, COL_K // HEAD_W, COL_V // HEAD_W,
                    heads=8, extra=(da_lambda[0], da_subln[0][:, None]))
    ob = _attention(qb, kb, vb, 0, 0, 0, heads=8)
    w_rt, b_rt = _prep_router(w_group[0], b_group[0], w_router[0], b_router[0])
    x1, hf, eid, cw = _merge(oa, ob, proj, x2d, w_branch_a[0].astype(bf16), w_branch_b[0].astype(bf16),
                             w_out[0].astype(bf16), ffn_norm[0][None, :], w_rt, b_rt)

    dest, ends = _plan(eid)
    dest = dest.reshape(TOP_K, N_TOK)
    ends = ends[0, :N_EXPERTS]
    xs = _dispatch(ends, dest, hf)
    y = _expert_ffn(ends, xs, w_exp_gate[0], w_exp_up[0], w_exp_down[0])
    out = _combine(dest, x1, cw.T, final_norm[None, :], y)
    return out.reshape(BATCH, SEQ, D_MODEL)
```

```python
import functools
import math

import jax
import jax.numpy as jnp
from jax import lax
from jax.experimental import pallas as pl
from jax.experimental.pallas import tpu as pltpu

f32 = jnp.float32
bf16 = jnp.bfloat16
i32 = jnp.int32

D_MODEL = 2048
BATCH = 4
SEQ = 2048
N_TOK = BATCH * SEQ
HEADS = 8
HEAD_W = 128
DA_HALF = 64
MLA_Q_RANK = 512
MLA_KV_RANK = 256
MLA_NOPE = 64
MLA_ROPE = 32
N_GROUPS = 4
EXPERTS_PER_GROUP = 8
N_EXPERTS = N_GROUPS * EXPERTS_PER_GROUP
TOP_K = 2
EXPERT_FF = 512
ROPE_THETA = 10000.0
NORM_EPS = 1e-6
NEG_INF = -1e30
LAM_INIT = 0.8 - 0.6 * math.exp(-0.3 * 0)
LOG2E = math.log2(math.e)

LANES = 128
VMEM_LIMIT = 56 * 1024 * 1024

COL_Q, COL_K, COL_V = 0, 1024, 2048
COL_CQ, COL_CKV = 3072, 3584
COL_KR = 3840
COL_GA, COL_GB = 4096, 6144
PROJ_W = 8192
KR_LANE = MLA_NOPE

TM_PROJ, TN_PROJ = 1024, 1024
TM_PREP = 512
TQ = 256
TM_MERGE = 256
ROW_BLK = 128
N_ROWS = (N_TOK * TOP_K // ROW_BLK + N_EXPERTS) * ROW_BLK
N_BLKS = N_ROWS // ROW_BLK
TT_DISP = 1024
TM_COMB = 256

Q_SCALE_A = (DA_HALF ** -0.5) * LOG2E
Q_SCALE_B = ((MLA_NOPE + MLA_ROPE) ** -0.5) * LOG2E


def _cparams(sem, **kw):
    return pltpu.CompilerParams(dimension_semantics=sem, vmem_limit_bytes=VMEM_LIMIT, **kw)


def _rms(x, g):
    return x * lax.rsqrt(jnp.mean(x * x, axis=-1, keepdims=True) + NORM_EPS) * g


SLAB = D_MODEL // LANES


ISSUE_UNROLL = 8


def _slab_rows(row0, nrows):
    return pl.ds(pl.multiple_of(row0 * SLAB, SLAB), nrows * SLAB)


def _slab_store(ref, idx, val):
    rows = val.shape[0]
    for j in range(SLAB):
        ref[idx + (pl.ds(j, rows, stride=SLAB), slice(None))] = val[:, j * LANES:(j + 1) * LANES]


def _slab_chunk(ref, idx, rows, j):
    return ref[idx + (pl.ds(j, rows, stride=SLAB), slice(None))]


def _slab_load(ref, idx, rows):
    return jnp.concatenate([_slab_chunk(ref, idx, rows, j) for j in range(SLAB)], axis=1)


def _rope_lanes(blk, cos, sin, first_half, half):
    up = pltpu.roll(blk, LANES - half, 1)
    down = pltpu.roll(blk, half, 1)
    return blk * cos + jnp.where(first_half, up, down) * sin


ROPE_NA = DA_HALF // 2
ROPE_NB = MLA_ROPE // 2


def _rope_tables_kernel(pos_ref, inv_ref, cos_ref, sin_ref):
    half = pos_ref.shape[0] // 2
    lane = lax.broadcasted_iota(i32, (half, LANES), 1)
    pos = jnp.where(lane < LANES // 2, pos_ref[:half, :], pos_ref[half:, :]).astype(f32)
    ang = pos * inv_ref[...]
    c = jnp.cos(ang)
    s = jnp.sin(ang)
    one = lambda n: jnp.ones((half, n), f32)
    zero = lambda n: jnp.zeros((half, n), f32)
    tail = LANES - KR_LANE - MLA_ROPE
    for part in range(2):
        o = part * (LANES // 2)
        rows = slice(part * half, (part + 1) * half)
        ca, sa = c[:, o:o + ROPE_NA], s[:, o:o + ROPE_NA]
        cb, sb = c[:, o + ROPE_NA:o + ROPE_NA + ROPE_NB], s[:, o + ROPE_NA:o + ROPE_NA + ROPE_NB]
        cos_ref[rows, :] = jnp.concatenate([ca, ca, ca, ca, one(KR_LANE), cb, cb, one(tail)], axis=1)
        sin_ref[rows, :] = jnp.concatenate([-sa, -sa, sa, sa, zero(KR_LANE), -sb, sb, zero(tail)],
                                           axis=1)


def _rope_tables(positions):
    inv_a = ROPE_THETA ** (-jnp.arange(0, DA_HALF, 2, dtype=f32) / DA_HALF)
    inv_b = ROPE_THETA ** (-jnp.arange(0, MLA_ROPE, 2, dtype=f32) / MLA_ROPE)
    inv = jnp.concatenate([inv_a, inv_b, jnp.zeros((LANES // 2 - ROPE_NA - ROPE_NB,), f32)])
    inv = jnp.tile(inv, 2)[None, :]
    pos = positions.reshape(N_TOK, 1)
    tm = 1024
    return pl.pallas_call(
        _rope_tables_kernel,
        grid=(N_TOK // tm,),
        in_specs=[pl.BlockSpec((tm, 1), lambda i: (i, 0)),
                  pl.BlockSpec((1, LANES), lambda i: (0, 0))],
        out_specs=[pl.BlockSpec((tm, 2 * LANES), lambda i: (i, 0)),
                   pl.BlockSpec((tm, 2 * LANES), lambda i: (i, 0))],
        out_shape=[jax.ShapeDtypeStruct((N_TOK, 2 * LANES), f32)] * 2,
        compiler_params=_cparams(("arbitrary",)),
        name="rope_tables",
    )(pos, inv)


def _in_proj_kernel(x_ref, g_ref, w_ref, cos_ref, sin_ref, o_ref, h_scr):
    n = pl.program_id(1)

    @pl.when(n == 0)
    def _():
        h_scr[...] = _rms(x_ref[...], g_ref[...]).astype(bf16)

    acc = lax.dot_general(h_scr[...], w_ref[...], (((1,), (1,)), ((), ())),
                          preferred_element_type=f32)
    n_rope = COL_V // TN_PROJ
    n_q = COL_K // TN_PROJ

    @pl.when(n < n_rope)
    def _():
        c = jnp.where(n < n_q, Q_SCALE_A, 1.0).astype(f32)
        cos = cos_ref[...] * c
        sin = sin_ref[...] * c
        for j in range(TN_PROJ // LANES):
            blk = acc[:, j * LANES:(j + 1) * LANES]
            rot = pltpu.roll(blk, LANES // 2, 1)
            o_ref[:, j * LANES:(j + 1) * LANES] = (blk * cos + rot * sin).astype(bf16)

    @pl.when(n >= n_rope)
    def _():
        o_ref[...] = acc.astype(bf16)


def _in_proj(x2d, g, w_pad, cos_t, sin_t):
    return pl.pallas_call(
        _in_proj_kernel,
        grid=(N_TOK // TM_PROJ, PROJ_W // TN_PROJ),
        in_specs=[pl.BlockSpec((TM_PROJ, D_MODEL), lambda i, n: (i, 0)),
                  pl.BlockSpec((1, D_MODEL), lambda i, n: (0, 0)),
                  pl.BlockSpec((TN_PROJ, D_MODEL), lambda i, n: (n, 0)),
                  pl.BlockSpec((TM_PROJ, LANES), lambda i, n: (i, 0)),
                  pl.BlockSpec((TM_PROJ, LANES), lambda i, n: (i, 0))],
        out_specs=pl.BlockSpec((TM_PROJ, TN_PROJ), lambda i, n: (i, n)),
        out_shape=jax.ShapeDtypeStruct((N_TOK, PROJ_W), bf16),
        scratch_shapes=[pltpu.VMEM((TM_PROJ, D_MODEL), bf16)],
        compiler_params=_cparams(("arbitrary", "arbitrary")),
        name="in_proj",
    )(x2d, g, w_pad, cos_t, sin_t)


def _mla_prep_kernel(cq_ref, ckv_ref, kr_ref, gq_ref, gkv_ref, wuq_ref, wkv_ref, cos_ref, sin_ref,
                     q_ref, k_ref, v_ref):
    cqn = _rms(cq_ref[...].astype(f32), gq_ref[...]).astype(bf16)
    q = jnp.dot(cqn, wuq_ref[...], preferred_element_type=f32)
    ckvn = _rms(ckv_ref[...].astype(f32), gkv_ref[...]).astype(bf16)
    kv = jnp.dot(ckvn, wkv_ref[...], preferred_element_type=f32)
    cos = cos_ref[...]
    sin = sin_ref[...]
    lane = lax.broadcasted_iota(i32, (TM_PREP, LANES), 1)
    first = lane < (KR_LANE + MLA_ROPE // 2)
    rope = lambda b: _rope_lanes(b, cos, sin, first, MLA_ROPE // 2)
    k_rot = rope(kr_ref[...].astype(f32))
    for h in range(HEADS):
        sl = slice(h * HEAD_W, (h + 1) * HEAD_W)
        q_ref[:, sl] = (rope(q[:, sl]) * Q_SCALE_B).astype(bf16)
        k_ref[:, sl] = (kv[:, sl] + k_rot).astype(bf16)
    v_ref[...] = kv[:, HEADS * HEAD_W:].astype(bf16)


def _mla_prep(proj, gq, gkv, wuq_p, wkv_p, cos_t, sin_t):
    tm = TM_PREP
    w = HEADS * HEAD_W
    return pl.pallas_call(
        _mla_prep_kernel,
        grid=(N_TOK // tm,),
        in_specs=[pl.BlockSpec((tm, MLA_Q_RANK), lambda i: (i, COL_CQ // MLA_Q_RANK)),
                  pl.BlockSpec((tm, MLA_KV_RANK), lambda i: (i, COL_CKV // MLA_KV_RANK)),
                  pl.BlockSpec((tm, LANES), lambda i: (i, COL_KR // LANES)),
                  pl.BlockSpec((1, MLA_Q_RANK), lambda i: (0, 0)),
                  pl.BlockSpec((1, MLA_KV_RANK), lambda i: (0, 0)),
                  pl.BlockSpec((MLA_Q_RANK, w), lambda i: (0, 0)),
                  pl.BlockSpec((MLA_KV_RANK, 2 * w), lambda i: (0, 0)),
                  pl.BlockSpec((tm, LANES), lambda i: (i, 1)),
                  pl.BlockSpec((tm, LANES), lambda i: (i, 1))],
        out_specs=[pl.BlockSpec((tm, w), lambda i: (i, 0))] * 3,
        out_shape=[jax.ShapeDtypeStruct((N_TOK, w), bf16)] * 3,
        compiler_params=_cparams(("arbitrary",)),
        name="mla_prep",
    )(proj, proj, proj, gq, gkv, wuq_p, wkv_p, cos_t, sin_t)


def _attn_kernel(*refs, ncomp, heads, qtiles):
    if ncomp == 2:
        q_ref, k_ref, v_ref, lam_ref, g_ref, o_ref, acc_scr, vt_scr = refs
    else:
        q_ref, k_ref, v_ref, o_ref, acc_scr, vt_scr = refs
    qi = pl.program_id(2)

    @pl.when(qi == 0)
    def _():
        for j in range(SEQ // TQ):
            for h in range(heads):
                vt_scr[j, h] = v_ref[j * TQ:(j + 1) * TQ, h * HEAD_W:(h + 1) * HEAD_W].T

    lane = lax.broadcasted_iota(i32, (TQ, HEAD_W), 1)
    qs = []
    for t in range(qtiles):
        for h in range(heads):
            q = q_ref[t * TQ:(t + 1) * TQ, h * HEAD_W:(h + 1) * HEAD_W]
            if ncomp == 2:
                zero = jnp.zeros_like(q)
                in_a = (lane % DA_HALF) < (DA_HALF // 2)
                qs += [jnp.where(in_a, q, zero), jnp.where(in_a, zero, q)]
            else:
                qs.append(q)
    per_tile = heads * ncomp
    head_of = lambda n: (n % per_tile) // ncomp

    def step(j, carry, modes):
        off = pl.multiple_of(j * TQ, TQ)
        live = [n for n in range(len(qs)) if modes[n // per_tile] != "skip"]
        ss = {}
        for n in live:
            h = head_of(n)
            k = k_ref[pl.ds(off, TQ), h * HEAD_W:(h + 1) * HEAD_W]
            s = lax.dot_general(k, qs[n], (((1,), (1,)), ((), ())), preferred_element_type=f32)
            if modes[n // per_tile] == "diag":
                key = lax.broadcasted_iota(i32, (TQ, TQ), 0)
                qry = lax.broadcasted_iota(i32, (TQ, TQ), 1)
                s = jnp.where(key <= qry, s, NEG_INF)
            ss[n] = s
        stats = {}
        for n in live:
            m, l = carry[n]
            s = ss[n]
            m_new = jnp.maximum(m, jnp.max(s, axis=0, keepdims=True))
            alpha = jnp.exp2(m - m_new)
            p = jnp.exp2(s - m_new)
            l_new = alpha * l + jnp.sum(p, axis=0, keepdims=True)
            stats[n] = (m_new, l_new, alpha, p.astype(bf16))
        out = list(carry)
        for n in live:
            m_new, l_new, alpha, p = stats[n]
            pv = jnp.dot(vt_scr[j, head_of(n)], p, preferred_element_type=f32)
            acc_scr[n] = alpha * acc_scr[n] + pv
            out[n] = (m_new, l_new)
        return tuple(out)

    acc_scr[...] = jnp.zeros_like(acc_scr)
    init = tuple((jnp.full((1, TQ), NEG_INF, f32), jnp.zeros((1, TQ), f32)) for _ in qs)
    first = qi * qtiles
    carry = lax.fori_loop(0, first, lambda j, c: step(j, c, ("full",) * qtiles), init)
    for u in range(qtiles):
        modes = tuple("skip" if t < u else ("diag" if t == u else "full") for t in range(qtiles))
        carry = step(first + u, carry, modes)
    carry = tuple((m, l, acc_scr[n]) for n, (m, l) in enumerate(carry))

    if ncomp == 2:
        ll = lam_ref[...]
        s1 = jnp.sum(ll[0:1, :] * ll[1:2, :], axis=-1, keepdims=True)
        s2 = jnp.sum(ll[2:3, :] * ll[3:4, :], axis=-1, keepdims=True)
        lam = jnp.exp(s1) - jnp.exp(s2) + LAM_INIT
    for t in range(qtiles):
        for h in range(heads):
            n0 = (t * heads + h) * ncomp
            if ncomp == 2:
                (_, l0, a0), (_, l1, a1) = carry[n0], carry[n0 + 1]
                o = a0 / l0 - lam * (a1 / l1)
                ms = jnp.mean(o * o, axis=0, keepdims=True)
                o = o * lax.rsqrt(ms + NORM_EPS) * g_ref[...] * (1.0 - LAM_INIT)
            else:
                _, l0, a0 = carry[n0]
                o = a0 / l0
            o_ref[t * TQ:(t + 1) * TQ, h * HEAD_W:(h + 1) * HEAD_W] = o.T.astype(bf16)


def _attention(q_arr, k_arr, v_arr, q_blk0, k_blk0, v_blk0, heads, qtiles, extra=()):
    ncomp = 2 if extra else 1
    tq = qtiles * TQ
    nq = SEQ // tq
    w = heads * HEAD_W
    qb, kb, vb = q_blk0 // heads, k_blk0 // heads, v_blk0 // heads
    in_specs = [pl.BlockSpec((tq, w), lambda b, h, i: (b * nq + i, qb + h)),
                pl.BlockSpec((SEQ, w), lambda b, h, i: (b, kb + h)),
                pl.BlockSpec((SEQ, w), lambda b, h, i: (b, vb + h))]
    for a in extra:
        in_specs.append(pl.BlockSpec(a.shape, lambda b, h, i: (0, 0)))
    return pl.pallas_call(
        functools.partial(_attn_kernel, ncomp=ncomp, heads=heads, qtiles=qtiles),
        grid=(BATCH, HEADS // heads, nq),
        in_specs=in_specs,
        out_specs=pl.BlockSpec((tq, w), lambda b, h, i: (b * nq + i, h)),
        out_shape=jax.ShapeDtypeStruct((N_TOK, HEADS * HEAD_W), bf16),
        scratch_shapes=[pltpu.VMEM((qtiles * heads * ncomp, HEAD_W, TQ), f32),
                        pltpu.VMEM((SEQ // TQ, heads, HEAD_W, TQ), bf16)],
        compiler_params=_cparams(("arbitrary", "arbitrary", "arbitrary")),
        name="diff_attn" if extra else "mla_attn",
    )(q_arr, k_arr, v_arr, *extra)


RT_GROUP_ROW = 0
RT_EXPERT_ROW = 8


def _route(hf, w_ref, b_ref, eid_ref, cw_ref):
    hf_hi = hf.astype(bf16)
    hf_lo = (hf - hf_hi.astype(f32)).astype(bf16)
    a = jnp.dot(hf_hi, w_ref[...], preferred_element_type=f32)
    b = jnp.dot(hf_lo, w_ref[:, :LANES], preferred_element_type=f32)
    logits = a[:, :LANES] + (a[:, LANES:] + b) + b_ref[...]
    lt = logits.T
    tm = lt.shape[1]
    ridx = lax.broadcasted_iota(i32, (EXPERTS_PER_GROUP, tm), 0).astype(f32)

    def top1(v):
        vmax = jnp.max(v, axis=0, keepdims=True)
        idx = jnp.min(jnp.where(v == vmax, ridx, float(EXPERTS_PER_GROUP)), axis=0, keepdims=True)
        return vmax, idx

    gl = lt[RT_GROUP_ROW:RT_GROUP_ROW + 8, :]
    gmax, gsel = top1(gl)
    p_g = 1.0 / jnp.sum(jnp.exp(gl - gmax), axis=0, keepdims=True)
    el = jnp.zeros((EXPERTS_PER_GROUP, tm), f32)
    for g in range(N_GROUPS):
        r0 = RT_EXPERT_ROW + g * EXPERTS_PER_GROUP
        el = jnp.where(gsel == float(g), lt[r0:r0 + EXPERTS_PER_GROUP, :], el)
    e1, i1 = top1(el)
    e2, i2 = top1(jnp.where(ridx == i1, -jnp.inf, el))
    r = jnp.exp(e2 - e1)
    w1 = 1.0 / (1.0 + r)
    base = gsel * EXPERTS_PER_GROUP
    eid_ref[0:1, :] = (base + i1).astype(i32)
    eid_ref[1:2, :] = (base + i2).astype(i32)
    cw_ref[0:1, :] = p_g * w1
    cw_ref[1:2, :] = p_g * (r * w1)


def _merge_kernel(oa_ref, ob_ref, ga_ref, gb_ref, x_ref, wa_ref, wb_ref, wo_ref, gf_ref, wr_ref, br_ref,
                  x1_ref, hf_ref, eid_ref, cw_ref):
    ya = jnp.dot(oa_ref[...], wa_ref[...], preferred_element_type=f32)
    yb = jnp.dot(ob_ref[...], wb_ref[...], preferred_element_type=f32)
    mixed = (jax.nn.sigmoid(ga_ref[...].astype(f32)) * ya
             + jax.nn.sigmoid(gb_ref[...].astype(f32)) * yb)
    x1 = x_ref[...] + jnp.dot(mixed.astype(bf16), wo_ref[...], preferred_element_type=f32)
    x1_ref[...] = x1
    hf = _rms(x1, gf_ref[...])
    _slab_store(hf_ref, (), hf)
    _route(hf, wr_ref, br_ref, eid_ref, cw_ref)


def _merge(oa, ob, proj, x2d, wa, wb, wo, gf, w_rt, b_rt):
    tm = TM_MERGE
    w = HEADS * HEAD_W
    const = lambda shape: pl.BlockSpec(shape, lambda i: (0, 0), pipeline_mode=pl.Buffered(1))
    return pl.pallas_call(
        _merge_kernel,
        grid=(N_TOK // tm,),
        in_specs=[pl.BlockSpec((tm, w), lambda i: (i, 0)),
                  pl.BlockSpec((tm, w), lambda i: (i, 0)),
                  pl.BlockSpec((tm, D_MODEL), lambda i: (i, COL_GA // D_MODEL)),
                  pl.BlockSpec((tm, D_MODEL), lambda i: (i, COL_GB // D_MODEL)),
                  pl.BlockSpec((tm, D_MODEL), lambda i: (i, 0)),
                  const((w, D_MODEL)), const((w, D_MODEL)), const((D_MODEL, D_MODEL)),
                  const((1, D_MODEL)), const((D_MODEL, 2 * LANES)), const((1, LANES))],
        out_specs=[pl.BlockSpec((tm, D_MODEL), lambda i: (i, 0)),
                   pl.BlockSpec((tm * SLAB, LANES), lambda i: (i, 0)),
                   pl.BlockSpec((TOP_K, tm), lambda i: (0, i)),
                   pl.BlockSpec((TOP_K, tm), lambda i: (0, i))],
        out_shape=[jax.ShapeDtypeStruct((N_TOK, D_MODEL), f32),
                   jax.ShapeDtypeStruct((N_TOK * SLAB, LANES), f32),
                   jax.ShapeDtypeStruct((TOP_K, N_TOK), i32),
                   jax.ShapeDtypeStruct((TOP_K, N_TOK), f32)],
        compiler_params=_cparams(("arbitrary",)),
        name="merge_route",
    )(oa, ob, proj, proj, x2d, wa, wb, wo, gf, w_rt, b_rt)


PLAN_N = 128


def _plan_kernel(eid_ref, dest_ref, ends_ref):
    eid = eid_ref[...]
    r = lax.broadcasted_iota(i32, (PLAN_N, PLAN_N), 0)
    c = lax.broadcasted_iota(i32, (PLAN_N, PLAN_N), 1)
    upper = (r < c).astype(bf16)
    lower = (c < r).astype(bf16)
    ones = jnp.ones((PLAN_N, PLAN_N), bf16)
    lane = lax.broadcasted_iota(i32, (1, LANES), 1)
    dest = jnp.zeros((PLAN_N, PLAN_N), f32)
    ends = jnp.zeros((1, LANES), f32)
    start = jnp.zeros((1, 1), f32)
    for e in range(N_EXPERTS):
        m = (eid == e).astype(f32)
        mb = m.astype(bf16)
        within = jnp.dot(mb, upper, preferred_element_type=f32)
        rowtot = jnp.dot(mb, ones, preferred_element_type=f32)
        rowpre = jnp.dot(lower, rowtot.astype(bf16), preferred_element_type=f32)
        cnt = jnp.sum(jnp.sum(m, axis=1, keepdims=True), axis=0, keepdims=True)
        dest = dest + m * (start + within + rowpre)
        start = start + jnp.floor((cnt + (ROW_BLK - 1)) * (1.0 / ROW_BLK)) * ROW_BLK
        ends = jnp.where(lane == e, start, ends)
    dest_ref[...] = dest.astype(i32)
    ends_ref[...] = ends.astype(i32)


def _plan(eid):
    whole = lambda shape: pl.BlockSpec(shape, lambda: (0, 0))
    return pl.pallas_call(
        _plan_kernel,
        in_specs=[whole((PLAN_N, PLAN_N))],
        out_specs=[whole((PLAN_N, PLAN_N)), whole((1, LANES))],
        out_shape=[jax.ShapeDtypeStruct((PLAN_N, PLAN_N), i32),
                   jax.ShapeDtypeStruct((1, LANES), i32)],
        name="moe_plan",
    )(eid.reshape(PLAN_N, PLAN_N))


def _dispatch_kernel(ends_ref, dest_ref, hf_ref, xs_hbm, zbuf, sem):
    i = pl.program_id(0)

    def tail_copy(e):
        pend = ends_ref[e]
        prev = jnp.where(e > 0, ends_ref[jnp.maximum(e - 1, 0)], 0)
        row0 = pl.multiple_of(jnp.maximum(pend - ROW_BLK, 0), ROW_BLK)
        return pend > prev, pltpu.make_async_copy(zbuf, xs_hbm.at[_slab_rows(row0, ROW_BLK), :], sem)

    @pl.when(i == 0)
    def _():
        zbuf[...] = jnp.zeros_like(zbuf)

        def zstart(e, _):
            used, cp = tail_copy(e)

            @pl.when(used)
            def _():
                cp.start()
            return 0

        def zwait(e, _):
            used, cp = tail_copy(e)

            @pl.when(used)
            def _():
                cp.wait()
            return 0

        def unused_copy(b):
            row0 = pl.multiple_of(b * ROW_BLK, ROW_BLK)
            return pltpu.make_async_copy(zbuf, xs_hbm.at[_slab_rows(row0, ROW_BLK), :], sem)

        def ustart(b, _):
            unused_copy(b).start()
            return 0

        def uwait(b, _):
            unused_copy(b).wait()
            return 0

        n_used = ends_ref[N_EXPERTS - 1] // ROW_BLK
        lax.fori_loop(0, N_EXPERTS, zstart, 0)
        lax.fori_loop(n_used, N_BLKS, ustart, 0)
        lax.fori_loop(0, N_EXPERTS, zwait, 0)
        lax.fori_loop(n_used, N_BLKS, uwait, 0)

    base = i * TT_DISP

    def row_copy(t, k):
        d = dest_ref[k, base + t]
        return pltpu.make_async_copy(hf_ref.at[_slab_rows(t, 1), :], xs_hbm.at[_slab_rows(d, 1), :], sem)

    def issue(t, _):
        for k in range(TOP_K):
            row_copy(t, k).start(priority=k)
        return 0

    def drain(t, _):
        for k in range(TOP_K):
            row_copy(t, k).wait()
        return 0

    lax.fori_loop(0, TT_DISP, issue, 0, unroll=ISSUE_UNROLL)
    lax.fori_loop(0, TT_DISP, drain, 0, unroll=ISSUE_UNROLL)


def _dispatch(ends, dest, hf):
    return pl.pallas_call(
        _dispatch_kernel,
        grid_spec=pltpu.PrefetchScalarGridSpec(
            num_scalar_prefetch=2,
            grid=(N_TOK // TT_DISP,),
            in_specs=[pl.BlockSpec((TT_DISP * SLAB, LANES), lambda i, e, d: (i, 0))],
            out_specs=pl.BlockSpec(memory_space=pl.ANY),
            scratch_shapes=[pltpu.VMEM((ROW_BLK * SLAB, LANES), f32),
                            pltpu.SemaphoreType.DMA(())]),
        out_shape=jax.ShapeDtypeStruct((N_ROWS * SLAB, LANES), f32),
        compiler_params=_cparams(("arbitrary",)),
        name="moe_dispatch",
    )(ends, dest, hf)


W_RING = 3
X_RING = 6
Y_RING = 3


def _ffn_kernel(ends_ref, wg_hbm, wu_hbm, wd_hbm, xs_hbm, y_hbm,
                wg_f, wu_f, wd_f, wg_s, wu_s, wd_s, xbuf, ybuf, sem_w, sem_x, sem_y):
    n_used = ends_ref[N_EXPERTS - 1] // ROW_BLK

    def w_copies(ex):
        slot = ex % W_RING
        return [pltpu.make_async_copy(hbm.at[ex], buf.at[slot], sem_w.at[i, slot])
                for i, (hbm, buf) in enumerate(((wg_hbm, wg_f), (wu_hbm, wu_f), (wd_hbm, wd_f)))]

    def advance(e_from, e_to):
        def one(ex, _):
            for cp in w_copies(ex):
                cp.wait()

            @pl.when(ex + (W_RING - 1) < N_EXPERTS)
            def _():
                for cp in w_copies(ex + (W_RING - 1)):
                    cp.start()
            return 0

        lax.fori_loop(e_from + 1, e_to + 1, one, 0)

    def x_copy(b):
        slot = b % X_RING
        return pltpu.make_async_copy(xs_hbm.at[_slab_rows(b * ROW_BLK, ROW_BLK), :], xbuf.at[slot],
                                     sem_x.at[slot])

    def y_copy(b):
        slot = b % Y_RING
        return pltpu.make_async_copy(ybuf.at[slot], y_hbm.at[_slab_rows(b * ROW_BLK, ROW_BLK), :],
                                     sem_y.at[slot])

    for ex in range(W_RING - 1):
        for cp in w_copies(ex):
            cp.start()
    for b0 in range(X_RING - 1):
        @pl.when(b0 < n_used)
        def _():
            x_copy(b0).start()

    def body(b, e):
        @pl.when(b + (X_RING - 1) < n_used)
        def _():
            x_copy(b + (X_RING - 1)).start()

        e_new = lax.while_loop(lambda ex: b * ROW_BLK >= ends_ref[ex], lambda ex: ex + 1,
                               jnp.maximum(e, 0))

        @pl.when(e_new != e)
        def _():
            advance(e, e_new)
            wslot = e_new % W_RING
            wg_s[...] = wg_f[wslot].astype(bf16)
            wu_s[...] = wu_f[wslot].astype(bf16)
            wd_s[...] = wd_f[wslot].astype(bf16)

        x_copy(b).wait()

        @pl.when(b >= Y_RING)
        def _():
            y_copy(b - Y_RING).wait()

        x = _slab_load(xbuf, (b % X_RING,), ROW_BLK).astype(bf16)
        h1 = jnp.dot(x, wg_s[...], preferred_element_type=f32)
        h2 = jnp.dot(x, wu_s[...], preferred_element_type=f32)
        a = (h1 * jax.nn.sigmoid(h1) * h2).astype(bf16)
        _slab_store(ybuf, (b % Y_RING,), jnp.dot(a, wd_s[...], preferred_element_type=f32))
        y_copy(b).start()
        return e_new

    e_last = lax.fori_loop(0, n_used, body, jnp.int32(-1))
    advance(e_last, jnp.int32(N_EXPERTS - 1))

    for back in range(Y_RING, 0, -1):
        @pl.when(n_used >= back)
        def _():
            y_copy(n_used - back).wait()

    ybuf[0] = jnp.zeros((ROW_BLK * SLAB, LANES), f32)

    def unused_copy(b):
        return pltpu.make_async_copy(ybuf.at[0], y_hbm.at[_slab_rows(b * ROW_BLK, ROW_BLK), :],
                                     sem_y.at[0])

    def ustart(b, _):
        unused_copy(b).start()
        return 0

    def uwait(b, _):
        unused_copy(b).wait()
        return 0

    lax.fori_loop(n_used, N_BLKS, ustart, 0)
    lax.fori_loop(n_used, N_BLKS, uwait, 0)


def _expert_ffn(ends, xs, wg, wu, wd):
    return pl.pallas_call(
        _ffn_kernel,
        grid_spec=pltpu.PrefetchScalarGridSpec(
            num_scalar_prefetch=1,
            grid=(1,),
            in_specs=[pl.BlockSpec(memory_space=pl.ANY)] * 4,
            out_specs=pl.BlockSpec(memory_space=pl.ANY),
            scratch_shapes=[pltpu.VMEM((W_RING, D_MODEL, EXPERT_FF), f32),
                            pltpu.VMEM((W_RING, D_MODEL, EXPERT_FF), f32),
                            pltpu.VMEM((W_RING, EXPERT_FF, D_MODEL), f32),
                            pltpu.VMEM((D_MODEL, EXPERT_FF), bf16),
                            pltpu.VMEM((D_MODEL, EXPERT_FF), bf16),
                            pltpu.VMEM((EXPERT_FF, D_MODEL), bf16),
                            pltpu.VMEM((X_RING, ROW_BLK * SLAB, LANES), f32),
                            pltpu.VMEM((Y_RING, ROW_BLK * SLAB, LANES), f32),
                            pltpu.SemaphoreType.DMA((3, W_RING)),
                            pltpu.SemaphoreType.DMA((X_RING,)),
                            pltpu.SemaphoreType.DMA((Y_RING,))]),
        out_shape=jax.ShapeDtypeStruct((N_ROWS * SLAB, LANES), f32),
        compiler_params=_cparams(("arbitrary",)),
        name="expert_ffn",
    )(ends, wg, wu, wd, xs)


def _combine_kernel(dest_ref, x_ref, cw_ref, g_ref, y_hbm, o_ref, ybuf, sem):
    i = pl.program_id(0)
    n_steps = pl.num_programs(0)

    def row_copy(step, t, k):
        slot = step % 2
        d = dest_ref[k, step * TM_COMB + t]
        return pltpu.make_async_copy(y_hbm.at[_slab_rows(d, 1), :],
                                     ybuf.at[slot, k, _slab_rows(t, 1), :], sem.at[slot])

    def issue(step):
        def one(t, _):
            for k in range(TOP_K):
                row_copy(step, t, k).start()
            return 0
        lax.fori_loop(0, TM_COMB, one, 0, unroll=ISSUE_UNROLL)

    def drain(step):
        def one(t, _):
            for k in range(TOP_K):
                row_copy(step, t, k).wait()
            return 0
        lax.fori_loop(0, TM_COMB, one, 0, unroll=ISSUE_UNROLL)

    @pl.when(i == 0)
    def _():
        issue(i)

    @pl.when(i + 1 < n_steps)
    def _():
        issue(i + 1)

    drain(i)
    slot = i % 2
    w = cw_ref[...]
    x = (x_ref[...] + w[:, 0:1] * _slab_load(ybuf, (slot, 0), TM_COMB)
         + w[:, 1:2] * _slab_load(ybuf, (slot, 1), TM_COMB))
    o_ref[...] = _rms(x, g_ref[...])


def _combine(dest, x1, cw_t, g, y):
    tm = TM_COMB
    return pl.pallas_call(
        _combine_kernel,
        grid_spec=pltpu.PrefetchScalarGridSpec(
            num_scalar_prefetch=1,
            grid=(N_TOK // tm,),
            in_specs=[pl.BlockSpec((tm, D_MODEL), lambda i, d: (i, 0)),
                      pl.BlockSpec((tm, TOP_K), lambda i, d: (i, 0)),
                      pl.BlockSpec((1, D_MODEL), lambda i, d: (0, 0)),
                      pl.BlockSpec(memory_space=pl.ANY)],
            out_specs=pl.BlockSpec((tm, D_MODEL), lambda i, d: (i, 0)),
            scratch_shapes=[pltpu.VMEM((2, TOP_K, tm * SLAB, LANES), f32),
                            pltpu.SemaphoreType.DMA((2,))]),
        out_shape=jax.ShapeDtypeStruct((N_TOK, D_MODEL), f32),
        compiler_params=_cparams(("arbitrary",)),
        name="moe_combine",
    )(dest, x1, cw_t, g, y)


TR_RELAYOUT = 256
KR_TILE = COL_KR // TR_RELAYOUT


def _w_in_relayout_kernel(w_ref, o_ref):
    i = pl.program_id(0)

    @pl.when(i < COL_V // TR_RELAYOUT)
    def _():
        q = ROPE_NA
        for g in range(TR_RELAYOUT // LANES):
            r = g * LANES
            o_ref[r:r + q, :] = w_ref[r:r + q, :].astype(bf16)
            o_ref[r + q:r + 2 * q, :] = w_ref[r + 2 * q:r + 3 * q, :].astype(bf16)
            o_ref[r + 2 * q:r + 3 * q, :] = w_ref[r + q:r + 2 * q, :].astype(bf16)
            o_ref[r + 3 * q:r + 4 * q, :] = w_ref[r + 3 * q:r + 4 * q, :].astype(bf16)

    @pl.when((i >= COL_V // TR_RELAYOUT) & (i != KR_TILE))
    def _():
        o_ref[...] = w_ref[...].astype(bf16)

    @pl.when(i == KR_TILE)
    def _():
        o_ref[...] = jnp.zeros_like(o_ref)
        o_ref[KR_LANE:KR_LANE + MLA_ROPE, :] = w_ref[0:MLA_ROPE, :].astype(bf16)


def _prep_w_in(w_in_t):
    n_src, d = w_in_t.shape
    gate_src = COL_KR + MLA_ROPE

    gate_tile = COL_GA // TR_RELAYOUT
    sub = 8
    shift = (COL_GA - gate_src) // sub

    def src_row(i):
        return ((TR_RELAYOUT // sub) * i - shift * (i // gate_tile)) * sub

    return pl.pallas_call(
        _w_in_relayout_kernel,
        grid=(PROJ_W // TR_RELAYOUT,),
        in_specs=[pl.BlockSpec((pl.Element(TR_RELAYOUT), pl.Element(d)), lambda i: (src_row(i), 0))],
        out_specs=pl.BlockSpec((TR_RELAYOUT, d), lambda i: (i, 0)),
        out_shape=jax.ShapeDtypeStruct((PROJ_W, d), bf16),
        compiler_params=_cparams(("arbitrary",)),
        name="w_in_relayout",
    )(w_in_t)


def _prep_w_uq(w_uq):
    w = w_uq.reshape(MLA_Q_RANK, HEADS, MLA_NOPE + MLA_ROPE)
    w = jnp.pad(w, ((0, 0), (0, 0), (0, HEAD_W - MLA_NOPE - MLA_ROPE)))
    return w.reshape(MLA_Q_RANK, HEADS * HEAD_W).astype(bf16)


def _prep_w_ukv(w_ukv):
    w = w_ukv.reshape(MLA_KV_RANK, HEADS, MLA_NOPE + HEAD_W)
    wk = jnp.pad(w[:, :, :MLA_NOPE], ((0, 0), (0, 0), (0, HEAD_W - MLA_NOPE)))
    wv = w[:, :, MLA_NOPE:]
    return jnp.concatenate([wk.reshape(MLA_KV_RANK, -1), wv.reshape(MLA_KV_RANK, -1)], axis=1).astype(bf16)


def _prep_router(w_group, b_group, w_router, b_router):
    z = lambda n: jnp.zeros((D_MODEL, n), f32)
    w = jnp.concatenate([w_group, z(RT_EXPERT_ROW - N_GROUPS), w_router,
                         z(LANES - RT_EXPERT_ROW - N_EXPERTS)], axis=1)
    b = jnp.concatenate([b_group, jnp.full((RT_EXPERT_ROW - N_GROUPS,), NEG_INF, f32), b_router,
                         jnp.zeros((LANES - RT_EXPERT_ROW - N_EXPERTS,), f32)])[None, :]
    w_hi = w.astype(bf16)
    w_lo = (w - w_hi.astype(f32)).astype(bf16)
    return jnp.concatenate([w_hi, w_lo], axis=1), b


def kernel(x, positions, attn_norm, w_in, da_lambda, da_subln, mla_q_norm, mla_w_uq, mla_kv_norm, mla_w_ukv, w_branch_a, w_branch_b, w_out, ffn_norm, w_group, b_group, w_router, b_router, w_exp_gate, w_exp_up, w_exp_down, final_norm):
    x2d = x.reshape(N_TOK, D_MODEL)
    cos_t, sin_t = _rope_tables(positions)

    proj = _in_proj(x2d, attn_norm[0][None, :], _prep_w_in(w_in[0].T), cos_t, sin_t)
    qb, kb, vb = _mla_prep(proj, mla_q_norm[0][None, :], mla_kv_norm[0][None, :],
                           _prep_w_uq(mla_w_uq[0]), _prep_w_ukv(mla_w_ukv[0]), cos_t, sin_t)
    oa = _attention(proj, proj, proj, COL_Q // HEAD_W, COL_K // HEAD_W, COL_V // HEAD_W,
                    heads=8, qtiles=2, extra=(da_lambda[0], da_subln[0][:, None]))
    ob = _attention(qb, kb, vb, 0, 0, 0, heads=8, qtiles=2)
    w_rt, b_rt = _prep_router(w_group[0], b_group[0], w_router[0], b_router[0])
    x1, hf, eid, cw = _merge(oa, ob, proj, x2d, w_branch_a[0].astype(bf16), w_branch_b[0].astype(bf16),
                             w_out[0].astype(bf16), ffn_norm[0][None, :], w_rt, b_rt)

    dest, ends = _plan(eid)
    dest = dest.reshape(TOP_K, N_TOK)
    ends = ends[0, :N_EXPERTS]
    xs = _dispatch(ends, dest, hf)
    y = _expert_ffn(ends, xs, w_exp_gate[0], w_exp_up[0], w_exp_down[0])
    out = _combine(dest, x1, cw.T, final_norm[None, :], y)
    return out.reshape(BATCH, SEQ, D_MODEL)
```

```python
import functools
import math

import jax
import jax.numpy as jnp
from jax import lax
from jax.experimental import pallas as pl
from jax.experimental.pallas import tpu as pltpu

f32 = jnp.float32
bf16 = jnp.bfloat16
i32 = jnp.int32

D_MODEL = 2048
BATCH = 4
SEQ = 2048
N_TOK = BATCH * SEQ
HEADS = 8
HEAD_W = 128
DA_HALF = 64
MLA_Q_RANK = 512
MLA_KV_RANK = 256
MLA_NOPE = 64
MLA_ROPE = 32
N_GROUPS = 4
EXPERTS_PER_GROUP = 8
N_EXPERTS = N_GROUPS * EXPERTS_PER_GROUP
TOP_K = 2
EXPERT_FF = 512
ROPE_THETA = 10000.0
NORM_EPS = 1e-6
NEG_INF = -1e30
LAM_INIT = 0.8 - 0.6 * math.exp(-0.3 * 0)
LOG2E = math.log2(math.e)

LANES = 128
VMEM_LIMIT = 56 * 1024 * 1024

COL_Q, COL_K, COL_V = 0, 1024, 2048
COL_CQ, COL_CKV = 3072, 3584
COL_KR = 3840
COL_GA, COL_GB = 4096, 6144
PROJ_W = 8192
KR_LANE = MLA_NOPE

TM_PROJ, TN_PROJ = 1024, 1024
TM_PREP = 512
TQ = 256
TM_MERGE = 256
ROW_BLK = 128
N_ROWS = (N_TOK * TOP_K // ROW_BLK + N_EXPERTS) * ROW_BLK
N_BLKS = N_ROWS // ROW_BLK
TT_DISP = 1024
TM_COMB = 128

Q_SCALE_A = (DA_HALF ** -0.5) * LOG2E
Q_SCALE_B = ((MLA_NOPE + MLA_ROPE) ** -0.5) * LOG2E


def _cparams(sem, **kw):
    return pltpu.CompilerParams(dimension_semantics=sem, vmem_limit_bytes=VMEM_LIMIT, **kw)


def _rms(x, g):
    return x * lax.rsqrt(jnp.mean(x * x, axis=-1, keepdims=True) + NORM_EPS) * g


SLAB = D_MODEL // LANES


ISSUE_UNROLL = 16


def _slab_rows(row0, nrows):
    return pl.ds(pl.multiple_of(row0 * SLAB, SLAB), nrows * SLAB)


def _slab_store(ref, idx, val):
    rows = val.shape[0]
    for j in range(SLAB):
        ref[idx + (pl.ds(j, rows, stride=SLAB), slice(None))] = val[:, j * LANES:(j + 1) * LANES]


def _slab_chunk(ref, idx, rows, j):
    return ref[idx + (pl.ds(j, rows, stride=SLAB), slice(None))]


def _slab_load(ref, idx, rows):
    return jnp.concatenate([_slab_chunk(ref, idx, rows, j) for j in range(SLAB)], axis=1)


def _rope_lanes(blk, cos, sin, first_half, half):
    up = pltpu.roll(blk, LANES - half, 1)
    down = pltpu.roll(blk, half, 1)
    return blk * cos + jnp.where(first_half, up, down) * sin


ROPE_NA = DA_HALF // 2
ROPE_NB = MLA_ROPE // 2


def _rope_tables_kernel(pos_ref, inv_ref, cos_ref, sin_ref):
    half = pos_ref.shape[0] // 2
    lane = lax.broadcasted_iota(i32, (half, LANES), 1)
    pos = jnp.where(lane < LANES // 2, pos_ref[:half, :], pos_ref[half:, :]).astype(f32)
    ang = pos * inv_ref[...]
    c = jnp.cos(ang)
    s = jnp.sin(ang)
    one = lambda n: jnp.ones((half, n), f32)
    zero = lambda n: jnp.zeros((half, n), f32)
    tail = LANES - KR_LANE - MLA_ROPE
    for part in range(2):
        o = part * (LANES // 2)
        rows = slice(part * half, (part + 1) * half)
        ca, sa = c[:, o:o + ROPE_NA], s[:, o:o + ROPE_NA]
        cb, sb = c[:, o + ROPE_NA:o + ROPE_NA + ROPE_NB], s[:, o + ROPE_NA:o + ROPE_NA + ROPE_NB]
        cos_ref[rows, :] = jnp.concatenate([ca, ca, ca, ca, one(KR_LANE), cb, cb, one(tail)], axis=1)
        sin_ref[rows, :] = jnp.concatenate([-sa, -sa, sa, sa, zero(KR_LANE), -sb, sb, zero(tail)],
                                           axis=1)


def _rope_tables(positions):
    inv_a = ROPE_THETA ** (-jnp.arange(0, DA_HALF, 2, dtype=f32) / DA_HALF)
    inv_b = ROPE_THETA ** (-jnp.arange(0, MLA_ROPE, 2, dtype=f32) / MLA_ROPE)
    inv = jnp.concatenate([inv_a, inv_b, jnp.zeros((LANES // 2 - ROPE_NA - ROPE_NB,), f32)])
    inv = jnp.tile(inv, 2)[None, :]
    pos = positions.reshape(N_TOK, 1)
    tm = 1024
    return pl.pallas_call(
        _rope_tables_kernel,
        grid=(N_TOK // tm,),
        in_specs=[pl.BlockSpec((tm, 1), lambda i: (i, 0)),
                  pl.BlockSpec((1, LANES), lambda i: (0, 0))],
        out_specs=[pl.BlockSpec((tm, 2 * LANES), lambda i: (i, 0)),
                   pl.BlockSpec((tm, 2 * LANES), lambda i: (i, 0))],
        out_shape=[jax.ShapeDtypeStruct((N_TOK, 2 * LANES), f32)] * 2,
        compiler_params=_cparams(("arbitrary",)),
        name="rope_tables",
    )(pos, inv)


def _in_proj_kernel(x_ref, g_ref, w_ref, cos_ref, sin_ref, o_ref, h_scr):
    n = pl.program_id(1)

    @pl.when(n == 0)
    def _():
        h_scr[...] = _rms(x_ref[...], g_ref[...]).astype(bf16)

    acc = lax.dot_general(h_scr[...], w_ref[...], (((1,), (1,)), ((), ())),
                          preferred_element_type=f32)
    n_rope = COL_V // TN_PROJ
    n_q = COL_K // TN_PROJ

    @pl.when(n < n_rope)
    def _():
        c = jnp.where(n < n_q, Q_SCALE_A, 1.0).astype(f32)
        cos = cos_ref[...] * c
        sin = sin_ref[...] * c
        for j in range(TN_PROJ // LANES):
            blk = acc[:, j * LANES:(j + 1) * LANES]
            rot = pltpu.roll(blk, LANES // 2, 1)
            o_ref[:, j * LANES:(j + 1) * LANES] = (blk * cos + rot * sin).astype(bf16)

    @pl.when(n >= n_rope)
    def _():
        o_ref[...] = acc.astype(bf16)


def _in_proj(x2d, g, w_pad, cos_t, sin_t):
    return pl.pallas_call(
        _in_proj_kernel,
        grid=(N_TOK // TM_PROJ, PROJ_W // TN_PROJ),
        in_specs=[pl.BlockSpec((TM_PROJ, D_MODEL), lambda i, n: (i, 0)),
                  pl.BlockSpec((1, D_MODEL), lambda i, n: (0, 0)),
                  pl.BlockSpec((TN_PROJ, D_MODEL), lambda i, n: (n, 0)),
                  pl.BlockSpec((TM_PROJ, LANES), lambda i, n: (i, 0)),
                  pl.BlockSpec((TM_PROJ, LANES), lambda i, n: (i, 0))],
        out_specs=pl.BlockSpec((TM_PROJ, TN_PROJ), lambda i, n: (i, n)),
        out_shape=jax.ShapeDtypeStruct((N_TOK, PROJ_W), bf16),
        scratch_shapes=[pltpu.VMEM((TM_PROJ, D_MODEL), bf16)],
        compiler_params=_cparams(("arbitrary", "arbitrary")),
        name="in_proj",
    )(x2d, g, w_pad, cos_t, sin_t)


def _mla_prep_kernel(cq_ref, ckv_ref, kr_ref, gq_ref, gkv_ref, wuq_ref, wkv_ref, cos_ref, sin_ref,
                     q_ref, k_ref, v_ref):
    cqn = _rms(cq_ref[...].astype(f32), gq_ref[...]).astype(bf16)
    q = jnp.dot(cqn, wuq_ref[...], preferred_element_type=f32)
    ckvn = _rms(ckv_ref[...].astype(f32), gkv_ref[...]).astype(bf16)
    kv = jnp.dot(ckvn, wkv_ref[...], preferred_element_type=f32)
    cos = cos_ref[...]
    sin = sin_ref[...]
    lane = lax.broadcasted_iota(i32, (TM_PREP, LANES), 1)
    first = lane < (KR_LANE + MLA_ROPE // 2)
    rope = lambda b: _rope_lanes(b, cos, sin, first, MLA_ROPE // 2)
    k_rot = rope(kr_ref[...].astype(f32))
    for h in range(HEADS):
        sl = slice(h * HEAD_W, (h + 1) * HEAD_W)
        q_ref[:, sl] = (rope(q[:, sl]) * Q_SCALE_B).astype(bf16)
        k_ref[:, sl] = (kv[:, sl] + k_rot).astype(bf16)
    v_ref[...] = kv[:, HEADS * HEAD_W:].astype(bf16)


def _mla_prep(proj, gq, gkv, wuq_p, wkv_p, cos_t, sin_t):
    tm = TM_PREP
    w = HEADS * HEAD_W
    return pl.pallas_call(
        _mla_prep_kernel,
        grid=(N_TOK // tm,),
        in_specs=[pl.BlockSpec((tm, MLA_Q_RANK), lambda i: (i, COL_CQ // MLA_Q_RANK)),
                  pl.BlockSpec((tm, MLA_KV_RANK), lambda i: (i, COL_CKV // MLA_KV_RANK)),
                  pl.BlockSpec((tm, LANES), lambda i: (i, COL_KR // LANES)),
                  pl.BlockSpec((1, MLA_Q_RANK), lambda i: (0, 0)),
                  pl.BlockSpec((1, MLA_KV_RANK), lambda i: (0, 0)),
                  pl.BlockSpec((MLA_Q_RANK, w), lambda i: (0, 0)),
                  pl.BlockSpec((MLA_KV_RANK, 2 * w), lambda i: (0, 0)),
                  pl.BlockSpec((tm, LANES), lambda i: (i, 1)),
                  pl.BlockSpec((tm, LANES), lambda i: (i, 1))],
        out_specs=[pl.BlockSpec((tm, w), lambda i: (i, 0))] * 3,
        out_shape=[jax.ShapeDtypeStruct((N_TOK, w), bf16)] * 3,
        compiler_params=_cparams(("arbitrary",)),
        name="mla_prep",
    )(proj, proj, proj, gq, gkv, wuq_p, wkv_p, cos_t, sin_t)


def _attn_kernel(*refs, ncomp, heads, qtiles):
    if ncomp == 2:
        q_ref, k_ref, v_ref, lam_ref, g_ref, o_ref, acc_scr, vt_scr = refs
    else:
        q_ref, k_ref, v_ref, o_ref, acc_scr, vt_scr = refs
    qi = pl.program_id(2)

    @pl.when(qi == 0)
    def _():
        for j in range(SEQ // TQ):
            for h in range(heads):
                vt_scr[j, h] = v_ref[j * TQ:(j + 1) * TQ, h * HEAD_W:(h + 1) * HEAD_W].T

    lane = lax.broadcasted_iota(i32, (TQ, HEAD_W), 1)
    qs = []
    for t in range(qtiles):
        for h in range(heads):
            q = q_ref[t * TQ:(t + 1) * TQ, h * HEAD_W:(h + 1) * HEAD_W]
            if ncomp == 2:
                zero = jnp.zeros_like(q)
                in_a = (lane % DA_HALF) < (DA_HALF // 2)
                qs += [jnp.where(in_a, q, zero), jnp.where(in_a, zero, q)]
            else:
                qs.append(q)
    per_tile = heads * ncomp
    head_of = lambda n: (n % per_tile) // ncomp

    def step(j, carry, modes):
        off = pl.multiple_of(j * TQ, TQ)
        live = [n for n in range(len(qs)) if modes[n // per_tile] != "skip"]
        ss = {}
        for n in live:
            h = head_of(n)
            k = k_ref[pl.ds(off, TQ), h * HEAD_W:(h + 1) * HEAD_W]
            s = lax.dot_general(k, qs[n], (((1,), (1,)), ((), ())), preferred_element_type=f32)
            if modes[n // per_tile] == "diag":
                key = lax.broadcasted_iota(i32, (TQ, TQ), 0)
                qry = lax.broadcasted_iota(i32, (TQ, TQ), 1)
                s = jnp.where(key <= qry, s, NEG_INF)
            ss[n] = s
        stats = {}
        for n in live:
            m, l = carry[n]
            s = ss[n]
            m_new = jnp.maximum(m, jnp.max(s, axis=0, keepdims=True))
            alpha = jnp.exp2(m - m_new)
            p = jnp.exp2(s - m_new)
            l_new = alpha * l + jnp.sum(p, axis=0, keepdims=True)
            stats[n] = (m_new, l_new, alpha, p.astype(bf16))
        out = list(carry)
        for n in live:
            m_new, l_new, alpha, p = stats[n]
            pv = jnp.dot(vt_scr[j, head_of(n)], p, preferred_element_type=f32)
            acc_scr[n] = alpha * acc_scr[n] + pv
            out[n] = (m_new, l_new)
        return tuple(out)

    acc_scr[...] = jnp.zeros_like(acc_scr)
    init = tuple((jnp.full((1, TQ), NEG_INF, f32), jnp.zeros((1, TQ), f32)) for _ in qs)
    first = qi * qtiles
    carry = lax.fori_loop(0, first, lambda j, c: step(j, c, ("full",) * qtiles), init)
    for u in range(qtiles):
        modes = tuple("skip" if t < u else ("diag" if t == u else "full") for t in range(qtiles))
        carry = step(first + u, carry, modes)
    carry = tuple((m, l, acc_scr[n]) for n, (m, l) in enumerate(carry))

    if ncomp == 2:
        ll = lam_ref[...]
        s1 = jnp.sum(ll[0:1, :] * ll[1:2, :], axis=-1, keepdims=True)
        s2 = jnp.sum(ll[2:3, :] * ll[3:4, :], axis=-1, keepdims=True)
        lam = jnp.exp(s1) - jnp.exp(s2) + LAM_INIT
    for t in range(qtiles):
        for h in range(heads):
            n0 = (t * heads + h) * ncomp
            if ncomp == 2:
                (_, l0, a0), (_, l1, a1) = carry[n0], carry[n0 + 1]
                o = a0 / l0 - lam * (a1 / l1)
                ms = jnp.mean(o * o, axis=0, keepdims=True)
                o = o * lax.rsqrt(ms + NORM_EPS) * g_ref[...] * (1.0 - LAM_INIT)
            else:
                _, l0, a0 = carry[n0]
                o = a0 / l0
            o_ref[t * TQ:(t + 1) * TQ, h * HEAD_W:(h + 1) * HEAD_W] = o.T.astype(bf16)


def _attention(q_arr, k_arr, v_arr, q_blk0, k_blk0, v_blk0, heads, qtiles, extra=()):
    ncomp = 2 if extra else 1
    tq = qtiles * TQ
    nq = SEQ // tq
    w = heads * HEAD_W
    qb, kb, vb = q_blk0 // heads, k_blk0 // heads, v_blk0 // heads
    in_specs = [pl.BlockSpec((tq, w), lambda b, h, i: (b * nq + i, qb + h)),
                pl.BlockSpec((SEQ, w), lambda b, h, i: (b, kb + h)),
                pl.BlockSpec((SEQ, w), lambda b, h, i: (b, vb + h))]
    for a in extra:
        in_specs.append(pl.BlockSpec(a.shape, lambda b, h, i: (0, 0)))
    return pl.pallas_call(
        functools.partial(_attn_kernel, ncomp=ncomp, heads=heads, qtiles=qtiles),
        grid=(BATCH, HEADS // heads, nq),
        in_specs=in_specs,
        out_specs=pl.BlockSpec((tq, w), lambda b, h, i: (b * nq + i, h)),
        out_shape=jax.ShapeDtypeStruct((N_TOK, HEADS * HEAD_W), bf16),
        scratch_shapes=[pltpu.VMEM((qtiles * heads * ncomp, HEAD_W, TQ), f32),
                        pltpu.VMEM((SEQ // TQ, heads, HEAD_W, TQ), bf16)],
        compiler_params=_cparams(("arbitrary", "arbitrary", "arbitrary")),
        name="diff_attn" if extra else "mla_attn",
    )(q_arr, k_arr, v_arr, *extra)


RT_GROUP_ROW = 0
RT_EXPERT_ROW = 8


def _route(hf, w_ref, b_ref, eid_ref, cw_ref):
    hf_hi = hf.astype(bf16)
    hf_lo = (hf - hf_hi.astype(f32)).astype(bf16)
    a = jnp.dot(hf_hi, w_ref[...], preferred_element_type=f32)
    b = jnp.dot(hf_lo, w_ref[:, :LANES], preferred_element_type=f32)
    logits = a[:, :LANES] + (a[:, LANES:] + b) + b_ref[...]
    lt = logits.T
    tm = lt.shape[1]
    ridx = lax.broadcasted_iota(i32, (EXPERTS_PER_GROUP, tm), 0).astype(f32)

    def top1(v):
        vmax = jnp.max(v, axis=0, keepdims=True)
        idx = jnp.min(jnp.where(v == vmax, ridx, float(EXPERTS_PER_GROUP)), axis=0, keepdims=True)
        return vmax, idx

    gl = lt[RT_GROUP_ROW:RT_GROUP_ROW + 8, :]
    gmax, gsel = top1(gl)
    p_g = 1.0 / jnp.sum(jnp.exp(gl - gmax), axis=0, keepdims=True)
    el = jnp.zeros((EXPERTS_PER_GROUP, tm), f32)
    for g in range(N_GROUPS):
        r0 = RT_EXPERT_ROW + g * EXPERTS_PER_GROUP
        el = jnp.where(gsel == float(g), lt[r0:r0 + EXPERTS_PER_GROUP, :], el)
    e1, i1 = top1(el)
    e2, i2 = top1(jnp.where(ridx == i1, -jnp.inf, el))
    r = jnp.exp(e2 - e1)
    w1 = 1.0 / (1.0 + r)
    base = gsel * EXPERTS_PER_GROUP
    eid_ref[0:1, :] = (base + i1).astype(i32)
    eid_ref[1:2, :] = (base + i2).astype(i32)
    cw_ref[0:1, :] = p_g * w1
    cw_ref[1:2, :] = p_g * (r * w1)


def _merge_kernel(oa_ref, ob_ref, ga_ref, gb_ref, x_ref, wa_ref, wb_ref, wo_ref, gf_ref, wr_ref, br_ref,
                  x1_ref, hf_ref, eid_ref, cw_ref):
    ya = jnp.dot(oa_ref[...], wa_ref[...], preferred_element_type=f32)
    yb = jnp.dot(ob_ref[...], wb_ref[...], preferred_element_type=f32)
    mixed = (jax.nn.sigmoid(ga_ref[...].astype(f32)) * ya
             + jax.nn.sigmoid(gb_ref[...].astype(f32)) * yb)
    x1 = x_ref[...] + jnp.dot(mixed.astype(bf16), wo_ref[...], preferred_element_type=f32)
    x1_ref[...] = x1
    hf = _rms(x1, gf_ref[...])
    _slab_store(hf_ref, (), hf)
    _route(hf, wr_ref, br_ref, eid_ref, cw_ref)


def _merge(oa, ob, proj, x2d, wa, wb, wo, gf, w_rt, b_rt):
    tm = TM_MERGE
    w = HEADS * HEAD_W
    const = lambda shape: pl.BlockSpec(shape, lambda i: (0, 0), pipeline_mode=pl.Buffered(1))
    return pl.pallas_call(
        _merge_kernel,
        grid=(N_TOK // tm,),
        in_specs=[pl.BlockSpec((tm, w), lambda i: (i, 0)),
                  pl.BlockSpec((tm, w), lambda i: (i, 0)),
                  pl.BlockSpec((tm, D_MODEL), lambda i: (i, COL_GA // D_MODEL)),
                  pl.BlockSpec((tm, D_MODEL), lambda i: (i, COL_GB // D_MODEL)),
                  pl.BlockSpec((tm, D_MODEL), lambda i: (i, 0)),
                  const((w, D_MODEL)), const((w, D_MODEL)), const((D_MODEL, D_MODEL)),
                  const((1, D_MODEL)), const((D_MODEL, 2 * LANES)), const((1, LANES))],
        out_specs=[pl.BlockSpec((tm, D_MODEL), lambda i: (i, 0)),
                   pl.BlockSpec((tm * SLAB, LANES), lambda i: (i, 0)),
                   pl.BlockSpec((TOP_K, tm), lambda i: (0, i)),
                   pl.BlockSpec((TOP_K, tm), lambda i: (0, i))],
        out_shape=[jax.ShapeDtypeStruct((N_TOK, D_MODEL), f32),
                   jax.ShapeDtypeStruct((N_TOK * SLAB, LANES), f32),
                   jax.ShapeDtypeStruct((TOP_K, N_TOK), i32),
                   jax.ShapeDtypeStruct((TOP_K, N_TOK), f32)],
        compiler_params=_cparams(("arbitrary",)),
        name="merge_route",
    )(oa, ob, proj, proj, x2d, wa, wb, wo, gf, w_rt, b_rt)


PLAN_N = 128


def _plan_kernel(eid_ref, dest_ref, ends_ref):
    eid = eid_ref[...]
    r = lax.broadcasted_iota(i32, (PLAN_N, PLAN_N), 0)
    c = lax.broadcasted_iota(i32, (PLAN_N, PLAN_N), 1)
    upper = (r < c).astype(bf16)
    lower = (c < r).astype(bf16)
    ones = jnp.ones((PLAN_N, PLAN_N), bf16)
    lane = lax.broadcasted_iota(i32, (1, LANES), 1)
    dest = jnp.zeros((PLAN_N, PLAN_N), f32)
    ends = jnp.zeros((1, LANES), f32)
    start = jnp.zeros((1, 1), f32)
    for e in range(N_EXPERTS):
        m = (eid == e).astype(f32)
        mb = m.astype(bf16)
        within = jnp.dot(mb, upper, preferred_element_type=f32)
        rowtot = jnp.dot(mb, ones, preferred_element_type=f32)
        rowpre = jnp.dot(lower, rowtot.astype(bf16), preferred_element_type=f32)
        cnt = jnp.sum(jnp.sum(m, axis=1, keepdims=True), axis=0, keepdims=True)
        dest = dest + m * (start + within + rowpre)
        start = start + jnp.floor((cnt + (ROW_BLK - 1)) * (1.0 / ROW_BLK)) * ROW_BLK
        ends = jnp.where(lane == e, start, ends)
    dest_ref[...] = dest.astype(i32)
    ends_ref[...] = ends.astype(i32)


def _plan(eid):
    whole = lambda shape: pl.BlockSpec(shape, lambda: (0, 0))
    return pl.pallas_call(
        _plan_kernel,
        in_specs=[whole((PLAN_N, PLAN_N))],
        out_specs=[whole((PLAN_N, PLAN_N)), whole((1, LANES))],
        out_shape=[jax.ShapeDtypeStruct((PLAN_N, PLAN_N), i32),
                   jax.ShapeDtypeStruct((1, LANES), i32)],
        name="moe_plan",
    )(eid.reshape(PLAN_N, PLAN_N))


def _dispatch_kernel(ends_ref, dest_ref, hf_ref, xs_hbm, zbuf, sem):
    i = pl.program_id(0)

    def tail_copy(e):
        pend = ends_ref[e]
        prev = jnp.where(e > 0, ends_ref[jnp.maximum(e - 1, 0)], 0)
        row0 = pl.multiple_of(jnp.maximum(pend - ROW_BLK, 0), ROW_BLK)
        return pend > prev, pltpu.make_async_copy(zbuf, xs_hbm.at[_slab_rows(row0, ROW_BLK), :], sem)

    @pl.when(i == 0)
    def _():
        zbuf[...] = jnp.zeros_like(zbuf)

        def zstart(e, _):
            used, cp = tail_copy(e)

            @pl.when(used)
            def _():
                cp.start()
            return 0

        def zwait(e, _):
            used, cp = tail_copy(e)

            @pl.when(used)
            def _():
                cp.wait()
            return 0

        def unused_copy(b):
            row0 = pl.multiple_of(b * ROW_BLK, ROW_BLK)
            return pltpu.make_async_copy(zbuf, xs_hbm.at[_slab_rows(row0, ROW_BLK), :], sem)

        def ustart(b, _):
            unused_copy(b).start()
            return 0

        def uwait(b, _):
            unused_copy(b).wait()
            return 0

        n_used = ends_ref[N_EXPERTS - 1] // ROW_BLK
        lax.fori_loop(0, N_EXPERTS, zstart, 0)
        lax.fori_loop(n_used, N_BLKS, ustart, 0)
        lax.fori_loop(0, N_EXPERTS, zwait, 0)
        lax.fori_loop(n_used, N_BLKS, uwait, 0)

    base = i * TT_DISP

    def row_copy(t, k):
        d = dest_ref[k, base + t]
        return pltpu.make_async_copy(hf_ref.at[_slab_rows(t, 1), :], xs_hbm.at[_slab_rows(d, 1), :], sem)

    def issue(t, _):
        for k in range(TOP_K):
            row_copy(t, k).start(priority=k)
        return 0

    def drain(t, _):
        for k in range(TOP_K):
            row_copy(t, k).wait()
        return 0

    lax.fori_loop(0, TT_DISP, issue, 0, unroll=ISSUE_UNROLL)
    lax.fori_loop(0, TT_DISP, drain, 0, unroll=ISSUE_UNROLL)


def _dispatch(ends, dest, hf):
    return pl.pallas_call(
        _dispatch_kernel,
        grid_spec=pltpu.PrefetchScalarGridSpec(
            num_scalar_prefetch=2,
            grid=(N_TOK // TT_DISP,),
            in_specs=[pl.BlockSpec((TT_DISP * SLAB, LANES), lambda i, e, d: (i, 0))],
            out_specs=pl.BlockSpec(memory_space=pl.ANY),
            scratch_shapes=[pltpu.VMEM((ROW_BLK * SLAB, LANES), f32),
                            pltpu.SemaphoreType.DMA(())]),
        out_shape=jax.ShapeDtypeStruct((N_ROWS * SLAB, LANES), f32),
        compiler_params=_cparams(("arbitrary",)),
        name="moe_dispatch",
    )(ends, dest, hf)


W_RING = 3
X_RING = 6
Y_RING = 3


def _ffn_kernel(ends_ref, wg_hbm, wu_hbm, wd_hbm, xs_hbm, y_hbm,
                wg_f, wu_f, wd_f, wg_s, wu_s, wd_s, xbuf, ybuf, sem_w, sem_x, sem_y):
    n_used = ends_ref[N_EXPERTS - 1] // ROW_BLK

    def w_copies(ex):
        slot = ex % W_RING
        return [pltpu.make_async_copy(hbm.at[ex], buf.at[slot], sem_w.at[i, slot])
                for i, (hbm, buf) in enumerate(((wg_hbm, wg_f), (wu_hbm, wu_f), (wd_hbm, wd_f)))]

    def advance(e_from, e_to):
        def one(ex, _):
            for cp in w_copies(ex):
                cp.wait()

            @pl.when(ex + (W_RING - 1) < N_EXPERTS)
            def _():
                for cp in w_copies(ex + (W_RING - 1)):
                    cp.start()
            return 0

        lax.fori_loop(e_from + 1, e_to + 1, one, 0)

    def x_copy(b):
        slot = b % X_RING
        return pltpu.make_async_copy(xs_hbm.at[_slab_rows(b * ROW_BLK, ROW_BLK), :], xbuf.at[slot],
                                     sem_x.at[slot])

    def y_copy(b):
        slot = b % Y_RING
        return pltpu.make_async_copy(ybuf.at[slot], y_hbm.at[_slab_rows(b * ROW_BLK, ROW_BLK), :],
                                     sem_y.at[slot])

    for ex in range(W_RING - 1):
        for cp in w_copies(ex):
            cp.start()
    for b0 in range(X_RING - 1):
        @pl.when(b0 < n_used)
        def _():
            x_copy(b0).start()

    def body(b, e):
        @pl.when(b + (X_RING - 1) < n_used)
        def _():
            x_copy(b + (X_RING - 1)).start()

        e_new = lax.while_loop(lambda ex: b * ROW_BLK >= ends_ref[ex], lambda ex: ex + 1,
                               jnp.maximum(e, 0))

        @pl.when(e_new != e)
        def _():
            advance(e, e_new)
            wslot = e_new % W_RING
            wg_s[...] = wg_f[wslot].astype(bf16)
            wu_s[...] = wu_f[wslot].astype(bf16)
            wd_s[...] = wd_f[wslot].astype(bf16)

        x_copy(b).wait()

        @pl.when(b >= Y_RING)
        def _():
            y_copy(b - Y_RING).wait()

        x = _slab_load(xbuf, (b % X_RING,), ROW_BLK).astype(bf16)
        h1 = jnp.dot(x, wg_s[...], preferred_element_type=f32)
        h2 = jnp.dot(x, wu_s[...], preferred_element_type=f32)
        a = (h1 * jax.nn.sigmoid(h1) * h2).astype(bf16)
        _slab_store(ybuf, (b % Y_RING,), jnp.dot(a, wd_s[...], preferred_element_type=f32))
        y_copy(b).start()
        return e_new

    e_last = lax.fori_loop(0, n_used, body, jnp.int32(-1))
    advance(e_last, jnp.int32(N_EXPERTS - 1))

    for back in range(Y_RING, 0, -1):
        @pl.when(n_used >= back)
        def _():
            y_copy(n_used - back).wait()

    ybuf[0] = jnp.zeros((ROW_BLK * SLAB, LANES), f32)

    def unused_copy(b):
        return pltpu.make_async_copy(ybuf.at[0], y_hbm.at[_slab_rows(b * ROW_BLK, ROW_BLK), :],
                                     sem_y.at[0])

    def ustart(b, _):
        unused_copy(b).start()
        return 0

    def uwait(b, _):
        unused_copy(b).wait()
        return 0

    lax.fori_loop(n_used, N_BLKS, ustart, 0)
    lax.fori_loop(n_used, N_BLKS, uwait, 0)


def _expert_ffn(ends, xs, wg, wu, wd):
    return pl.pallas_call(
        _ffn_kernel,
        grid_spec=pltpu.PrefetchScalarGridSpec(
            num_scalar_prefetch=1,
            grid=(1,),
            in_specs=[pl.BlockSpec(memory_space=pl.ANY)] * 4,
            out_specs=pl.BlockSpec(memory_space=pl.ANY),
            scratch_shapes=[pltpu.VMEM((W_RING, D_MODEL, EXPERT_FF), f32),
                            pltpu.VMEM((W_RING, D_MODEL, EXPERT_FF), f32),
                            pltpu.VMEM((W_RING, EXPERT_FF, D_MODEL), f32),
                            pltpu.VMEM((D_MODEL, EXPERT_FF), bf16),
                            pltpu.VMEM((D_MODEL, EXPERT_FF), bf16),
                            pltpu.VMEM((EXPERT_FF, D_MODEL), bf16),
                            pltpu.VMEM((X_RING, ROW_BLK * SLAB, LANES), f32),
                            pltpu.VMEM((Y_RING, ROW_BLK * SLAB, LANES), f32),
                            pltpu.SemaphoreType.DMA((3, W_RING)),
                            pltpu.SemaphoreType.DMA((X_RING,)),
                            pltpu.SemaphoreType.DMA((Y_RING,))]),
        out_shape=jax.ShapeDtypeStruct((N_ROWS * SLAB, LANES), f32),
        compiler_params=_cparams(("arbitrary",)),
        name="expert_ffn",
    )(ends, wg, wu, wd, xs)


def _combine_kernel(dest_ref, x_ref, cw_ref, g_ref, y_hbm, o_ref, ybuf, sem):
    i = pl.program_id(0)
    n_steps = pl.num_programs(0)

    def row_copy(step, t, k):
        slot = step % 2
        d = dest_ref[k, step * TM_COMB + t]
        return pltpu.make_async_copy(y_hbm.at[_slab_rows(d, 1), :],
                                     ybuf.at[slot, k, _slab_rows(t, 1), :], sem.at[slot])

    def issue(step):
        def one(t, _):
            for k in range(TOP_K):
                row_copy(step, t, k).start()
            return 0
        lax.fori_loop(0, TM_COMB, one, 0, unroll=ISSUE_UNROLL)

    def drain(step):
        def one(t, _):
            for k in range(TOP_K):
                row_copy(step, t, k).wait()
            return 0
        lax.fori_loop(0, TM_COMB, one, 0, unroll=ISSUE_UNROLL)

    @pl.when(i == 0)
    def _():
        issue(i)

    @pl.when(i + 1 < n_steps)
    def _():
        issue(i + 1)

    drain(i)
    slot = i % 2
    w = cw_ref[...]
    x = (x_ref[...] + w[:, 0:1] * _slab_load(ybuf, (slot, 0), TM_COMB)
         + w[:, 1:2] * _slab_load(ybuf, (slot, 1), TM_COMB))
    o_ref[...] = _rms(x, g_ref[...])


def _combine(dest, x1, cw_t, g, y):
    tm = TM_COMB
    return pl.pallas_call(
        _combine_kernel,
        grid_spec=pltpu.PrefetchScalarGridSpec(
            num_scalar_prefetch=1,
            grid=(N_TOK // tm,),
            in_specs=[pl.BlockSpec((tm, D_MODEL), lambda i, d: (i, 0)),
                      pl.BlockSpec((tm, TOP_K), lambda i, d: (i, 0)),
                      pl.BlockSpec((1, D_MODEL), lambda i, d: (0, 0)),
                      pl.BlockSpec(memory_space=pl.ANY)],
            out_specs=pl.BlockSpec((tm, D_MODEL), lambda i, d: (i, 0)),
            scratch_shapes=[pltpu.VMEM((2, TOP_K, tm * SLAB, LANES), f32),
                            pltpu.SemaphoreType.DMA((2,))]),
        out_shape=jax.ShapeDtypeStruct((N_TOK, D_MODEL), f32),
        compiler_params=_cparams(("arbitrary",)),
        name="moe_combine",
    )(dest, x1, cw_t, g, y)


TR_RELAYOUT = 256
KR_TILE = COL_KR // TR_RELAYOUT


def _w_in_relayout_kernel(w_ref, o_ref):
    i = pl.program_id(0)

    @pl.when(i < COL_V // TR_RELAYOUT)
    def _():
        q = ROPE_NA
        for g in range(TR_RELAYOUT // LANES):
            r = g * LANES
            o_ref[r:r + q, :] = w_ref[r:r + q, :].astype(bf16)
            o_ref[r + q:r + 2 * q, :] = w_ref[r + 2 * q:r + 3 * q, :].astype(bf16)
            o_ref[r + 2 * q:r + 3 * q, :] = w_ref[r + q:r + 2 * q, :].astype(bf16)
            o_ref[r + 3 * q:r + 4 * q, :] = w_ref[r + 3 * q:r + 4 * q, :].astype(bf16)

    @pl.when((i >= COL_V // TR_RELAYOUT) & (i != KR_TILE))
    def _():
        o_ref[...] = w_ref[...].astype(bf16)

    @pl.when(i == KR_TILE)
    def _():
        o_ref[...] = jnp.zeros_like(o_ref)
        o_ref[KR_LANE:KR_LANE + MLA_ROPE, :] = w_ref[0:MLA_ROPE, :].astype(bf16)


def _prep_w_in(w_in_t):
    n_src, d = w_in_t.shape
    gate_src = COL_KR + MLA_ROPE

    gate_tile = COL_GA // TR_RELAYOUT
    sub = 8
    shift = (COL_GA - gate_src) // sub

    def src_row(i):
        return ((TR_RELAYOUT // sub) * i - shift * (i // gate_tile)) * sub

    return pl.pallas_call(
        _w_in_relayout_kernel,
        grid=(PROJ_W // TR_RELAYOUT,),
        in_specs=[pl.BlockSpec((pl.Element(TR_RELAYOUT), pl.Element(d)), lambda i: (src_row(i), 0))],
        out_specs=pl.BlockSpec((TR_RELAYOUT, d), lambda i: (i, 0)),
        out_shape=jax.ShapeDtypeStruct((PROJ_W, d), bf16),
        compiler_params=_cparams(("arbitrary",)),
        name="w_in_relayout",
    )(w_in_t)


def _prep_w_uq(w_uq):
    w = w_uq.reshape(MLA_Q_RANK, HEADS, MLA_NOPE + MLA_ROPE)
    w = jnp.pad(w, ((0, 0), (0, 0), (0, HEAD_W - MLA_NOPE - MLA_ROPE)))
    return w.reshape(MLA_Q_RANK, HEADS * HEAD_W).astype(bf16)


def _prep_w_ukv(w_ukv):
    w = w_ukv.reshape(MLA_KV_RANK, HEADS, MLA_NOPE + HEAD_W)
    wk = jnp.pad(w[:, :, :MLA_NOPE], ((0, 0), (0, 0), (0, HEAD_W - MLA_NOPE)))
    wv = w[:, :, MLA_NOPE:]
    return jnp.concatenate([wk.reshape(MLA_KV_RANK, -1), wv.reshape(MLA_KV_RANK, -1)], axis=1).astype(bf16)


def _prep_router(w_group, b_group, w_router, b_router):
    z = lambda n: jnp.zeros((D_MODEL, n), f32)
    w = jnp.concatenate([w_group, z(RT_EXPERT_ROW - N_GROUPS), w_router,
                         z(LANES - RT_EXPERT_ROW - N_EXPERTS)], axis=1)
    b = jnp.concatenate([b_group, jnp.full((RT_EXPERT_ROW - N_GROUPS,), NEG_INF, f32), b_router,
                         jnp.zeros((LANES - RT_EXPERT_ROW - N_EXPERTS,), f32)])[None, :]
    w_hi = w.astype(bf16)
    w_lo = (w - w_hi.astype(f32)).astype(bf16)
    return jnp.concatenate([w_hi, w_lo], axis=1), b


def kernel(x, positions, attn_norm, w_in, da_lambda, da_subln, mla_q_norm, mla_w_uq, mla_kv_norm, mla_w_ukv, w_branch_a, w_branch_b, w_out, ffn_norm, w_group, b_group, w_router, b_router, w_exp_gate, w_exp_up, w_exp_down, final_norm):
    x2d = x.reshape(N_TOK, D_MODEL)
    cos_t, sin_t = _rope_tables(positions)

    proj = _in_proj(x2d, attn_norm[0][None, :], _prep_w_in(w_in[0].T), cos_t, sin_t)
    qb, kb, vb = _mla_prep(proj, mla_q_norm[0][None, :], mla_kv_norm[0][None, :],
                           _prep_w_uq(mla_w_uq[0]), _prep_w_ukv(mla_w_ukv[0]), cos_t, sin_t)
    oa = _attention(proj, proj, proj, COL_Q // HEAD_W, COL_K // HEAD_W, COL_V // HEAD_W,
                    heads=8, qtiles=2, extra=(da_lambda[0], da_subln[0][:, None]))
    ob = _attention(qb, kb, vb, 0, 0, 0, heads=8, qtiles=2)
    w_rt, b_rt = _prep_router(w_group[0], b_group[0], w_router[0], b_router[0])
    x1, hf, eid, cw = _merge(oa, ob, proj, x2d, w_branch_a[0].astype(bf16), w_branch_b[0].astype(bf16),
                             w_out[0].astype(bf16), ffn_norm[0][None, :], w_rt, b_rt)

    dest, ends = _plan(eid)
    dest = dest.reshape(TOP_K, N_TOK)
    ends = ends[0, :N_EXPERTS]
    xs = _dispatch(ends, dest, hf)
    y = _expert_ffn(ends, xs, w_exp_gate[0], w_exp_up[0], w_exp_down[0])
    out = _combine(dest, x1, cw.T, final_norm[None, :], y)
    return out.reshape(BATCH, SEQ, D_MODEL)
```

```python
import functools
import math

import jax
import jax.numpy as jnp
from jax import lax
from jax.experimental import pallas as pl
from jax.experimental.pallas import tpu as pltpu

f32 = jnp.float32
bf16 = jnp.bfloat16
i32 = jnp.int32

D_MODEL = 2048
BATCH = 4
SEQ = 2048
N_TOK = BATCH * SEQ
HEADS = 8
HEAD_W = 128
DA_HALF = 64
MLA_Q_RANK = 512
MLA_KV_RANK = 256
MLA_NOPE = 64
MLA_ROPE = 32
N_GROUPS = 4
EXPERTS_PER_GROUP = 8
N_EXPERTS = N_GROUPS * EXPERTS_PER_GROUP
TOP_K = 2
EXPERT_FF = 512
ROPE_THETA = 10000.0
NORM_EPS = 1e-6
NEG_INF = -1e30
LAM_INIT = 0.8 - 0.6 * math.exp(-0.3 * 0)
LOG2E = math.log2(math.e)

LANES = 128
VMEM_LIMIT = 56 * 1024 * 1024

COL_Q, COL_K, COL_V = 0, 1024, 2048
COL_CQ, COL_CKV = 3072, 3584
COL_KR = 3840
COL_GA, COL_GB = 4096, 6144
PROJ_W = 8192
KR_LANE = MLA_NOPE

TM_PROJ, TN_PROJ = 1024, 1024
TM_PREP = 512
TQ = 256
TM_MERGE = 256
ROW_BLK = 128
N_ROWS = (N_TOK * TOP_K // ROW_BLK + N_EXPERTS) * ROW_BLK
N_BLKS = N_ROWS // ROW_BLK
TT_DISP = 1024
TM_COMB = 128

Q_SCALE_A = (DA_HALF ** -0.5) * LOG2E
Q_SCALE_B = ((MLA_NOPE + MLA_ROPE) ** -0.5) * LOG2E


def _cparams(sem, **kw):
    return pltpu.CompilerParams(dimension_semantics=sem, vmem_limit_bytes=VMEM_LIMIT, **kw)


def _rms(x, g):
    return x * lax.rsqrt(jnp.mean(x * x, axis=-1, keepdims=True) + NORM_EPS) * g


SLAB = D_MODEL // LANES


ISSUE_UNROLL = 16


def _slab_rows(row0, nrows):
    return pl.ds(pl.multiple_of(row0 * SLAB, SLAB), nrows * SLAB)


def _slab_store(ref, idx, val):
    rows = val.shape[0]
    for j in range(SLAB):
        ref[idx + (pl.ds(j, rows, stride=SLAB), slice(None))] = val[:, j * LANES:(j + 1) * LANES]


def _slab_chunk(ref, idx, rows, j):
    return ref[idx + (pl.ds(j, rows, stride=SLAB), slice(None))]


def _slab_load(ref, idx, rows):
    return jnp.concatenate([_slab_chunk(ref, idx, rows, j) for j in range(SLAB)], axis=1)


def _rope_lanes(blk, cos, sin, first_half, half):
    up = pltpu.roll(blk, LANES - half, 1)
    down = pltpu.roll(blk, half, 1)
    return blk * cos + jnp.where(first_half, up, down) * sin


ROPE_NA = DA_HALF // 2
ROPE_NB = MLA_ROPE // 2


def _rope_tables_kernel(pos_ref, inv_ref, cos_ref, sin_ref):
    half = pos_ref.shape[0] // 2
    lane = lax.broadcasted_iota(i32, (half, LANES), 1)
    pos = jnp.where(lane < LANES // 2, pos_ref[:half, :], pos_ref[half:, :]).astype(f32)
    ang = pos * inv_ref[...]
    c = jnp.cos(ang)
    s = jnp.sin(ang)
    one = lambda n: jnp.ones((half, n), f32)
    zero = lambda n: jnp.zeros((half, n), f32)
    tail = LANES - KR_LANE - MLA_ROPE
    for part in range(2):
        o = part * (LANES // 2)
        rows = slice(part * half, (part + 1) * half)
        ca, sa = c[:, o:o + ROPE_NA], s[:, o:o + ROPE_NA]
        cb, sb = c[:, o + ROPE_NA:o + ROPE_NA + ROPE_NB], s[:, o + ROPE_NA:o + ROPE_NA + ROPE_NB]
        cos_ref[rows, :] = jnp.concatenate([ca, ca, ca, ca, one(KR_LANE), cb, cb, one(tail)], axis=1)
        sin_ref[rows, :] = jnp.concatenate([-sa, -sa, sa, sa, zero(KR_LANE), -sb, sb, zero(tail)],
                                           axis=1)


def _rope_tables(positions):
    inv_a = ROPE_THETA ** (-jnp.arange(0, DA_HALF, 2, dtype=f32) / DA_HALF)
    inv_b = ROPE_THETA ** (-jnp.arange(0, MLA_ROPE, 2, dtype=f32) / MLA_ROPE)
    inv = jnp.concatenate([inv_a, inv_b, jnp.zeros((LANES // 2 - ROPE_NA - ROPE_NB,), f32)])
    inv = jnp.tile(inv, 2)[None, :]
    pos = positions.reshape(N_TOK, 1)
    tm = 1024
    return pl.pallas_call(
        _rope_tables_kernel,
        grid=(N_TOK // tm,),
        in_specs=[pl.BlockSpec((tm, 1), lambda i: (i, 0)),
                  pl.BlockSpec((1, LANES), lambda i: (0, 0))],
        out_specs=[pl.BlockSpec((tm, 2 * LANES), lambda i: (i, 0)),
                   pl.BlockSpec((tm, 2 * LANES), lambda i: (i, 0))],
        out_shape=[jax.ShapeDtypeStruct((N_TOK, 2 * LANES), f32)] * 2,
        compiler_params=_cparams(("arbitrary",)),
        name="rope_tables",
    )(pos, inv)


def _in_proj_kernel(x_ref, g_ref, w_ref, cos_ref, sin_ref, o_ref, h_scr):
    n = pl.program_id(1)

    @pl.when(n == 0)
    def _():
        h_scr[...] = _rms(x_ref[...], g_ref[...]).astype(bf16)

    acc = lax.dot_general(h_scr[...], w_ref[...], (((1,), (1,)), ((), ())),
                          preferred_element_type=f32)
    n_rope = COL_V // TN_PROJ
    n_q = COL_K // TN_PROJ

    @pl.when(n < n_rope)
    def _():
        c = jnp.where(n < n_q, Q_SCALE_A, 1.0).astype(f32)
        cos = cos_ref[...] * c
        sin = sin_ref[...] * c
        for j in range(TN_PROJ // LANES):
            blk = acc[:, j * LANES:(j + 1) * LANES]
            rot = pltpu.roll(blk, LANES // 2, 1)
            o_ref[:, j * LANES:(j + 1) * LANES] = (blk * cos + rot * sin).astype(bf16)

    @pl.when(n >= n_rope)
    def _():
        o_ref[...] = acc.astype(bf16)


def _in_proj(x2d, g, w_pad, cos_t, sin_t):
    return pl.pallas_call(
        _in_proj_kernel,
        grid=(N_TOK // TM_PROJ, PROJ_W // TN_PROJ),
        in_specs=[pl.BlockSpec((TM_PROJ, D_MODEL), lambda i, n: (i, 0)),
                  pl.BlockSpec((1, D_MODEL), lambda i, n: (0, 0)),
                  pl.BlockSpec((TN_PROJ, D_MODEL), lambda i, n: (n, 0)),
                  pl.BlockSpec((TM_PROJ, LANES), lambda i, n: (i, 0)),
                  pl.BlockSpec((TM_PROJ, LANES), lambda i, n: (i, 0))],
        out_specs=pl.BlockSpec((TM_PROJ, TN_PROJ), lambda i, n: (i, n)),
        out_shape=jax.ShapeDtypeStruct((N_TOK, PROJ_W), bf16),
        scratch_shapes=[pltpu.VMEM((TM_PROJ, D_MODEL), bf16)],
        compiler_params=_cparams(("arbitrary", "arbitrary")),
        name="in_proj",
    )(x2d, g, w_pad, cos_t, sin_t)


def _mla_prep_kernel(cq_ref, ckv_ref, kr_ref, gq_ref, gkv_ref, wuq_ref, wkv_ref, cos_ref, sin_ref,
                     q_ref, k_ref, v_ref):
    cqn = _rms(cq_ref[...].astype(f32), gq_ref[...]).astype(bf16)
    q = jnp.dot(cqn, wuq_ref[...], preferred_element_type=f32)
    ckvn = _rms(ckv_ref[...].astype(f32), gkv_ref[...]).astype(bf16)
    kv = jnp.dot(ckvn, wkv_ref[...], preferred_element_type=f32)
    cos = cos_ref[...]
    sin = sin_ref[...]
    lane = lax.broadcasted_iota(i32, (TM_PREP, LANES), 1)
    first = lane < (KR_LANE + MLA_ROPE // 2)
    rope = lambda b: _rope_lanes(b, cos, sin, first, MLA_ROPE // 2)
    k_rot = rope(kr_ref[...].astype(f32))
    for h in range(HEADS):
        sl = slice(h * HEAD_W, (h + 1) * HEAD_W)
        q_ref[:, sl] = (rope(q[:, sl]) * Q_SCALE_B).astype(bf16)
        k_ref[:, sl] = (kv[:, sl] + k_rot).astype(bf16)
    v_ref[...] = kv[:, HEADS * HEAD_W:].astype(bf16)


def _mla_prep(proj, gq, gkv, wuq_p, wkv_p, cos_t, sin_t):
    tm = TM_PREP
    w = HEADS * HEAD_W
    return pl.pallas_call(
        _mla_prep_kernel,
        grid=(N_TOK // tm,),
        in_specs=[pl.BlockSpec((tm, MLA_Q_RANK), lambda i: (i, COL_CQ // MLA_Q_RANK)),
                  pl.BlockSpec((tm, MLA_KV_RANK), lambda i: (i, COL_CKV // MLA_KV_RANK)),
                  pl.BlockSpec((tm, LANES), lambda i: (i, COL_KR // LANES)),
                  pl.BlockSpec((1, MLA_Q_RANK), lambda i: (0, 0)),
                  pl.BlockSpec((1, MLA_KV_RANK), lambda i: (0, 0)),
                  pl.BlockSpec((MLA_Q_RANK, w), lambda i: (0, 0)),
                  pl.BlockSpec((MLA_KV_RANK, 2 * w), lambda i: (0, 0)),
                  pl.BlockSpec((tm, LANES), lambda i: (i, 1)),
                  pl.BlockSpec((tm, LANES), lambda i: (i, 1))],
        out_specs=[pl.BlockSpec((tm, w), lambda i: (i, 0))] * 3,
        out_shape=[jax.ShapeDtypeStruct((N_TOK, w), bf16)] * 3,
        compiler_params=_cparams(("arbitrary",)),
        name="mla_prep",
    )(proj, proj, proj, gq, gkv, wuq_p, wkv_p, cos_t, sin_t)


def _attn_kernel(*refs, ncomp, heads, qtiles):
    if ncomp == 2:
        q_ref, k_ref, v_ref, lam_ref, g_ref, o_ref, acc_scr, vt_scr = refs
    else:
        q_ref, k_ref, v_ref, o_ref, acc_scr, vt_scr = refs
    qi = pl.program_id(2)

    @pl.when(qi == 0)
    def _():
        for j in range(SEQ // TQ):
            for h in range(heads):
                vt_scr[j, h] = v_ref[j * TQ:(j + 1) * TQ, h * HEAD_W:(h + 1) * HEAD_W].T

    lane = lax.broadcasted_iota(i32, (TQ, HEAD_W), 1)
    qs = []
    for t in range(qtiles):
        for h in range(heads):
            q = q_ref[t * TQ:(t + 1) * TQ, h * HEAD_W:(h + 1) * HEAD_W]
            if ncomp == 2:
                zero = jnp.zeros_like(q)
                in_a = (lane % DA_HALF) < (DA_HALF // 2)
                qs += [jnp.where(in_a, q, zero), jnp.where(in_a, zero, q)]
            else:
                qs.append(q)
    per_tile = heads * ncomp
    head_of = lambda n: (n % per_tile) // ncomp

    def step(j, carry, modes):
        off = pl.multiple_of(j * TQ, TQ)
        live = [n for n in range(len(qs)) if modes[n // per_tile] != "skip"]
        ss = {}
        for n in live:
            h = head_of(n)
            k = k_ref[pl.ds(off, TQ), h * HEAD_W:(h + 1) * HEAD_W]
            s = lax.dot_general(k, qs[n], (((1,), (1,)), ((), ())), preferred_element_type=f32)
            if modes[n // per_tile] == "diag":
                key = lax.broadcasted_iota(i32, (TQ, TQ), 0)
                qry = lax.broadcasted_iota(i32, (TQ, TQ), 1)
                s = jnp.where(key <= qry, s, NEG_INF)
            ss[n] = s
        stats = {}
        for n in live:
            m, l = carry[n]
            s = ss[n]
            m_new = jnp.maximum(m, jnp.max(s, axis=0, keepdims=True))
            alpha = jnp.exp2(m - m_new)
            p = jnp.exp2(s - m_new)
            l_new = alpha * l + jnp.sum(p, axis=0, keepdims=True)
            stats[n] = (m_new, l_new, alpha, p.astype(bf16))
        out = list(carry)
        for n in live:
            m_new, l_new, alpha, p = stats[n]
            pv = jnp.dot(vt_scr[j, head_of(n)], p, preferred_element_type=f32)
            acc_scr[n] = alpha * acc_scr[n] + pv
            out[n] = (m_new, l_new)
        return tuple(out)

    acc_scr[...] = jnp.zeros_like(acc_scr)
    init = tuple((jnp.full((1, TQ), NEG_INF, f32), jnp.zeros((1, TQ), f32)) for _ in qs)
    first = qi * qtiles
    carry = lax.fori_loop(0, first, lambda j, c: step(j, c, ("full",) * qtiles), init)
    for u in range(qtiles):
        modes = tuple("skip" if t < u else ("diag" if t == u else "full") for t in range(qtiles))
        carry = step(first + u, carry, modes)
    carry = tuple((m, l, acc_scr[n]) for n, (m, l) in enumerate(carry))

    if ncomp == 2:
        ll = lam_ref[...]
        s1 = jnp.sum(ll[0:1, :] * ll[1:2, :], axis=-1, keepdims=True)
        s2 = jnp.sum(ll[2:3, :] * ll[3:4, :], axis=-1, keepdims=True)
        lam = jnp.exp(s1) - jnp.exp(s2) + LAM_INIT
    for t in range(qtiles):
        for h in range(heads):
            n0 = (t * heads + h) * ncomp
            if ncomp == 2:
                (_, l0, a0), (_, l1, a1) = carry[n0], carry[n0 + 1]
                o = a0 / l0 - lam * (a1 / l1)
                ms = jnp.mean(o * o, axis=0, keepdims=True)
                o = o * lax.rsqrt(ms + NORM_EPS) * g_ref[...] * (1.0 - LAM_INIT)
            else:
                _, l0, a0 = carry[n0]
                o = a0 / l0
            o_ref[t * TQ:(t + 1) * TQ, h * HEAD_W:(h + 1) * HEAD_W] = o.T.astype(bf16)


def _attention(q_arr, k_arr, v_arr, q_blk0, k_blk0, v_blk0, heads, qtiles, extra=()):
    ncomp = 2 if extra else 1
    tq = qtiles * TQ
    nq = SEQ // tq
    w = heads * HEAD_W
    qb, kb, vb = q_blk0 // heads, k_blk0 // heads, v_blk0 // heads
    in_specs = [pl.BlockSpec((tq, w), lambda b, h, i: (b * nq + i, qb + h)),
                pl.BlockSpec((SEQ, w), lambda b, h, i: (b, kb + h)),
                pl.BlockSpec((SEQ, w), lambda b, h, i: (b, vb + h))]
    for a in extra:
        in_specs.append(pl.BlockSpec(a.shape, lambda b, h, i: (0, 0)))
    return pl.pallas_call(
        functools.partial(_attn_kernel, ncomp=ncomp, heads=heads, qtiles=qtiles),
        grid=(BATCH, HEADS // heads, nq),
        in_specs=in_specs,
        out_specs=pl.BlockSpec((tq, w), lambda b, h, i: (b * nq + i, h)),
        out_shape=jax.ShapeDtypeStruct((N_TOK, HEADS * HEAD_W), bf16),
        scratch_shapes=[pltpu.VMEM((qtiles * heads * ncomp, HEAD_W, TQ), f32),
                        pltpu.VMEM((SEQ // TQ, heads, HEAD_W, TQ), bf16)],
        compiler_params=_cparams(("arbitrary", "arbitrary", "arbitrary")),
        name="diff_attn" if extra else "mla_attn",
    )(q_arr, k_arr, v_arr, *extra)


RT_GROUP_ROW = 0
RT_EXPERT_ROW = 8


def _route(hf, w_ref, b_ref, eid_ref, cw_ref):
    hf_hi = hf.astype(bf16)
    hf_lo = (hf - hf_hi.astype(f32)).astype(bf16)
    a = jnp.dot(hf_hi, w_ref[...], preferred_element_type=f32)
    b = jnp.dot(hf_lo, w_ref[:, :LANES], preferred_element_type=f32)
    logits = a[:, :LANES] + (a[:, LANES:] + b) + b_ref[...]
    lt = logits.T
    tm = lt.shape[1]
    ridx = lax.broadcasted_iota(i32, (EXPERTS_PER_GROUP, tm), 0).astype(f32)

    def top1(v):
        vmax = jnp.max(v, axis=0, keepdims=True)
        idx = jnp.min(jnp.where(v == vmax, ridx, float(EXPERTS_PER_GROUP)), axis=0, keepdims=True)
        return vmax, idx

    gl = lt[RT_GROUP_ROW:RT_GROUP_ROW + 8, :]
    gmax, gsel = top1(gl)
    p_g = 1.0 / jnp.sum(jnp.exp(gl - gmax), axis=0, keepdims=True)
    el = jnp.zeros((EXPERTS_PER_GROUP, tm), f32)
    for g in range(N_GROUPS):
        r0 = RT_EXPERT_ROW + g * EXPERTS_PER_GROUP
        el = jnp.where(gsel == float(g), lt[r0:r0 + EXPERTS_PER_GROUP, :], el)
    e1, i1 = top1(el)
    e2, i2 = top1(jnp.where(ridx == i1, -jnp.inf, el))
    r = jnp.exp(e2 - e1)
    w1 = 1.0 / (1.0 + r)
    base = gsel * EXPERTS_PER_GROUP
    eid_ref[0:1, :] = (base + i1).astype(i32)
    eid_ref[1:2, :] = (base + i2).astype(i32)
    cw_ref[0:1, :] = p_g * w1
    cw_ref[1:2, :] = p_g * (r * w1)


def _merge_kernel(oa_ref, ob_ref, ga_ref, gb_ref, x_ref, wa_ref, wb_ref, wo_ref, gf_ref, wr_ref, br_ref,
                  x1_ref, hf_ref, eid_ref, cw_ref, slab_scr):
    ya = jnp.dot(oa_ref[...], wa_ref[...], preferred_element_type=f32)
    yb = jnp.dot(ob_ref[...], wb_ref[...], preferred_element_type=f32)
    mixed = (jax.nn.sigmoid(ga_ref[...].astype(f32)) * ya
             + jax.nn.sigmoid(gb_ref[...].astype(f32)) * yb)
    x1 = x_ref[...] + jnp.dot(mixed.astype(bf16), wo_ref[...], preferred_element_type=f32)
    x1_ref[...] = x1
    hf = _rms(x1, gf_ref[...])
    _slab_store(slab_scr, (), hf)
    hf_ref[...] = slab_scr[...].astype(bf16)
    _route(hf, wr_ref, br_ref, eid_ref, cw_ref)


def _merge(oa, ob, proj, x2d, wa, wb, wo, gf, w_rt, b_rt):
    tm = TM_MERGE
    w = HEADS * HEAD_W
    const = lambda shape: pl.BlockSpec(shape, lambda i: (0, 0), pipeline_mode=pl.Buffered(1))
    return pl.pallas_call(
        _merge_kernel,
        grid=(N_TOK // tm,),
        in_specs=[pl.BlockSpec((tm, w), lambda i: (i, 0)),
                  pl.BlockSpec((tm, w), lambda i: (i, 0)),
                  pl.BlockSpec((tm, D_MODEL), lambda i: (i, COL_GA // D_MODEL)),
                  pl.BlockSpec((tm, D_MODEL), lambda i: (i, COL_GB // D_MODEL)),
                  pl.BlockSpec((tm, D_MODEL), lambda i: (i, 0)),
                  const((w, D_MODEL)), const((w, D_MODEL)), const((D_MODEL, D_MODEL)),
                  const((1, D_MODEL)), const((D_MODEL, 2 * LANES)), const((1, LANES))],
        out_specs=[pl.BlockSpec((tm, D_MODEL), lambda i: (i, 0)),
                   pl.BlockSpec((tm * SLAB, LANES), lambda i: (i, 0)),
                   pl.BlockSpec((TOP_K, tm), lambda i: (0, i)),
                   pl.BlockSpec((TOP_K, tm), lambda i: (0, i))],
        out_shape=[jax.ShapeDtypeStruct((N_TOK, D_MODEL), f32),
                   jax.ShapeDtypeStruct((N_TOK * SLAB, LANES), bf16),
                   jax.ShapeDtypeStruct((TOP_K, N_TOK), i32),
                   jax.ShapeDtypeStruct((TOP_K, N_TOK), f32)],
        scratch_shapes=[pltpu.VMEM((tm * SLAB, LANES), f32)],
        compiler_params=_cparams(("arbitrary",)),
        name="merge_route",
    )(oa, ob, proj, proj, x2d, wa, wb, wo, gf, w_rt, b_rt)


PLAN_N = 128


def _plan_kernel(eid_ref, dest_ref, ends_ref):
    eid = eid_ref[...]
    r = lax.broadcasted_iota(i32, (PLAN_N, PLAN_N), 0)
    c = lax.broadcasted_iota(i32, (PLAN_N, PLAN_N), 1)
    upper = (r < c).astype(bf16)
    lower = (c < r).astype(bf16)
    ones = jnp.ones((PLAN_N, PLAN_N), bf16)
    lane = lax.broadcasted_iota(i32, (1, LANES), 1)
    dest = jnp.zeros((PLAN_N, PLAN_N), f32)
    ends = jnp.zeros((1, LANES), f32)
    start = jnp.zeros((1, 1), f32)
    for e in range(N_EXPERTS):
        m = (eid == e).astype(f32)
        mb = m.astype(bf16)
        within = jnp.dot(mb, upper, preferred_element_type=f32)
        rowtot = jnp.dot(mb, ones, preferred_element_type=f32)
        rowpre = jnp.dot(lower, rowtot.astype(bf16), preferred_element_type=f32)
        cnt = jnp.sum(jnp.sum(m, axis=1, keepdims=True), axis=0, keepdims=True)
        dest = dest + m * (start + within + rowpre)
        start = start + jnp.floor((cnt + (ROW_BLK - 1)) * (1.0 / ROW_BLK)) * ROW_BLK
        ends = jnp.where(lane == e, start, ends)
    dest_ref[...] = dest.astype(i32)
    ends_ref[...] = ends.astype(i32)


def _plan(eid):
    whole = lambda shape: pl.BlockSpec(shape, lambda: (0, 0))
    return pl.pallas_call(
        _plan_kernel,
        in_specs=[whole((PLAN_N, PLAN_N))],
        out_specs=[whole((PLAN_N, PLAN_N)), whole((1, LANES))],
        out_shape=[jax.ShapeDtypeStruct((PLAN_N, PLAN_N), i32),
                   jax.ShapeDtypeStruct((1, LANES), i32)],
        name="moe_plan",
    )(eid.reshape(PLAN_N, PLAN_N))


def _dispatch_kernel(ends_ref, dest_ref, hf_ref, xs_hbm, zbuf, sem):
    i = pl.program_id(0)

    def tail_copy(e):
        pend = ends_ref[e]
        prev = jnp.where(e > 0, ends_ref[jnp.maximum(e - 1, 0)], 0)
        row0 = pl.multiple_of(jnp.maximum(pend - ROW_BLK, 0), ROW_BLK)
        return pend > prev, pltpu.make_async_copy(zbuf, xs_hbm.at[_slab_rows(row0, ROW_BLK), :], sem)

    @pl.when(i == 0)
    def _():
        zbuf[...] = jnp.zeros_like(zbuf)

        def zstart(e, _):
            used, cp = tail_copy(e)

            @pl.when(used)
            def _():
                cp.start()
            return 0

        def zwait(e, _):
            used, cp = tail_copy(e)

            @pl.when(used)
            def _():
                cp.wait()
            return 0

        def unused_copy(b):
            row0 = pl.multiple_of(b * ROW_BLK, ROW_BLK)
            return pltpu.make_async_copy(zbuf, xs_hbm.at[_slab_rows(row0, ROW_BLK), :], sem)

        def ustart(b, _):
            unused_copy(b).start()
            return 0

        def uwait(b, _):
            unused_copy(b).wait()
            return 0

        n_used = ends_ref[N_EXPERTS - 1] // ROW_BLK
        lax.fori_loop(0, N_EXPERTS, zstart, 0)
        lax.fori_loop(n_used, N_BLKS, ustart, 0)
        lax.fori_loop(0, N_EXPERTS, zwait, 0)
        lax.fori_loop(n_used, N_BLKS, uwait, 0)

    base = i * TT_DISP

    def row_copy(t, k):
        d = dest_ref[k, base + t]
        return pltpu.make_async_copy(hf_ref.at[_slab_rows(t, 1), :], xs_hbm.at[_slab_rows(d, 1), :], sem)

    def issue(t, _):
        for k in range(TOP_K):
            row_copy(t, k).start(priority=k)
        return 0

    def drain(t, _):
        for k in range(TOP_K):
            row_copy(t, k).wait()
        return 0

    lax.fori_loop(0, TT_DISP, issue, 0, unroll=ISSUE_UNROLL)
    lax.fori_loop(0, TT_DISP, drain, 0, unroll=ISSUE_UNROLL)


def _dispatch(ends, dest, hf):
    return pl.pallas_call(
        _dispatch_kernel,
        grid_spec=pltpu.PrefetchScalarGridSpec(
            num_scalar_prefetch=2,
            grid=(N_TOK // TT_DISP,),
            in_specs=[pl.BlockSpec((TT_DISP * SLAB, LANES), lambda i, e, d: (i, 0))],
            out_specs=pl.BlockSpec(memory_space=pl.ANY),
            scratch_shapes=[pltpu.VMEM((ROW_BLK * SLAB, LANES), bf16),
                            pltpu.SemaphoreType.DMA(())]),
        out_shape=jax.ShapeDtypeStruct((N_ROWS * SLAB, LANES), bf16),
        compiler_params=_cparams(("arbitrary",)),
        name="moe_dispatch",
    )(ends, dest, hf)


W_RING = 3
X_RING = 6
Y_RING = 3


def _ffn_kernel(ends_ref, wg_hbm, wu_hbm, wd_hbm, xs_hbm, y_hbm,
                wg_f, wu_f, wd_f, wg_s, wu_s, wd_s, xbuf, ybuf, slab_f, sem_w, sem_x, sem_y):
    n_used = ends_ref[N_EXPERTS - 1] // ROW_BLK

    def w_copies(ex):
        slot = ex % W_RING
        return [pltpu.make_async_copy(hbm.at[ex], buf.at[slot], sem_w.at[i, slot])
                for i, (hbm, buf) in enumerate(((wg_hbm, wg_f), (wu_hbm, wu_f), (wd_hbm, wd_f)))]

    def advance(e_from, e_to):
        def one(ex, _):
            for cp in w_copies(ex):
                cp.wait()

            @pl.when(ex + (W_RING - 1) < N_EXPERTS)
            def _():
                for cp in w_copies(ex + (W_RING - 1)):
                    cp.start()
            return 0

        lax.fori_loop(e_from + 1, e_to + 1, one, 0)

    def x_copy(b):
        slot = b % X_RING
        return pltpu.make_async_copy(xs_hbm.at[_slab_rows(b * ROW_BLK, ROW_BLK), :], xbuf.at[slot],
                                     sem_x.at[slot])

    def y_copy(b):
        slot = b % Y_RING
        return pltpu.make_async_copy(ybuf.at[slot], y_hbm.at[_slab_rows(b * ROW_BLK, ROW_BLK), :],
                                     sem_y.at[slot])

    for ex in range(W_RING - 1):
        for cp in w_copies(ex):
            cp.start()
    for b0 in range(X_RING - 1):
        @pl.when(b0 < n_used)
        def _():
            x_copy(b0).start()

    def body(b, e):
        @pl.when(b + (X_RING - 1) < n_used)
        def _():
            x_copy(b + (X_RING - 1)).start()

        e_new = lax.while_loop(lambda ex: b * ROW_BLK >= ends_ref[ex], lambda ex: ex + 1,
                               jnp.maximum(e, 0))

        @pl.when(e_new != e)
        def _():
            advance(e, e_new)
            wslot = e_new % W_RING
            wg_s[...] = wg_f[wslot].astype(bf16)
            wu_s[...] = wu_f[wslot].astype(bf16)
            wd_s[...] = wd_f[wslot].astype(bf16)

        x_copy(b).wait()

        @pl.when(b >= Y_RING)
        def _():
            y_copy(b - Y_RING).wait()

        slab_f[...] = xbuf[b % X_RING].astype(f32)
        x = _slab_load(slab_f, (), ROW_BLK).astype(bf16)
        h1 = jnp.dot(x, wg_s[...], preferred_element_type=f32)
        h2 = jnp.dot(x, wu_s[...], preferred_element_type=f32)
        a = (h1 * jax.nn.sigmoid(h1) * h2).astype(bf16)
        _slab_store(ybuf, (b % Y_RING,), jnp.dot(a, wd_s[...], preferred_element_type=f32))
        y_copy(b).start()
        return e_new

    e_last = lax.fori_loop(0, n_used, body, jnp.int32(-1))
    advance(e_last, jnp.int32(N_EXPERTS - 1))

    for back in range(Y_RING, 0, -1):
        @pl.when(n_used >= back)
        def _():
            y_copy(n_used - back).wait()

    ybuf[0] = jnp.zeros((ROW_BLK * SLAB, LANES), f32)

    def unused_copy(b):
        return pltpu.make_async_copy(ybuf.at[0], y_hbm.at[_slab_rows(b * ROW_BLK, ROW_BLK), :],
                                     sem_y.at[0])

    def ustart(b, _):
        unused_copy(b).start()
        return 0

    def uwait(b, _):
        unused_copy(b).wait()
        return 0

    lax.fori_loop(n_used, N_BLKS, ustart, 0)
    lax.fori_loop(n_used, N_BLKS, uwait, 0)


def _expert_ffn(ends, xs, wg, wu, wd):
    return pl.pallas_call(
        _ffn_kernel,
        grid_spec=pltpu.PrefetchScalarGridSpec(
            num_scalar_prefetch=1,
            grid=(1,),
            in_specs=[pl.BlockSpec(memory_space=pl.ANY)] * 4,
            out_specs=pl.BlockSpec(memory_space=pl.ANY),
            scratch_shapes=[pltpu.VMEM((W_RING, D_MODEL, EXPERT_FF), f32),
                            pltpu.VMEM((W_RING, D_MODEL, EXPERT_FF), f32),
                            pltpu.VMEM((W_RING, EXPERT_FF, D_MODEL), f32),
                            pltpu.VMEM((D_MODEL, EXPERT_FF), bf16),
                            pltpu.VMEM((D_MODEL, EXPERT_FF), bf16),
                            pltpu.VMEM((EXPERT_FF, D_MODEL), bf16),
                            pltpu.VMEM((X_RING, ROW_BLK * SLAB, LANES), bf16),
                            pltpu.VMEM((Y_RING, ROW_BLK * SLAB, LANES), f32),
                            pltpu.VMEM((ROW_BLK * SLAB, LANES), f32),
                            pltpu.SemaphoreType.DMA((3, W_RING)),
                            pltpu.SemaphoreType.DMA((X_RING,)),
                            pltpu.SemaphoreType.DMA((Y_RING,))]),
        out_shape=jax.ShapeDtypeStruct((N_ROWS * SLAB, LANES), f32),
        compiler_params=_cparams(("arbitrary",)),
        name="expert_ffn",
    )(ends, wg, wu, wd, xs)


def _combine_kernel(dest_ref, x_ref, cw_ref, g_ref, y_hbm, o_ref, ybuf, sem):
    i = pl.program_id(0)
    n_steps = pl.num_programs(0)

    def row_copy(step, t, k):
        slot = step % 2
        d = dest_ref[k, step * TM_COMB + t]
        return pltpu.make_async_copy(y_hbm.at[_slab_rows(d, 1), :],
                                     ybuf.at[slot, k, _slab_rows(t, 1), :], sem.at[slot])

    def issue(step):
        def one(t, _):
            for k in range(TOP_K):
                row_copy(step, t, k).start()
            return 0
        lax.fori_loop(0, TM_COMB, one, 0, unroll=ISSUE_UNROLL)

    def drain(step):
        def one(t, _):
            for k in range(TOP_K):
                row_copy(step, t, k).wait()
            return 0
        lax.fori_loop(0, TM_COMB, one, 0, unroll=ISSUE_UNROLL)

    @pl.when(i == 0)
    def _():
        issue(i)

    @pl.when(i + 1 < n_steps)
    def _():
        issue(i + 1)

    drain(i)
    slot = i % 2
    w = cw_ref[...]
    x = (x_ref[...] + w[:, 0:1] * _slab_load(ybuf, (slot, 0), TM_COMB)
         + w[:, 1:2] * _slab_load(ybuf, (slot, 1), TM_COMB))
    o_ref[...] = _rms(x, g_ref[...])


def _combine(dest, x1, cw_t, g, y):
    tm = TM_COMB
    return pl.pallas_call(
        _combine_kernel,
        grid_spec=pltpu.PrefetchScalarGridSpec(
            num_scalar_prefetch=1,
            grid=(N_TOK // tm,),
            in_specs=[pl.BlockSpec((tm, D_MODEL), lambda i, d: (i, 0)),
                      pl.BlockSpec((tm, TOP_K), lambda i, d: (i, 0)),
                      pl.BlockSpec((1, D_MODEL), lambda i, d: (0, 0)),
                      pl.BlockSpec(memory_space=pl.ANY)],
            out_specs=pl.BlockSpec((tm, D_MODEL), lambda i, d: (i, 0)),
            scratch_shapes=[pltpu.VMEM((2, TOP_K, tm * SLAB, LANES), f32),
                            pltpu.SemaphoreType.DMA((2,))]),
        out_shape=jax.ShapeDtypeStruct((N_TOK, D_MODEL), f32),
        compiler_params=_cparams(("arbitrary",)),
        name="moe_combine",
    )(dest, x1, cw_t, g, y)


TR_RELAYOUT = 256
KR_TILE = COL_KR // TR_RELAYOUT


def _w_in_relayout_kernel(w_ref, o_ref):
    i = pl.program_id(0)

    @pl.when(i < COL_V // TR_RELAYOUT)
    def _():
        q = ROPE_NA
        for g in range(TR_RELAYOUT // LANES):
            r = g * LANES
            o_ref[r:r + q, :] = w_ref[r:r + q, :].astype(bf16)
            o_ref[r + q:r + 2 * q, :] = w_ref[r + 2 * q:r + 3 * q, :].astype(bf16)
            o_ref[r + 2 * q:r + 3 * q, :] = w_ref[r + q:r + 2 * q, :].astype(bf16)
            o_ref[r + 3 * q:r + 4 * q, :] = w_ref[r + 3 * q:r + 4 * q, :].astype(bf16)

    @pl.when((i >= COL_V // TR_RELAYOUT) & (i != KR_TILE))
    def _():
        o_ref[...] = w_ref[...].astype(bf16)

    @pl.when(i == KR_TILE)
    def _():
        o_ref[...] = jnp.zeros_like(o_ref)
        o_ref[KR_LANE:KR_LANE + MLA_ROPE, :] = w_ref[0:MLA_ROPE, :].astype(bf16)


def _prep_w_in(w_in_t):
    n_src, d = w_in_t.shape
    gate_src = COL_KR + MLA_ROPE

    gate_tile = COL_GA // TR_RELAYOUT
    sub = 8
    shift = (COL_GA - gate_src) // sub

    def src_row(i):
        return ((TR_RELAYOUT // sub) * i - shift * (i // gate_tile)) * sub

    return pl.pallas_call(
        _w_in_relayout_kernel,
        grid=(PROJ_W // TR_RELAYOUT,),
        in_specs=[pl.BlockSpec((pl.Element(TR_RELAYOUT), pl.Element(d)), lambda i: (src_row(i), 0))],
        out_specs=pl.BlockSpec((TR_RELAYOUT, d), lambda i: (i, 0)),
        out_shape=jax.ShapeDtypeStruct((PROJ_W, d), bf16),
        compiler_params=_cparams(("arbitrary",)),
        name="w_in_relayout",
    )(w_in_t)


def _prep_w_uq(w_uq):
    w = w_uq.reshape(MLA_Q_RANK, HEADS, MLA_NOPE + MLA_ROPE)
    w = jnp.pad(w, ((0, 0), (0, 0), (0, HEAD_W - MLA_NOPE - MLA_ROPE)))
    return w.reshape(MLA_Q_RANK, HEADS * HEAD_W).astype(bf16)


def _prep_w_ukv(w_ukv):
    w = w_ukv.reshape(MLA_KV_RANK, HEADS, MLA_NOPE + HEAD_W)
    wk = jnp.pad(w[:, :, :MLA_NOPE], ((0, 0), (0, 0), (0, HEAD_W - MLA_NOPE)))
    wv = w[:, :, MLA_NOPE:]
    return jnp.concatenate([wk.reshape(MLA_KV_RANK, -1), wv.reshape(MLA_KV_RANK, -1)], axis=1).astype(bf16)


def _prep_router(w_group, b_group, w_router, b_router):
    z = lambda n: jnp.zeros((D_MODEL, n), f32)
    w = jnp.concatenate([w_group, z(RT_EXPERT_ROW - N_GROUPS), w_router,
                         z(LANES - RT_EXPERT_ROW - N_EXPERTS)], axis=1)
    b = jnp.concatenate([b_group, jnp.full((RT_EXPERT_ROW - N_GROUPS,), NEG_INF, f32), b_router,
                         jnp.zeros((LANES - RT_EXPERT_ROW - N_EXPERTS,), f32)])[None, :]
    w_hi = w.astype(bf16)
    w_lo = (w - w_hi.astype(f32)).astype(bf16)
    return jnp.concatenate([w_hi, w_lo], axis=1), b


def kernel(x, positions, attn_norm, w_in, da_lambda, da_subln, mla_q_norm, mla_w_uq, mla_kv_norm, mla_w_ukv, w_branch_a, w_branch_b, w_out, ffn_norm, w_group, b_group, w_router, b_router, w_exp_gate, w_exp_up, w_exp_down, final_norm):
    x2d = x.reshape(N_TOK, D_MODEL)
    cos_t, sin_t = _rope_tables(positions)

    proj = _in_proj(x2d, attn_norm[0][None, :], _prep_w_in(w_in[0].T), cos_t, sin_t)
    qb, kb, vb = _mla_prep(proj, mla_q_norm[0][None, :], mla_kv_norm[0][None, :],
                           _prep_w_uq(mla_w_uq[0]), _prep_w_ukv(mla_w_ukv[0]), cos_t, sin_t)
    oa = _attention(proj, proj, proj, COL_Q // HEAD_W, COL_K // HEAD_W, COL_V // HEAD_W,
                    heads=8, qtiles=2, extra=(da_lambda[0], da_subln[0][:, None]))
    ob = _attention(qb, kb, vb, 0, 0, 0, heads=8, qtiles=2)
    w_rt, b_rt = _prep_router(w_group[0], b_group[0], w_router[0], b_router[0])
    x1, hf, eid, cw = _merge(oa, ob, proj, x2d, w_branch_a[0].astype(bf16), w_branch_b[0].astype(bf16),
                             w_out[0].astype(bf16), ffn_norm[0][None, :], w_rt, b_rt)

    dest, ends = _plan(eid)
    dest = dest.reshape(TOP_K, N_TOK)
    ends = ends[0, :N_EXPERTS]
    xs = _dispatch(ends, dest, hf)
    y = _expert_ffn(ends, xs, w_exp_gate[0], w_exp_up[0], w_exp_down[0])
    out = _combine(dest, x1, cw.T, final_norm[None, :], y)
    return out.reshape(BATCH, SEQ, D_MODEL)
```

```python
import functools
import math

import jax
import jax.numpy as jnp
from jax import lax
from jax.experimental import pallas as pl
from jax.experimental.pallas import tpu as pltpu

f32 = jnp.float32
bf16 = jnp.bfloat16
i32 = jnp.int32

D_MODEL = 2048
BATCH = 4
SEQ = 2048
N_TOK = BATCH * SEQ
HEADS = 8
HEAD_W = 128
DA_HALF = 64
MLA_Q_RANK = 512
MLA_KV_RANK = 256
MLA_NOPE = 64
MLA_ROPE = 32
N_GROUPS = 4
EXPERTS_PER_GROUP = 8
N_EXPERTS = N_GROUPS * EXPERTS_PER_GROUP
TOP_K = 2
EXPERT_FF = 512
ROPE_THETA = 10000.0
NORM_EPS = 1e-6
NEG_INF = -1e30
LAM_INIT = 0.8 - 0.6 * math.exp(-0.3 * 0)
LOG2E = math.log2(math.e)

LANES = 128
VMEM_LIMIT = 56 * 1024 * 1024

COL_Q, COL_K, COL_V = 0, 1024, 2048
COL_CQ, COL_CKV = 3072, 3584
COL_KR = 3840
COL_GA, COL_GB = 4096, 6144
PROJ_W = 8192
KR_LANE = MLA_NOPE

TM_PROJ, TN_PROJ = 1024, 1024
TM_PREP = 512
TQ = 256
TM_MERGE = 256
ROW_BLK = 128
N_ROWS = (N_TOK * TOP_K // ROW_BLK + N_EXPERTS) * ROW_BLK
N_BLKS = N_ROWS // ROW_BLK
TT_DISP = 2048
TM_COMB = 64

Q_SCALE_A = (DA_HALF ** -0.5) * LOG2E
Q_SCALE_B = ((MLA_NOPE + MLA_ROPE) ** -0.5) * LOG2E


def _cparams(sem, **kw):
    return pltpu.CompilerParams(dimension_semantics=sem, vmem_limit_bytes=VMEM_LIMIT, **kw)


def _rms(x, g):
    return x * lax.rsqrt(jnp.mean(x * x, axis=-1, keepdims=True) + NORM_EPS) * g


SLAB = D_MODEL // LANES


ISSUE_UNROLL = 16


def _slab_rows(row0, nrows):
    return pl.ds(pl.multiple_of(row0 * SLAB, SLAB), nrows * SLAB)


def _slab_store(ref, idx, val):
    rows = val.shape[0]
    for j in range(SLAB):
        ref[idx + (pl.ds(j, rows, stride=SLAB), slice(None))] = val[:, j * LANES:(j + 1) * LANES]


def _slab_chunk(ref, idx, rows, j):
    return ref[idx + (pl.ds(j, rows, stride=SLAB), slice(None))]


def _slab_load(ref, idx, rows):
    return jnp.concatenate([_slab_chunk(ref, idx, rows, j) for j in range(SLAB)], axis=1)


def _rope_lanes(blk, cos, sin, first_half, half):
    up = pltpu.roll(blk, LANES - half, 1)
    down = pltpu.roll(blk, half, 1)
    return blk * cos + jnp.where(first_half, up, down) * sin


ROPE_NA = DA_HALF // 2
ROPE_NB = MLA_ROPE // 2


def _rope_tables_kernel(pos_ref, inv_ref, cos_ref, sin_ref):
    half = pos_ref.shape[0] // 2
    lane = lax.broadcasted_iota(i32, (half, LANES), 1)
    pos = jnp.where(lane < LANES // 2, pos_ref[:half, :], pos_ref[half:, :]).astype(f32)
    ang = pos * inv_ref[...]
    c = jnp.cos(ang)
    s = jnp.sin(ang)
    one = lambda n: jnp.ones((half, n), f32)
    zero = lambda n: jnp.zeros((half, n), f32)
    tail = LANES - KR_LANE - MLA_ROPE
    for part in range(2):
        o = part * (LANES // 2)
        rows = slice(part * half, (part + 1) * half)
        ca, sa = c[:, o:o + ROPE_NA], s[:, o:o + ROPE_NA]
        cb, sb = c[:, o + ROPE_NA:o + ROPE_NA + ROPE_NB], s[:, o + ROPE_NA:o + ROPE_NA + ROPE_NB]
        cos_ref[rows, :] = jnp.concatenate([ca, ca, ca, ca, one(KR_LANE), cb, cb, one(tail)], axis=1)
        sin_ref[rows, :] = jnp.concatenate([-sa, -sa, sa, sa, zero(KR_LANE), -sb, sb, zero(tail)],
                                           axis=1)


def _rope_tables(positions):
    inv_a = ROPE_THETA ** (-jnp.arange(0, DA_HALF, 2, dtype=f32) / DA_HALF)
    inv_b = ROPE_THETA ** (-jnp.arange(0, MLA_ROPE, 2, dtype=f32) / MLA_ROPE)
    inv = jnp.concatenate([inv_a, inv_b, jnp.zeros((LANES // 2 - ROPE_NA - ROPE_NB,), f32)])
    inv = jnp.tile(inv, 2)[None, :]
    pos = positions.reshape(N_TOK, 1)
    tm = 1024
    return pl.pallas_call(
        _rope_tables_kernel,
        grid=(N_TOK // tm,),
        in_specs=[pl.BlockSpec((tm, 1), lambda i: (i, 0)),
                  pl.BlockSpec((1, LANES), lambda i: (0, 0))],
        out_specs=[pl.BlockSpec((tm, 2 * LANES), lambda i: (i, 0)),
                   pl.BlockSpec((tm, 2 * LANES), lambda i: (i, 0))],
        out_shape=[jax.ShapeDtypeStruct((N_TOK, 2 * LANES), f32)] * 2,
        compiler_params=_cparams(("arbitrary",)),
        name="rope_tables",
    )(pos, inv)


def _in_proj_kernel(x_ref, g_ref, w_ref, cos_ref, sin_ref, o_ref, h_scr):
    n = pl.program_id(1)

    @pl.when(n == 0)
    def _():
        h_scr[...] = _rms(x_ref[...], g_ref[...]).astype(bf16)

    acc = lax.dot_general(h_scr[...], w_ref[...], (((1,), (1,)), ((), ())),
                          preferred_element_type=f32)
    n_rope = COL_V // TN_PROJ
    n_q = COL_K // TN_PROJ

    @pl.when(n < n_rope)
    def _():
        c = jnp.where(n < n_q, Q_SCALE_A, 1.0).astype(f32)
        cos = cos_ref[...] * c
        sin = sin_ref[...] * c
        for j in range(TN_PROJ // LANES):
            blk = acc[:, j * LANES:(j + 1) * LANES]
            rot = pltpu.roll(blk, LANES // 2, 1)
            o_ref[:, j * LANES:(j + 1) * LANES] = (blk * cos + rot * sin).astype(bf16)

    @pl.when(n >= n_rope)
    def _():
        o_ref[...] = acc.astype(bf16)


def _in_proj(x2d, g, w_pad, cos_t, sin_t):
    return pl.pallas_call(
        _in_proj_kernel,
        grid=(N_TOK // TM_PROJ, PROJ_W // TN_PROJ),
        in_specs=[pl.BlockSpec((TM_PROJ, D_MODEL), lambda i, n: (i, 0)),
                  pl.BlockSpec((1, D_MODEL), lambda i, n: (0, 0)),
                  pl.BlockSpec((TN_PROJ, D_MODEL), lambda i, n: (n, 0)),
                  pl.BlockSpec((TM_PROJ, LANES), lambda i, n: (i, 0)),
                  pl.BlockSpec((TM_PROJ, LANES), lambda i, n: (i, 0))],
        out_specs=pl.BlockSpec((TM_PROJ, TN_PROJ), lambda i, n: (i, n)),
        out_shape=jax.ShapeDtypeStruct((N_TOK, PROJ_W), bf16),
        scratch_shapes=[pltpu.VMEM((TM_PROJ, D_MODEL), bf16)],
        compiler_params=_cparams(("arbitrary", "arbitrary")),
        name="in_proj",
    )(x2d, g, w_pad, cos_t, sin_t)


def _mla_prep_kernel(cq_ref, ckv_ref, kr_ref, gq_ref, gkv_ref, wuq_ref, wkv_ref, cos_ref, sin_ref,
                     q_ref, k_ref, v_ref):
    cqn = _rms(cq_ref[...].astype(f32), gq_ref[...]).astype(bf16)
    q = jnp.dot(cqn, wuq_ref[...], preferred_element_type=f32)
    ckvn = _rms(ckv_ref[...].astype(f32), gkv_ref[...]).astype(bf16)
    kv = jnp.dot(ckvn, wkv_ref[...], preferred_element_type=f32)
    cos = cos_ref[...]
    sin = sin_ref[...]
    lane = lax.broadcasted_iota(i32, (TM_PREP, LANES), 1)
    first = lane < (KR_LANE + MLA_ROPE // 2)
    rope = lambda b: _rope_lanes(b, cos, sin, first, MLA_ROPE // 2)
    k_rot = rope(kr_ref[...].astype(f32))
    for h in range(HEADS):
        sl = slice(h * HEAD_W, (h + 1) * HEAD_W)
        q_ref[:, sl] = (rope(q[:, sl]) * Q_SCALE_B).astype(bf16)
        k_ref[:, sl] = (kv[:, sl] + k_rot).astype(bf16)
    v_ref[...] = kv[:, HEADS * HEAD_W:].astype(bf16)


def _mla_prep(proj, gq, gkv, wuq_p, wkv_p, cos_t, sin_t):
    tm = TM_PREP
    w = HEADS * HEAD_W
    return pl.pallas_call(
        _mla_prep_kernel,
        grid=(N_TOK // tm,),
        in_specs=[pl.BlockSpec((tm, MLA_Q_RANK), lambda i: (i, COL_CQ // MLA_Q_RANK)),
                  pl.BlockSpec((tm, MLA_KV_RANK), lambda i: (i, COL_CKV // MLA_KV_RANK)),
                  pl.BlockSpec((tm, LANES), lambda i: (i, COL_KR // LANES)),
                  pl.BlockSpec((1, MLA_Q_RANK), lambda i: (0, 0)),
                  pl.BlockSpec((1, MLA_KV_RANK), lambda i: (0, 0)),
                  pl.BlockSpec((MLA_Q_RANK, w), lambda i: (0, 0)),
                  pl.BlockSpec((MLA_KV_RANK, 2 * w), lambda i: (0, 0)),
                  pl.BlockSpec((tm, LANES), lambda i: (i, 1)),
                  pl.BlockSpec((tm, LANES), lambda i: (i, 1))],
        out_specs=[pl.BlockSpec((tm, w), lambda i: (i, 0))] * 3,
        out_shape=[jax.ShapeDtypeStruct((N_TOK, w), bf16)] * 3,
        compiler_params=_cparams(("arbitrary",)),
        name="mla_prep",
    )(proj, proj, proj, gq, gkv, wuq_p, wkv_p, cos_t, sin_t)


def _attn_kernel(*refs, ncomp, heads, qtiles):
    if ncomp == 2:
        q_ref, k_ref, v_ref, lam_ref, g_ref, o_ref, acc_scr, vt_scr = refs
    else:
        q_ref, k_ref, v_ref, o_ref, acc_scr, vt_scr = refs
    qi = pl.program_id(2)

    @pl.when(qi == 0)
    def _():
        for j in range(SEQ // TQ):
            for h in range(heads):
                vt_scr[j, h] = v_ref[j * TQ:(j + 1) * TQ, h * HEAD_W:(h + 1) * HEAD_W].T

    lane = lax.broadcasted_iota(i32, (TQ, HEAD_W), 1)
    qs = []
    for t in range(qtiles):
        for h in range(heads):
            q = q_ref[t * TQ:(t + 1) * TQ, h * HEAD_W:(h + 1) * HEAD_W]
            if ncomp == 2:
                zero = jnp.zeros_like(q)
                in_a = (lane % DA_HALF) < (DA_HALF // 2)
                qs += [jnp.where(in_a, q, zero), jnp.where(in_a, zero, q)]
            else:
                qs.append(q)
    per_tile = heads * ncomp
    head_of = lambda n: (n % per_tile) // ncomp

    def step(j, carry, modes):
        off = pl.multiple_of(j * TQ, TQ)
        live = [n for n in range(len(qs)) if modes[n // per_tile] != "skip"]
        ss = {}
        for n in live:
            h = head_of(n)
            k = k_ref[pl.ds(off, TQ), h * HEAD_W:(h + 1) * HEAD_W]
            s = lax.dot_general(k, qs[n], (((1,), (1,)), ((), ())), preferred_element_type=f32)
            if modes[n // per_tile] == "diag":
                key = lax.broadcasted_iota(i32, (TQ, TQ), 0)
                qry = lax.broadcasted_iota(i32, (TQ, TQ), 1)
                s = jnp.where(key <= qry, s, NEG_INF)
            ss[n] = s
        stats = {}
        for n in live:
            m, l = carry[n]
            s = ss[n]
            m_new = jnp.maximum(m, jnp.max(s, axis=0, keepdims=True))
            alpha = jnp.exp2(m - m_new)
            p = jnp.exp2(s - m_new)
            l_new = alpha * l + jnp.sum(p, axis=0, keepdims=True)
            stats[n] = (m_new, l_new, alpha, p.astype(bf16))
        out = list(carry)
        for n in live:
            m_new, l_new, alpha, p = stats[n]
            pv = jnp.dot(vt_scr[j, head_of(n)], p, preferred_element_type=f32)
            acc_scr[n] = alpha * acc_scr[n] + pv
            out[n] = (m_new, l_new)
        return tuple(out)

    acc_scr[...] = jnp.zeros_like(acc_scr)
    init = tuple((jnp.full((1, TQ), NEG_INF, f32), jnp.zeros((1, TQ), f32)) for _ in qs)
    first = qi * qtiles
    carry = lax.fori_loop(0, first, lambda j, c: step(j, c, ("full",) * qtiles), init)
    for u in range(qtiles):
        modes = tuple("skip" if t < u else ("diag" if t == u else "full") for t in range(qtiles))
        carry = step(first + u, carry, modes)
    carry = tuple((m, l, acc_scr[n]) for n, (m, l) in enumerate(carry))

    if ncomp == 2:
        ll = lam_ref[...]
        s1 = jnp.sum(ll[0:1, :] * ll[1:2, :], axis=-1, keepdims=True)
        s2 = jnp.sum(ll[2:3, :] * ll[3:4, :], axis=-1, keepdims=True)
        lam = jnp.exp(s1) - jnp.exp(s2) + LAM_INIT
    for t in range(qtiles):
        for h in range(heads):
            n0 = (t * heads + h) * ncomp
            if ncomp == 2:
                (_, l0, a0), (_, l1, a1) = carry[n0], carry[n0 + 1]
                o = a0 / l0 - lam * (a1 / l1)
                ms = jnp.mean(o * o, axis=0, keepdims=True)
                o = o * lax.rsqrt(ms + NORM_EPS) * g_ref[...] * (1.0 - LAM_INIT)
            else:
                _, l0, a0 = carry[n0]
                o = a0 / l0
            o_ref[t * TQ:(t + 1) * TQ, h * HEAD_W:(h + 1) * HEAD_W] = o.T.astype(bf16)


def _attention(q_arr, k_arr, v_arr, q_blk0, k_blk0, v_blk0, heads, qtiles, extra=()):
    ncomp = 2 if extra else 1
    tq = qtiles * TQ
    nq = SEQ // tq
    w = heads * HEAD_W
    qb, kb, vb = q_blk0 // heads, k_blk0 // heads, v_blk0 // heads
    in_specs = [pl.BlockSpec((tq, w), lambda b, h, i: (b * nq + i, qb + h)),
                pl.BlockSpec((SEQ, w), lambda b, h, i: (b, kb + h)),
                pl.BlockSpec((SEQ, w), lambda b, h, i: (b, vb + h))]
    for a in extra:
        in_specs.append(pl.BlockSpec(a.shape, lambda b, h, i: (0, 0)))
    return pl.pallas_call(
        functools.partial(_attn_kernel, ncomp=ncomp, heads=heads, qtiles=qtiles),
        grid=(BATCH, HEADS // heads, nq),
        in_specs=in_specs,
        out_specs=pl.BlockSpec((tq, w), lambda b, h, i: (b * nq + i, h)),
        out_shape=jax.ShapeDtypeStruct((N_TOK, HEADS * HEAD_W), bf16),
        scratch_shapes=[pltpu.VMEM((qtiles * heads * ncomp, HEAD_W, TQ), f32),
                        pltpu.VMEM((SEQ // TQ, heads, HEAD_W, TQ), bf16)],
        compiler_params=_cparams(("arbitrary", "arbitrary", "arbitrary")),
        name="diff_attn" if extra else "mla_attn",
    )(q_arr, k_arr, v_arr, *extra)


RT_GROUP_ROW = 0
RT_EXPERT_ROW = 8


def _route(hf, w_ref, b_ref, eid_ref, cw_ref):
    hf_hi = hf.astype(bf16)
    hf_lo = (hf - hf_hi.astype(f32)).astype(bf16)
    a = jnp.dot(hf_hi, w_ref[...], preferred_element_type=f32)
    b = jnp.dot(hf_lo, w_ref[:, :LANES], preferred_element_type=f32)
    logits = a[:, :LANES] + (a[:, LANES:] + b) + b_ref[...]
    lt = logits.T
    tm = lt.shape[1]
    ridx = lax.broadcasted_iota(i32, (EXPERTS_PER_GROUP, tm), 0).astype(f32)

    def top1(v):
        vmax = jnp.max(v, axis=0, keepdims=True)
        idx = jnp.min(jnp.where(v == vmax, ridx, float(EXPERTS_PER_GROUP)), axis=0, keepdims=True)
        return vmax, idx

    gl = lt[RT_GROUP_ROW:RT_GROUP_ROW + 8, :]
    gmax, gsel = top1(gl)
    p_g = 1.0 / jnp.sum(jnp.exp(gl - gmax), axis=0, keepdims=True)
    el = jnp.zeros((EXPERTS_PER_GROUP, tm), f32)
    for g in range(N_GROUPS):
        r0 = RT_EXPERT_ROW + g * EXPERTS_PER_GROUP
        el = jnp.where(gsel == float(g), lt[r0:r0 + EXPERTS_PER_GROUP, :], el)
    e1, i1 = top1(el)
    e2, i2 = top1(jnp.where(ridx == i1, -jnp.inf, el))
    r = jnp.exp(e2 - e1)
    w1 = 1.0 / (1.0 + r)
    base = gsel * EXPERTS_PER_GROUP
    eid_ref[0:1, :] = (base + i1).astype(i32)
    eid_ref[1:2, :] = (base + i2).astype(i32)
    cw_ref[0:1, :] = p_g * w1
    cw_ref[1:2, :] = p_g * (r * w1)


def _merge_kernel(oa_ref, ob_ref, ga_ref, gb_ref, x_ref, wa_ref, wb_ref, wo_ref, gf_ref, wr_ref, br_ref,
                  x1_ref, hf_ref, eid_ref, cw_ref):
    ya = jnp.dot(oa_ref[...], wa_ref[...], preferred_element_type=f32)
    yb = jnp.dot(ob_ref[...], wb_ref[...], preferred_element_type=f32)
    mixed = (jax.nn.sigmoid(ga_ref[...].astype(f32)) * ya
             + jax.nn.sigmoid(gb_ref[...].astype(f32)) * yb)
    x1 = x_ref[...] + jnp.dot(mixed.astype(bf16), wo_ref[...], preferred_element_type=f32)
    x1_ref[...] = x1
    hf = _rms(x1, gf_ref[...])
    _slab_store(hf_ref, (), hf)
    _route(hf, wr_ref, br_ref, eid_ref, cw_ref)


def _merge(oa, ob, proj, x2d, wa, wb, wo, gf, w_rt, b_rt):
    tm = TM_MERGE
    w = HEADS * HEAD_W
    const = lambda shape: pl.BlockSpec(shape, lambda i: (0, 0), pipeline_mode=pl.Buffered(1))
    return pl.pallas_call(
        _merge_kernel,
        grid=(N_TOK // tm,),
        in_specs=[pl.BlockSpec((tm, w), lambda i: (i, 0)),
                  pl.BlockSpec((tm, w), lambda i: (i, 0)),
                  pl.BlockSpec((tm, D_MODEL), lambda i: (i, COL_GA // D_MODEL)),
                  pl.BlockSpec((tm, D_MODEL), lambda i: (i, COL_GB // D_MODEL)),
                  pl.BlockSpec((tm, D_MODEL), lambda i: (i, 0)),
                  const((w, D_MODEL)), const((w, D_MODEL)), const((D_MODEL, D_MODEL)),
                  const((1, D_MODEL)), const((D_MODEL, 2 * LANES)), const((1, LANES))],
        out_specs=[pl.BlockSpec((tm, D_MODEL), lambda i: (i, 0)),
                   pl.BlockSpec((tm * SLAB, LANES), lambda i: (i, 0)),
                   pl.BlockSpec((TOP_K, tm), lambda i: (0, i)),
                   pl.BlockSpec((TOP_K, tm), lambda i: (0, i))],
        out_shape=[jax.ShapeDtypeStruct((N_TOK, D_MODEL), f32),
                   jax.ShapeDtypeStruct((N_TOK * SLAB, LANES), f32),
                   jax.ShapeDtypeStruct((TOP_K, N_TOK), i32),
                   jax.ShapeDtypeStruct((TOP_K, N_TOK), f32)],
        compiler_params=_cparams(("arbitrary",)),
        name="merge_route",
    )(oa, ob, proj, proj, x2d, wa, wb, wo, gf, w_rt, b_rt)


PLAN_N = 128


def _plan_kernel(eid_ref, dest_ref, ends_ref):
    eid = eid_ref[...]
    r = lax.broadcasted_iota(i32, (PLAN_N, PLAN_N), 0)
    c = lax.broadcasted_iota(i32, (PLAN_N, PLAN_N), 1)
    upper = (r < c).astype(bf16)
    lower = (c < r).astype(bf16)
    ones = jnp.ones((PLAN_N, PLAN_N), bf16)
    lane = lax.broadcasted_iota(i32, (1, LANES), 1)
    dest = jnp.zeros((PLAN_N, PLAN_N), f32)
    ends = jnp.zeros((1, LANES), f32)
    start = jnp.zeros((1, 1), f32)
    for e in range(N_EXPERTS):
        m = (eid == e).astype(f32)
        mb = m.astype(bf16)
        within = jnp.dot(mb, upper, preferred_element_type=f32)
        rowtot = jnp.dot(mb, ones, preferred_element_type=f32)
        rowpre = jnp.dot(lower, rowtot.astype(bf16), preferred_element_type=f32)
        cnt = jnp.sum(jnp.sum(m, axis=1, keepdims=True), axis=0, keepdims=True)
        dest = dest + m * (start + within + rowpre)
        start = start + jnp.floor((cnt + (ROW_BLK - 1)) * (1.0 / ROW_BLK)) * ROW_BLK
        ends = jnp.where(lane == e, start, ends)
    dest_ref[...] = dest.astype(i32)
    ends_ref[...] = ends.astype(i32)


def _plan(eid):
    whole = lambda shape: pl.BlockSpec(shape, lambda: (0, 0))
    return pl.pallas_call(
        _plan_kernel,
        in_specs=[whole((PLAN_N, PLAN_N))],
        out_specs=[whole((PLAN_N, PLAN_N)), whole((1, LANES))],
        out_shape=[jax.ShapeDtypeStruct((PLAN_N, PLAN_N), i32),
                   jax.ShapeDtypeStruct((1, LANES), i32)],
        name="moe_plan",
    )(eid.reshape(PLAN_N, PLAN_N))


def _dispatch_kernel(ends_ref, dest_ref, hf_ref, xs_hbm, zbuf, sem):
    i = pl.program_id(0)

    def tail_copy(e):
        pend = ends_ref[e]
        prev = jnp.where(e > 0, ends_ref[jnp.maximum(e - 1, 0)], 0)
        row0 = pl.multiple_of(jnp.maximum(pend - ROW_BLK, 0), ROW_BLK)
        return pend > prev, pltpu.make_async_copy(zbuf, xs_hbm.at[_slab_rows(row0, ROW_BLK), :], sem)

    @pl.when(i == 0)
    def _():
        zbuf[...] = jnp.zeros_like(zbuf)

        def zstart(e, _):
            used, cp = tail_copy(e)

            @pl.when(used)
            def _():
                cp.start()
            return 0

        def zwait(e, _):
            used, cp = tail_copy(e)

            @pl.when(used)
            def _():
                cp.wait()
            return 0

        def unused_copy(b):
            row0 = pl.multiple_of(b * ROW_BLK, ROW_BLK)
            return pltpu.make_async_copy(zbuf, xs_hbm.at[_slab_rows(row0, ROW_BLK), :], sem)

        def ustart(b, _):
            unused_copy(b).start()
            return 0

        def uwait(b, _):
            unused_copy(b).wait()
            return 0

        n_used = ends_ref[N_EXPERTS - 1] // ROW_BLK
        lax.fori_loop(0, N_EXPERTS, zstart, 0)
        lax.fori_loop(n_used, N_BLKS, ustart, 0)
        lax.fori_loop(0, N_EXPERTS, zwait, 0)
        lax.fori_loop(n_used, N_BLKS, uwait, 0)

    base = i * TT_DISP

    def row_copy(t, k):
        d = dest_ref[k, base + t]
        return pltpu.make_async_copy(hf_ref.at[_slab_rows(t, 1), :], xs_hbm.at[_slab_rows(d, 1), :], sem)

    def issue(t, _):
        for k in range(TOP_K):
            row_copy(t, k).start(priority=k)
        return 0

    def drain(t, _):
        for k in range(TOP_K):
            row_copy(t, k).wait()
        return 0

    lax.fori_loop(0, TT_DISP, issue, 0, unroll=ISSUE_UNROLL)
    lax.fori_loop(0, TT_DISP, drain, 0, unroll=ISSUE_UNROLL)


def _dispatch(ends, dest, hf):
    return pl.pallas_call(
        _dispatch_kernel,
        grid_spec=pltpu.PrefetchScalarGridSpec(
            num_scalar_prefetch=2,
            grid=(N_TOK // TT_DISP,),
            in_specs=[pl.BlockSpec((TT_DISP * SLAB, LANES), lambda i, e, d: (i, 0))],
            out_specs=pl.BlockSpec(memory_space=pl.ANY),
            scratch_shapes=[pltpu.VMEM((ROW_BLK * SLAB, LANES), f32),
                            pltpu.SemaphoreType.DMA(())]),
        out_shape=jax.ShapeDtypeStruct((N_ROWS * SLAB, LANES), f32),
        compiler_params=_cparams(("arbitrary",)),
        name="moe_dispatch",
    )(ends, dest, hf)


W_RING = 3
X_RING = 6
Y_RING = 3


def _ffn_kernel(ends_ref, wg_hbm, wu_hbm, wd_hbm, xs_hbm, y_hbm,
                wg_f, wu_f, wd_f, wg_s, wu_s, wd_s, xbuf, ybuf, sem_w, sem_x, sem_y):
    n_used = ends_ref[N_EXPERTS - 1] // ROW_BLK

    def w_copies(ex):
        slot = ex % W_RING
        return [pltpu.make_async_copy(hbm.at[ex], buf.at[slot], sem_w.at[i, slot])
                for i, (hbm, buf) in enumerate(((wg_hbm, wg_f), (wu_hbm, wu_f), (wd_hbm, wd_f)))]

    def advance(e_from, e_to):
        def one(ex, _):
            for cp in w_copies(ex):
                cp.wait()

            @pl.when(ex + (W_RING - 1) < N_EXPERTS)
            def _():
                for cp in w_copies(ex + (W_RING - 1)):
                    cp.start()
            return 0

        lax.fori_loop(e_from + 1, e_to + 1, one, 0)

    def x_copy(b):
        slot = b % X_RING
        return pltpu.make_async_copy(xs_hbm.at[_slab_rows(b * ROW_BLK, ROW_BLK), :], xbuf.at[slot],
                                     sem_x.at[slot])

    def y_copy(b):
        slot = b % Y_RING
        return pltpu.make_async_copy(ybuf.at[slot], y_hbm.at[_slab_rows(b * ROW_BLK, ROW_BLK), :],
                                     sem_y.at[slot])

    for ex in range(W_RING - 1):
        for cp in w_copies(ex):
            cp.start()
    for b0 in range(X_RING - 1):
        @pl.when(b0 < n_used)
        def _():
            x_copy(b0).start()

    def body(b, e):
        @pl.when(b + (X_RING - 1) < n_used)
        def _():
            x_copy(b + (X_RING - 1)).start()

        e_new = lax.while_loop(lambda ex: b * ROW_BLK >= ends_ref[ex], lambda ex: ex + 1,
                               jnp.maximum(e, 0))

        @pl.when(e_new != e)
        def _():
            advance(e, e_new)
            wslot = e_new % W_RING
            wg_s[...] = wg_f[wslot].astype(bf16)
            wu_s[...] = wu_f[wslot].astype(bf16)
            wd_s[...] = wd_f[wslot].astype(bf16)

        x_copy(b).wait()

        @pl.when(b >= Y_RING)
        def _():
            y_copy(b - Y_RING).wait()

        x = _slab_load(xbuf, (b % X_RING,), ROW_BLK).astype(bf16)
        h1 = jnp.dot(x, wg_s[...], preferred_element_type=f32)
        h2 = jnp.dot(x, wu_s[...], preferred_element_type=f32)
        a = (h1 * jax.nn.sigmoid(h1) * h2).astype(bf16)
        _slab_store(ybuf, (b % Y_RING,), jnp.dot(a, wd_s[...], preferred_element_type=f32))
        y_copy(b).start()
        return e_new

    e_last = lax.fori_loop(0, n_used, body, jnp.int32(-1))
    advance(e_last, jnp.int32(N_EXPERTS - 1))

    for back in range(Y_RING, 0, -1):
        @pl.when(n_used >= back)
        def _():
            y_copy(n_used - back).wait()

    ybuf[0] = jnp.zeros((ROW_BLK * SLAB, LANES), f32)

    def unused_copy(b):
        return pltpu.make_async_copy(ybuf.at[0], y_hbm.at[_slab_rows(b * ROW_BLK, ROW_BLK), :],
                                     sem_y.at[0])

    def ustart(b, _):
        unused_copy(b).start()
        return 0

    def uwait(b, _):
        unused_copy(b).wait()
        return 0

    lax.fori_loop(n_used, N_BLKS, ustart, 0)
    lax.fori_loop(n_used, N_BLKS, uwait, 0)


def _expert_ffn(ends, xs, wg, wu, wd):
    return pl.pallas_call(
        _ffn_kernel,
        grid_spec=pltpu.PrefetchScalarGridSpec(
            num_scalar_prefetch=1,
            grid=(1,),
            in_specs=[pl.BlockSpec(memory_space=pl.ANY)] * 4,
            out_specs=pl.BlockSpec(memory_space=pl.ANY),
            scratch_shapes=[pltpu.VMEM((W_RING, D_MODEL, EXPERT_FF), f32),
                            pltpu.VMEM((W_RING, D_MODEL, EXPERT_FF), f32),
                            pltpu.VMEM((W_RING, EXPERT_FF, D_MODEL), f32),
                            pltpu.VMEM((D_MODEL, EXPERT_FF), bf16),
                            pltpu.VMEM((D_MODEL, EXPERT_FF), bf16),
                            pltpu.VMEM((EXPERT_FF, D_MODEL), bf16),
                            pltpu.VMEM((X_RING, ROW_BLK * SLAB, LANES), f32),
                            pltpu.VMEM((Y_RING, ROW_BLK * SLAB, LANES), f32),
                            pltpu.SemaphoreType.DMA((3, W_RING)),
                            pltpu.SemaphoreType.DMA((X_RING,)),
                            pltpu.SemaphoreType.DMA((Y_RING,))]),
        out_shape=jax.ShapeDtypeStruct((N_ROWS * SLAB, LANES), f32),
        compiler_params=_cparams(("arbitrary",)),
        name="expert_ffn",
    )(ends, wg, wu, wd, xs)


def _combine_kernel(dest_ref, x_ref, cw_ref, g_ref, y_hbm, o_ref, ybuf, sem):
    i = pl.program_id(0)
    n_steps = pl.num_programs(0)

    def row_copy(step, t, k):
        slot = step % 2
        d = dest_ref[k, step * TM_COMB + t]
        return pltpu.make_async_copy(y_hbm.at[_slab_rows(d, 1), :],
                                     ybuf.at[slot, k, _slab_rows(t, 1), :], sem.at[slot])

    def issue(step):
        def one(t, _):
            for k in range(TOP_K):
                row_copy(step, t, k).start()
            return 0
        lax.fori_loop(0, TM_COMB, one, 0, unroll=ISSUE_UNROLL)

    def drain(step):
        def one(t, _):
            for k in range(TOP_K):
                row_copy(step, t, k).wait()
            return 0
        lax.fori_loop(0, TM_COMB, one, 0, unroll=ISSUE_UNROLL)

    @pl.when(i == 0)
    def _():
        issue(i)

    @pl.when(i + 1 < n_steps)
    def _():
        issue(i + 1)

    drain(i)
    slot = i % 2
    w = cw_ref[...]
    x = (x_ref[...] + w[:, 0:1] * _slab_load(ybuf, (slot, 0), TM_COMB)
         + w[:, 1:2] * _slab_load(ybuf, (slot, 1), TM_COMB))
    o_ref[...] = _rms(x, g_ref[...])


def _combine(dest, x1, cw_t, g, y):
    tm = TM_COMB
    return pl.pallas_call(
        _combine_kernel,
        grid_spec=pltpu.PrefetchScalarGridSpec(
            num_scalar_prefetch=1,
            grid=(N_TOK // tm,),
            in_specs=[pl.BlockSpec((tm, D_MODEL), lambda i, d: (i, 0)),
                      pl.BlockSpec((tm, TOP_K), lambda i, d: (i, 0)),
                      pl.BlockSpec((1, D_MODEL), lambda i, d: (0, 0)),
                      pl.BlockSpec(memory_space=pl.ANY)],
            out_specs=pl.BlockSpec((tm, D_MODEL), lambda i, d: (i, 0)),
            scratch_shapes=[pltpu.VMEM((2, TOP_K, tm * SLAB, LANES), f32),
                            pltpu.SemaphoreType.DMA((2,))]),
        out_shape=jax.ShapeDtypeStruct((N_TOK, D_MODEL), f32),
        compiler_params=_cparams(("arbitrary",)),
        name="moe_combine",
    )(dest, x1, cw_t, g, y)


TR_RELAYOUT = 256
KR_TILE = COL_KR // TR_RELAYOUT


def _w_in_relayout_kernel(w_ref, o_ref):
    i = pl.program_id(0)

    @pl.when(i < COL_V // TR_RELAYOUT)
    def _():
        q = ROPE_NA
        for g in range(TR_RELAYOUT // LANES):
            r = g * LANES
            o_ref[r:r + q, :] = w_ref[r:r + q, :].astype(bf16)
            o_ref[r + q:r + 2 * q, :] = w_ref[r + 2 * q:r + 3 * q, :].astype(bf16)
            o_ref[r + 2 * q:r + 3 * q, :] = w_ref[r + q:r + 2 * q, :].astype(bf16)
            o_ref[r + 3 * q:r + 4 * q, :] = w_ref[r + 3 * q:r + 4 * q, :].astype(bf16)

    @pl.when((i >= COL_V // TR_RELAYOUT) & (i != KR_TILE))
    def _():
        o_ref[...] = w_ref[...].astype(bf16)

    @pl.when(i == KR_TILE)
    def _():
        o_ref[...] = jnp.zeros_like(o_ref)
        o_ref[KR_LANE:KR_LANE + MLA_ROPE, :] = w_ref[0:MLA_ROPE, :].astype(bf16)


def _prep_w_in(w_in_t):
    n_src, d = w_in_t.shape
    gate_src = COL_KR + MLA_ROPE

    gate_tile = COL_GA // TR_RELAYOUT
    sub = 8
    shift = (COL_GA - gate_src) // sub

    def src_row(i):
        return ((TR_RELAYOUT // sub) * i - shift * (i // gate_tile)) * sub

    return pl.pallas_call(
        _w_in_relayout_kernel,
        grid=(PROJ_W // TR_RELAYOUT,),
        in_specs=[pl.BlockSpec((pl.Element(TR_RELAYOUT), pl.Element(d)), lambda i: (src_row(i), 0))],
        out_specs=pl.BlockSpec((TR_RELAYOUT, d), lambda i: (i, 0)),
        out_shape=jax.ShapeDtypeStruct((PROJ_W, d), bf16),
        compiler_params=_cparams(("arbitrary",)),
        name="w_in_relayout",
    )(w_in_t)


def _prep_w_uq(w_uq):
    w = w_uq.reshape(MLA_Q_RANK, HEADS, MLA_NOPE + MLA_ROPE)
    w = jnp.pad(w, ((0, 0), (0, 0), (0, HEAD_W - MLA_NOPE - MLA_ROPE)))
    return w.reshape(MLA_Q_RANK, HEADS * HEAD_W).astype(bf16)


def _prep_w_ukv(w_ukv):
    w = w_ukv.reshape(MLA_KV_RANK, HEADS, MLA_NOPE + HEAD_W)
    wk = jnp.pad(w[:, :, :MLA_NOPE], ((0, 0), (0, 0), (0, HEAD_W - MLA_NOPE)))
    wv = w[:, :, MLA_NOPE:]
    return jnp.concatenate([wk.reshape(MLA_KV_RANK, -1), wv.reshape(MLA_KV_RANK, -1)], axis=1).astype(bf16)


def _prep_router(w_group, b_group, w_router, b_router):
    z = lambda n: jnp.zeros((D_MODEL, n), f32)
    w = jnp.concatenate([w_group, z(RT_EXPERT_ROW - N_GROUPS), w_router,
                         z(LANES - RT_EXPERT_ROW - N_EXPERTS)], axis=1)
    b = jnp.concatenate([b_group, jnp.full((RT_EXPERT_ROW - N_GROUPS,), NEG_INF, f32), b_router,
                         jnp.zeros((LANES - RT_EXPERT_ROW - N_EXPERTS,), f32)])[None, :]
    w_hi = w.astype(bf16)
    w_lo = (w - w_hi.astype(f32)).astype(bf16)
    return jnp.concatenate([w_hi, w_lo], axis=1), b


def kernel(x, positions, attn_norm, w_in, da_lambda, da_subln, mla_q_norm, mla_w_uq, mla_kv_norm, mla_w_ukv, w_branch_a, w_branch_b, w_out, ffn_norm, w_group, b_group, w_router, b_router, w_exp_gate, w_exp_up, w_exp_down, final_norm):
    x2d = x.reshape(N_TOK, D_MODEL)
    cos_t, sin_t = _rope_tables(positions)

    proj = _in_proj(x2d, attn_norm[0][None, :], _prep_w_in(w_in[0].T), cos_t, sin_t)
    qb, kb, vb = _mla_prep(proj, mla_q_norm[0][None, :], mla_kv_norm[0][None, :],
                           _prep_w_uq(mla_w_uq[0]), _prep_w_ukv(mla_w_ukv[0]), cos_t, sin_t)
    oa = _attention(proj, proj, proj, COL_Q // HEAD_W, COL_K // HEAD_W, COL_V // HEAD_W,
                    heads=8, qtiles=2, extra=(da_lambda[0], da_subln[0][:, None]))
    ob = _attention(qb, kb, vb, 0, 0, 0, heads=8, qtiles=4)
    w_rt, b_rt = _prep_router(w_group[0], b_group[0], w_router[0], b_router[0])
    x1, hf, eid, cw = _merge(oa, ob, proj, x2d, w_branch_a[0].astype(bf16), w_branch_b[0].astype(bf16),
                             w_out[0].astype(bf16), ffn_norm[0][None, :], w_rt, b_rt)

    dest, ends = _plan(eid)
    dest = dest.reshape(TOP_K, N_TOK)
    ends = ends[0, :N_EXPERTS]
    xs = _dispatch(ends, dest, hf)
    y = _expert_ffn(ends, xs, w_exp_gate[0], w_exp_up[0], w_exp_down[0])
    out = _combine(dest, x1, cw.T, final_norm[None, :], y)
    return out.reshape(BATCH, SEQ, D_MODEL)
```

```python
import functools
import math

import jax
import jax.numpy as jnp
from jax import lax
from jax.experimental import pallas as pl
from jax.experimental.pallas import tpu as pltpu

f32 = jnp.float32
bf16 = jnp.bfloat16
i32 = jnp.int32

D_MODEL = 2048
BATCH = 4
SEQ = 2048
N_TOK = BATCH * SEQ
HEADS = 8
HEAD_W = 128
DA_HALF = 64
MLA_Q_RANK = 512
MLA_KV_RANK = 256
MLA_NOPE = 64
MLA_ROPE = 32
N_GROUPS = 4
EXPERTS_PER_GROUP = 8
N_EXPERTS = N_GROUPS * EXPERTS_PER_GROUP
TOP_K = 2
EXPERT_FF = 512
ROPE_THETA = 10000.0
NORM_EPS = 1e-6
NEG_INF = -1e30
LAM_INIT = 0.8 - 0.6 * math.exp(-0.3 * 0)
LOG2E = math.log2(math.e)

LANES = 128
VMEM_LIMIT = 56 * 1024 * 1024

COL_Q, COL_K, COL_V = 0, 1024, 2048
COL_CQ, COL_CKV = 3072, 3584
COL_KR = 3840
COL_GA, COL_GB = 4096, 6144
PROJ_W = 8192
KR_LANE = MLA_NOPE

TM_PROJ, TN_PROJ = 1024, 1024
TM_PREP = 512
TQ = 256
TM_MERGE = 256
ROW_BLK = 128
N_ROWS = (N_TOK * TOP_K // ROW_BLK + N_EXPERTS) * ROW_BLK
N_BLKS = N_ROWS // ROW_BLK
TT_DISP = 1024
TM_COMB = 128

Q_SCALE_A = (DA_HALF ** -0.5) * LOG2E
Q_SCALE_B = ((MLA_NOPE + MLA_ROPE) ** -0.5) * LOG2E


def _cparams(sem, **kw):
    return pltpu.CompilerParams(dimension_semantics=sem, vmem_limit_bytes=VMEM_LIMIT, **kw)


def _rms(x, g):
    return x * lax.rsqrt(jnp.mean(x * x, axis=-1, keepdims=True) + NORM_EPS) * g


SLAB = D_MODEL // LANES


ISSUE_UNROLL = 16


def _slab_rows(row0, nrows):
    return pl.ds(pl.multiple_of(row0 * SLAB, SLAB), nrows * SLAB)


def _slab_store(ref, idx, val):
    rows = val.shape[0]
    for j in range(SLAB):
        ref[idx + (pl.ds(j, rows, stride=SLAB), slice(None))] = val[:, j * LANES:(j + 1) * LANES]


def _slab_chunk(ref, idx, rows, j):
    return ref[idx + (pl.ds(j, rows, stride=SLAB), slice(None))]


def _slab_load(ref, idx, rows):
    return jnp.concatenate([_slab_chunk(ref, idx, rows, j) for j in range(SLAB)], axis=1)


def _rope_lanes(blk, cos, sin, first_half, half):
    up = pltpu.roll(blk, LANES - half, 1)
    down = pltpu.roll(blk, half, 1)
    return blk * cos + jnp.where(first_half, up, down) * sin


ROPE_NA = DA_HALF // 2
ROPE_NB = MLA_ROPE // 2


def _rope_tables_kernel(pos_ref, inv_ref, cos_ref, sin_ref):
    half = pos_ref.shape[0] // 2
    lane = lax.broadcasted_iota(i32, (half, LANES), 1)
    pos = jnp.where(lane < LANES // 2, pos_ref[:half, :], pos_ref[half:, :]).astype(f32)
    ang = pos * inv_ref[...]
    c = jnp.cos(ang)
    s = jnp.sin(ang)
    one = lambda n: jnp.ones((half, n), f32)
    zero = lambda n: jnp.zeros((half, n), f32)
    tail = LANES - KR_LANE - MLA_ROPE
    for part in range(2):
        o = part * (LANES // 2)
        rows = slice(part * half, (part + 1) * half)
        ca, sa = c[:, o:o + ROPE_NA], s[:, o:o + ROPE_NA]
        cb, sb = c[:, o + ROPE_NA:o + ROPE_NA + ROPE_NB], s[:, o + ROPE_NA:o + ROPE_NA + ROPE_NB]
        cos_ref[rows, :] = jnp.concatenate([ca, ca, ca, ca, one(KR_LANE), cb, cb, one(tail)], axis=1)
        sin_ref[rows, :] = jnp.concatenate([-sa, -sa, sa, sa, zero(KR_LANE), -sb, sb, zero(tail)],
                                           axis=1)


def _rope_tables(positions):
    inv_a = ROPE_THETA ** (-jnp.arange(0, DA_HALF, 2, dtype=f32) / DA_HALF)
    inv_b = ROPE_THETA ** (-jnp.arange(0, MLA_ROPE, 2, dtype=f32) / MLA_ROPE)
    inv = jnp.concatenate([inv_a, inv_b, jnp.zeros((LANES // 2 - ROPE_NA - ROPE_NB,), f32)])
    inv = jnp.tile(inv, 2)[None, :]
    pos = positions.reshape(N_TOK, 1)
    tm = 1024
    return pl.pallas_call(
        _rope_tables_kernel,
        grid=(N_TOK // tm,),
        in_specs=[pl.BlockSpec((tm, 1), lambda i: (i, 0)),
                  pl.BlockSpec((1, LANES), lambda i: (0, 0))],
        out_specs=[pl.BlockSpec((tm, 2 * LANES), lambda i: (i, 0)),
                   pl.BlockSpec((tm, 2 * LANES), lambda i: (i, 0))],
        out_shape=[jax.ShapeDtypeStruct((N_TOK, 2 * LANES), f32)] * 2,
        compiler_params=_cparams(("arbitrary",)),
        name="rope_tables",
    )(pos, inv)


def _in_proj_kernel(x_ref, g_ref, w_ref, cos_ref, sin_ref, o_ref, h_scr):
    n = pl.program_id(1)

    @pl.when(n == 0)
    def _():
        h_scr[...] = _rms(x_ref[...], g_ref[...]).astype(bf16)

    acc = lax.dot_general(h_scr[...], w_ref[...], (((1,), (1,)), ((), ())),
                          preferred_element_type=f32)
    n_rope = COL_V // TN_PROJ
    n_q = COL_K // TN_PROJ

    @pl.when(n < n_rope)
    def _():
        c = jnp.where(n < n_q, Q_SCALE_A, 1.0).astype(f32)
        cos = cos_ref[...] * c
        sin = sin_ref[...] * c
        for j in range(TN_PROJ // LANES):
            blk = acc[:, j * LANES:(j + 1) * LANES]
            rot = pltpu.roll(blk, LANES // 2, 1)
            o_ref[:, j * LANES:(j + 1) * LANES] = (blk * cos + rot * sin).astype(bf16)

    @pl.when(n >= n_rope)
    def _():
        o_ref[...] = acc.astype(bf16)


def _in_proj(x2d, g, w_pad, cos_t, sin_t):
    return pl.pallas_call(
        _in_proj_kernel,
        grid=(N_TOK // TM_PROJ, PROJ_W // TN_PROJ),
        in_specs=[pl.BlockSpec((TM_PROJ, D_MODEL), lambda i, n: (i, 0)),
                  pl.BlockSpec((1, D_MODEL), lambda i, n: (0, 0)),
                  pl.BlockSpec((TN_PROJ, D_MODEL), lambda i, n: (n, 0)),
                  pl.BlockSpec((TM_PROJ, LANES), lambda i, n: (i, 0)),
                  pl.BlockSpec((TM_PROJ, LANES), lambda i, n: (i, 0))],
        out_specs=pl.BlockSpec((TM_PROJ, TN_PROJ), lambda i, n: (i, n)),
        out_shape=jax.ShapeDtypeStruct((N_TOK, PROJ_W), bf16),
        scratch_shapes=[pltpu.VMEM((TM_PROJ, D_MODEL), bf16)],
        compiler_params=_cparams(("arbitrary", "arbitrary")),
        name="in_proj",
    )(x2d, g, w_pad, cos_t, sin_t)


def _mla_prep_kernel(cq_ref, ckv_ref, kr_ref, gq_ref, gkv_ref, wuq_ref, wkv_ref, cos_ref, sin_ref,
                     q_ref, k_ref, v_ref):
    cqn = _rms(cq_ref[...].astype(f32), gq_ref[...]).astype(bf16)
    q = jnp.dot(cqn, wuq_ref[...], preferred_element_type=f32)
    ckvn = _rms(ckv_ref[...].astype(f32), gkv_ref[...]).astype(bf16)
    kv = jnp.dot(ckvn, wkv_ref[...], preferred_element_type=f32)
    cos = cos_ref[...]
    sin = sin_ref[...]
    lane = lax.broadcasted_iota(i32, (TM_PREP, LANES), 1)
    first = lane < (KR_LANE + MLA_ROPE // 2)
    rope = lambda b: _rope_lanes(b, cos, sin, first, MLA_ROPE // 2)
    k_rot = rope(kr_ref[...].astype(f32))
    for h in range(HEADS):
        sl = slice(h * HEAD_W, (h + 1) * HEAD_W)
        q_ref[:, sl] = (rope(q[:, sl]) * Q_SCALE_B).astype(bf16)
        k_ref[:, sl] = (kv[:, sl] + k_rot).astype(bf16)
    v_ref[...] = kv[:, HEADS * HEAD_W:].astype(bf16)


def _mla_prep(proj, gq, gkv, wuq_p, wkv_p, cos_t, sin_t):
    tm = TM_PREP
    w = HEADS * HEAD_W
    return pl.pallas_call(
        _mla_prep_kernel,
        grid=(N_TOK // tm,),
        in_specs=[pl.BlockSpec((tm, MLA_Q_RANK), lambda i: (i, COL_CQ // MLA_Q_RANK)),
                  pl.BlockSpec((tm, MLA_KV_RANK), lambda i: (i, COL_CKV // MLA_KV_RANK)),
                  pl.BlockSpec((tm, LANES), lambda i: (i, COL_KR // LANES)),
                  pl.BlockSpec((1, MLA_Q_RANK), lambda i: (0, 0)),
                  pl.BlockSpec((1, MLA_KV_RANK), lambda i: (0, 0)),
                  pl.BlockSpec((MLA_Q_RANK, w), lambda i: (0, 0)),
                  pl.BlockSpec((MLA_KV_RANK, 2 * w), lambda i: (0, 0)),
                  pl.BlockSpec((tm, LANES), lambda i: (i, 1)),
                  pl.BlockSpec((tm, LANES), lambda i: (i, 1))],
        out_specs=[pl.BlockSpec((tm, w), lambda i: (i, 0))] * 3,
        out_shape=[jax.ShapeDtypeStruct((N_TOK, w), bf16)] * 3,
        compiler_params=_cparams(("arbitrary",)),
        name="mla_prep",
    )(proj, proj, proj, gq, gkv, wuq_p, wkv_p, cos_t, sin_t)


def _attn_kernel(*refs, ncomp, heads, qtiles):
    if ncomp == 2:
        q_ref, k_ref, v_ref, lam_ref, g_ref, o_ref, acc_scr, vt_scr = refs
    else:
        q_ref, k_ref, v_ref, o_ref, acc_scr, vt_scr = refs
    qi = pl.program_id(2)

    @pl.when(qi == 0)
    def _():
        for j in range(SEQ // TQ):
            for h in range(heads):
                vt_scr[j, h] = v_ref[j * TQ:(j + 1) * TQ, h * HEAD_W:(h + 1) * HEAD_W].T

    lane = lax.broadcasted_iota(i32, (TQ, HEAD_W), 1)
    qs = []
    for t in range(qtiles):
        for h in range(heads):
            q = q_ref[t * TQ:(t + 1) * TQ, h * HEAD_W:(h + 1) * HEAD_W]
            if ncomp == 2:
                zero = jnp.zeros_like(q)
                in_a = (lane % DA_HALF) < (DA_HALF // 2)
                qs += [jnp.where(in_a, q, zero), jnp.where(in_a, zero, q)]
            else:
                qs.append(q)
    per_tile = heads * ncomp
    head_of = lambda n: (n % per_tile) // ncomp

    def step(j, carry, modes):
        off = pl.multiple_of(j * TQ, TQ)
        live = [n for n in range(len(qs)) if modes[n // per_tile] != "skip"]
        ss = {}
        for n in live:
            h = head_of(n)
            k = k_ref[pl.ds(off, TQ), h * HEAD_W:(h + 1) * HEAD_W]
            s = lax.dot_general(k, qs[n], (((1,), (1,)), ((), ())), preferred_element_type=f32)
            if modes[n // per_tile] == "diag":
                key = lax.broadcasted_iota(i32, (TQ, TQ), 0)
                qry = lax.broadcasted_iota(i32, (TQ, TQ), 1)
                s = jnp.where(key <= qry, s, NEG_INF)
            ss[n] = s
        stats = {}
        for n in live:
            m, l = carry[n]
            s = ss[n]
            m_new = jnp.maximum(m, jnp.max(s, axis=0, keepdims=True))
            alpha = jnp.exp2(m - m_new)
            p = jnp.exp2(s - m_new)
            l_new = alpha * l + jnp.sum(p, axis=0, keepdims=True)
            stats[n] = (m_new, l_new, alpha, p.astype(bf16))
        out = list(carry)
        for n in live:
            m_new, l_new, alpha, p = stats[n]
            pv = jnp.dot(vt_scr[j, head_of(n)], p, preferred_element_type=f32)
            acc_scr[n] = alpha * acc_scr[n] + pv
            out[n] = (m_new, l_new)
        return tuple(out)

    acc_scr[...] = jnp.zeros_like(acc_scr)
    init = tuple((jnp.full((1, TQ), NEG_INF, f32), jnp.zeros((1, TQ), f32)) for _ in qs)
    first = qi * qtiles
    carry = lax.fori_loop(0, first, lambda j, c: step(j, c, ("full",) * qtiles), init)
    for u in range(qtiles):
        modes = tuple("skip" if t < u else ("diag" if t == u else "full") for t in range(qtiles))
        carry = step(first + u, carry, modes)
    carry = tuple((m, l, acc_scr[n]) for n, (m, l) in enumerate(carry))

    if ncomp == 2:
        ll = lam_ref[...]
        s1 = jnp.sum(ll[0:1, :] * ll[1:2, :], axis=-1, keepdims=True)
        s2 = jnp.sum(ll[2:3, :] * ll[3:4, :], axis=-1, keepdims=True)
        lam = jnp.exp(s1) - jnp.exp(s2) + LAM_INIT
    for t in range(qtiles):
        for h in range(heads):
            n0 = (t * heads + h) * ncomp
            if ncomp == 2:
                (_, l0, a0), (_, l1, a1) = carry[n0], carry[n0 + 1]
                o = a0 / l0 - lam * (a1 / l1)
                ms = jnp.mean(o * o, axis=0, keepdims=True)
                o = o * lax.rsqrt(ms + NORM_EPS) * g_ref[...] * (1.0 - LAM_INIT)
            else:
                _, l0, a0 = carry[n0]
                o = a0 / l0
            o_ref[t * TQ:(t + 1) * TQ, h * HEAD_W:(h + 1) * HEAD_W] = o.T.astype(bf16)


def _attention(q_arr, k_arr, v_arr, q_blk0, k_blk0, v_blk0, heads, qtiles, extra=()):
    ncomp = 2 if extra else 1
    tq = qtiles * TQ
    nq = SEQ // tq
    w = heads * HEAD_W
    qb, kb, vb = q_blk0 // heads, k_blk0 // heads, v_blk0 // heads
    in_specs = [pl.BlockSpec((tq, w), lambda b, h, i: (b * nq + i, qb + h)),
                pl.BlockSpec((SEQ, w), lambda b, h, i: (b, kb + h)),
                pl.BlockSpec((SEQ, w), lambda b, h, i: (b, vb + h))]
    for a in extra:
        in_specs.append(pl.BlockSpec(a.shape, lambda b, h, i: (0, 0)))
    return pl.pallas_call(
        functools.partial(_attn_kernel, ncomp=ncomp, heads=heads, qtiles=qtiles),
        grid=(BATCH, HEADS // heads, nq),
        in_specs=in_specs,
        out_specs=pl.BlockSpec((tq, w), lambda b, h, i: (b * nq + i, h)),
        out_shape=jax.ShapeDtypeStruct((N_TOK, HEADS * HEAD_W), bf16),
        scratch_shapes=[pltpu.VMEM((qtiles * heads * ncomp, HEAD_W, TQ), f32),
                        pltpu.VMEM((SEQ // TQ, heads, HEAD_W, TQ), bf16)],
        compiler_params=_cparams(("arbitrary", "arbitrary", "arbitrary")),
        name="diff_attn" if extra else "mla_attn",
    )(q_arr, k_arr, v_arr, *extra)


RT_GROUP_ROW = 0
RT_EXPERT_ROW = 8


def _route(hf, w_ref, b_ref, eid_ref, cw_ref):
    hf_hi = hf.astype(bf16)
    hf_lo = (hf - hf_hi.astype(f32)).astype(bf16)
    a = jnp.dot(hf_hi, w_ref[...], preferred_element_type=f32)
    b = jnp.dot(hf_lo, w_ref[:, :LANES], preferred_element_type=f32)
    logits = a[:, :LANES] + (a[:, LANES:] + b) + b_ref[...]
    lt = logits.T
    tm = lt.shape[1]
    ridx = lax.broadcasted_iota(i32, (EXPERTS_PER_GROUP, tm), 0).astype(f32)

    def top1(v):
        vmax = jnp.max(v, axis=0, keepdims=True)
        idx = jnp.min(jnp.where(v == vmax, ridx, float(EXPERTS_PER_GROUP)), axis=0, keepdims=True)
        return vmax, idx

    gl = lt[RT_GROUP_ROW:RT_GROUP_ROW + 8, :]
    gmax, gsel = top1(gl)
    p_g = 1.0 / jnp.sum(jnp.exp(gl - gmax), axis=0, keepdims=True)
    el = jnp.zeros((EXPERTS_PER_GROUP, tm), f32)
    for g in range(N_GROUPS):
        r0 = RT_EXPERT_ROW + g * EXPERTS_PER_GROUP
        el = jnp.where(gsel == float(g), lt[r0:r0 + EXPERTS_PER_GROUP, :], el)
    e1, i1 = top1(el)
    e2, i2 = top1(jnp.where(ridx == i1, -jnp.inf, el))
    r = jnp.exp(e2 - e1)
    w1 = 1.0 / (1.0 + r)
    base = gsel * EXPERTS_PER_GROUP
    eid_ref[0:1, :] = (base + i1).astype(i32)
    eid_ref[1:2, :] = (base + i2).astype(i32)
    cw_ref[0:1, :] = p_g * w1
    cw_ref[1:2, :] = p_g * (r * w1)


def _merge_kernel(oa_ref, ob_ref, ga_ref, gb_ref, x_ref, wa_ref, wb_ref, wo_ref, gf_ref, wr_ref, br_ref,
                  x1_ref, hf_ref, eid_ref, cw_ref):
    ya = jnp.dot(oa_ref[...], wa_ref[...], preferred_element_type=f32)
    yb = jnp.dot(ob_ref[...], wb_ref[...], preferred_element_type=f32)
    mixed = (jax.nn.sigmoid(ga_ref[...].astype(f32)) * ya
             + jax.nn.sigmoid(gb_ref[...].astype(f32)) * yb)
    x1 = x_ref[...] + jnp.dot(mixed.astype(bf16), wo_ref[...], preferred_element_type=f32)
    x1_ref[...] = x1
    hf = _rms(x1, gf_ref[...])
    _slab_store(hf_ref, (), hf)
    _route(hf, wr_ref, br_ref, eid_ref, cw_ref)


def _merge(oa, ob, proj, x2d, wa, wb, wo, gf, w_rt, b_rt):
    tm = TM_MERGE
    w = HEADS * HEAD_W
    const = lambda shape: pl.BlockSpec(shape, lambda i: (0, 0), pipeline_mode=pl.Buffered(1))
    return pl.pallas_call(
        _merge_kernel,
        grid=(N_TOK // tm,),
        in_specs=[pl.BlockSpec((tm, w), lambda i: (i, 0)),
                  pl.BlockSpec((tm, w), lambda i: (i, 0)),
                  pl.BlockSpec((tm, D_MODEL), lambda i: (i, COL_GA // D_MODEL)),
                  pl.BlockSpec((tm, D_MODEL), lambda i: (i, COL_GB // D_MODEL)),
                  pl.BlockSpec((tm, D_MODEL), lambda i: (i, 0)),
                  const((w, D_MODEL)), const((w, D_MODEL)), const((D_MODEL, D_MODEL)),
                  const((1, D_MODEL)), const((D_MODEL, 2 * LANES)), const((1, LANES))],
        out_specs=[pl.BlockSpec((tm, D_MODEL), lambda i: (i, 0)),
                   pl.BlockSpec((tm * SLAB, LANES), lambda i: (i, 0)),
                   pl.BlockSpec((TOP_K, tm), lambda i: (0, i)),
                   pl.BlockSpec((TOP_K, tm), lambda i: (0, i))],
        out_shape=[jax.ShapeDtypeStruct((N_TOK, D_MODEL), f32),
                   jax.ShapeDtypeStruct((N_TOK * SLAB, LANES), f32),
                   jax.ShapeDtypeStruct((TOP_K, N_TOK), i32),
                   jax.ShapeDtypeStruct((TOP_K, N_TOK), f32)],
        compiler_params=_cparams(("arbitrary",)),
        name="merge_route",
    )(oa, ob, proj, proj, x2d, wa, wb, wo, gf, w_rt, b_rt)


PLAN_N = 128


def _plan_kernel(eid_ref, dest_ref, ends_ref):
    eid = eid_ref[...]
    r = lax.broadcasted_iota(i32, (PLAN_N, PLAN_N), 0)
    c = lax.broadcasted_iota(i32, (PLAN_N, PLAN_N), 1)
    upper = (r < c).astype(bf16)
    lower = (c < r).astype(bf16)
    ones = jnp.ones((PLAN_N, PLAN_N), bf16)
    lane = lax.broadcasted_iota(i32, (1, LANES), 1)
    dest = jnp.zeros((PLAN_N, PLAN_N), f32)
    ends = jnp.zeros((1, LANES), f32)
    start = jnp.zeros((1, 1), f32)
    for e in range(N_EXPERTS):
        m = (eid == e).astype(f32)
        mb = m.astype(bf16)
        within = jnp.dot(mb, upper, preferred_element_type=f32)
        rowtot = jnp.dot(mb, ones, preferred_element_type=f32)
        rowpre = jnp.dot(lower, rowtot.astype(bf16), preferred_element_type=f32)
        cnt = jnp.sum(jnp.sum(m, axis=1, keepdims=True), axis=0, keepdims=True)
        dest = dest + m * (start + within + rowpre)
        start = start + jnp.floor((cnt + (ROW_BLK - 1)) * (1.0 / ROW_BLK)) * ROW_BLK
        ends = jnp.where(lane == e, start, ends)
    dest_ref[...] = dest.astype(i32)
    ends_ref[...] = ends.astype(i32)


def _plan(eid):
    whole = lambda shape: pl.BlockSpec(shape, lambda: (0, 0))
    return pl.pallas_call(
        _plan_kernel,
        in_specs=[whole((PLAN_N, PLAN_N))],
        out_specs=[whole((PLAN_N, PLAN_N)), whole((1, LANES))],
        out_shape=[jax.ShapeDtypeStruct((PLAN_N, PLAN_N), i32),
                   jax.ShapeDtypeStruct((1, LANES), i32)],
        name="moe_plan",
    )(eid.reshape(PLAN_N, PLAN_N))


def _dispatch_kernel(ends_ref, dest_ref, hf_ref, xs_hbm, zbuf, sem):
    i = pl.program_id(0)

    def tail_copy(e):
        pend = ends_ref[e]
        prev = jnp.where(e > 0, ends_ref[jnp.maximum(e - 1, 0)], 0)
        row0 = pl.multiple_of(jnp.maximum(pend - ROW_BLK, 0), ROW_BLK)
        return pend > prev, pltpu.make_async_copy(zbuf, xs_hbm.at[_slab_rows(row0, ROW_BLK), :], sem)

    @pl.when(i == 0)
    def _():
        zbuf[...] = jnp.zeros_like(zbuf)

        def zstart(e, _):
            used, cp = tail_copy(e)

            @pl.when(used)
            def _():
                cp.start()
            return 0

        def zwait(e, _):
            used, cp = tail_copy(e)

            @pl.when(used)
            def _():
                cp.wait()
            return 0

        def unused_copy(b):
            row0 = pl.multiple_of(b * ROW_BLK, ROW_BLK)
            return pltpu.make_async_copy(zbuf, xs_hbm.at[_slab_rows(row0, ROW_BLK), :], sem)

        def ustart(b, _):
            unused_copy(b).start()
            return 0

        def uwait(b, _):
            unused_copy(b).wait()
            return 0

        n_used = ends_ref[N_EXPERTS - 1] // ROW_BLK
        lax.fori_loop(0, N_EXPERTS, zstart, 0)
        lax.fori_loop(n_used, N_BLKS, ustart, 0)
        lax.fori_loop(0, N_EXPERTS, zwait, 0)
        lax.fori_loop(n_used, N_BLKS, uwait, 0)

    base = i * TT_DISP

    def row_copy(t, k):
        d = dest_ref[k, base + t]
        return pltpu.make_async_copy(hf_ref.at[_slab_rows(t, 1), :], xs_hbm.at[_slab_rows(d, 1), :], sem)

    def issue(t, _):
        for k in range(TOP_K):
            row_copy(t, k).start(priority=k)
        return 0

    def drain(t, _):
        for k in range(TOP_K):
            row_copy(t, k).wait()
        return 0

    lax.fori_loop(0, TT_DISP, issue, 0, unroll=ISSUE_UNROLL)
    lax.fori_loop(0, TT_DISP, drain, 0, unroll=ISSUE_UNROLL)


def _dispatch(ends, dest, hf):
    return pl.pallas_call(
        _dispatch_kernel,
        grid_spec=pltpu.PrefetchScalarGridSpec(
            num_scalar_prefetch=2,
            grid=(N_TOK // TT_DISP,),
            in_specs=[pl.BlockSpec((TT_DISP * SLAB, LANES), lambda i, e, d: (i, 0))],
            out_specs=pl.BlockSpec(memory_space=pl.ANY),
            scratch_shapes=[pltpu.VMEM((ROW_BLK * SLAB, LANES), f32),
                            pltpu.SemaphoreType.DMA(())]),
        out_shape=jax.ShapeDtypeStruct((N_ROWS * SLAB, LANES), f32),
        compiler_params=_cparams(("arbitrary",)),
        name="moe_dispatch",
    )(ends, dest, hf)


W_RING = 3
X_RING = 6
Y_RING = 3


def _ffn_kernel(ends_ref, wg_hbm, wu_hbm, wd_hbm, xs_hbm, y_hbm,
                wg_f, wu_f, wd_f, wg_s, wu_s, wd_s, xbuf, ybuf, sem_w, sem_x, sem_y):
    n_used = ends_ref[N_EXPERTS - 1] // ROW_BLK

    def w_copies(ex):
        slot = ex % W_RING
        return [pltpu.make_async_copy(hbm.at[ex], buf.at[slot], sem_w.at[i, slot])
                for i, (hbm, buf) in enumerate(((wg_hbm, wg_f), (wu_hbm, wu_f), (wd_hbm, wd_f)))]

    def advance(e_from, e_to):
        def one(ex, _):
            for cp in w_copies(ex):
                cp.wait()

            @pl.when(ex + (W_RING - 1) < N_EXPERTS)
            def _():
                for cp in w_copies(ex + (W_RING - 1)):
                    cp.start()
            return 0

        lax.fori_loop(e_from + 1, e_to + 1, one, 0)

    def x_copy(b):
        slot = b % X_RING
        return pltpu.make_async_copy(xs_hbm.at[_slab_rows(b * ROW_BLK, ROW_BLK), :], xbuf.at[slot],
                                     sem_x.at[slot])

    def y_copy(b):
        slot = b % Y_RING
        return pltpu.make_async_copy(ybuf.at[slot], y_hbm.at[_slab_rows(b * ROW_BLK, ROW_BLK), :],
                                     sem_y.at[slot])

    for ex in range(W_RING - 1):
        for cp in w_copies(ex):
            cp.start()
    for b0 in range(X_RING - 1):
        @pl.when(b0 < n_used)
        def _():
            x_copy(b0).start()

    def body(b, e):
        @pl.when(b + (X_RING - 1) < n_used)
        def _():
            x_copy(b + (X_RING - 1)).start()

        e_new = lax.while_loop(lambda ex: b * ROW_BLK >= ends_ref[ex], lambda ex: ex + 1,
                               jnp.maximum(e, 0))

        @pl.when(e_new != e)
        def _():
            advance(e, e_new)
            wslot = e_new % W_RING
            wg_s[...] = wg_f[wslot].astype(bf16)
            wu_s[...] = wu_f[wslot].astype(bf16)
            wd_s[...] = wd_f[wslot].astype(bf16)

        x_copy(b).wait()

        @pl.when(b >= Y_RING)
        def _():
            y_copy(b - Y_RING).wait()

        x = _slab_load(xbuf, (b % X_RING,), ROW_BLK).astype(bf16)
        h1 = jnp.dot(x, wg_s[...], preferred_element_type=f32)
        h2 = jnp.dot(x, wu_s[...], preferred_element_type=f32)
        a = (h1 * jax.nn.sigmoid(h1) * h2).astype(bf16)
        _slab_store(ybuf, (b % Y_RING,), jnp.dot(a, wd_s[...], preferred_element_type=f32))
        y_copy(b).start()
        return e_new

    e_last = lax.fori_loop(0, n_used, body, jnp.int32(-1))
    advance(e_last, jnp.int32(N_EXPERTS - 1))

    for back in range(Y_RING, 0, -1):
        @pl.when(n_used >= back)
        def _():
            y_copy(n_used - back).wait()

    ybuf[0] = jnp.zeros((ROW_BLK * SLAB, LANES), f32)

    def unused_copy(b):
        return pltpu.make_async_copy(ybuf.at[0], y_hbm.at[_slab_rows(b * ROW_BLK, ROW_BLK), :],
                                     sem_y.at[0])

    def ustart(b, _):
        unused_copy(b).start()
        return 0

    def uwait(b, _):
        unused_copy(b).wait()
        return 0

    lax.fori_loop(n_used, N_BLKS, ustart, 0)
    lax.fori_loop(n_used, N_BLKS, uwait, 0)


def _expert_ffn(ends, xs, wg, wu, wd):
    return pl.pallas_call(
        _ffn_kernel,
        grid_spec=pltpu.PrefetchScalarGridSpec(
            num_scalar_prefetch=1,
            grid=(1,),
            in_specs=[pl.BlockSpec(memory_space=pl.ANY)] * 4,
            out_specs=pl.BlockSpec(memory_space=pl.ANY),
            scratch_shapes=[pltpu.VMEM((W_RING, D_MODEL, EXPERT_FF), f32),
                            pltpu.VMEM((W_RING, D_MODEL, EXPERT_FF), f32),
                            pltpu.VMEM((W_RING, EXPERT_FF, D_MODEL), f32),
                            pltpu.VMEM((D_MODEL, EXPERT_FF), bf16),
                            pltpu.VMEM((D_MODEL, EXPERT_FF), bf16),
                            pltpu.VMEM((EXPERT_FF, D_MODEL), bf16),
                            pltpu.VMEM((X_RING, ROW_BLK * SLAB, LANES), f32),
                            pltpu.VMEM((Y_RING, ROW_BLK * SLAB, LANES), f32),
                            pltpu.SemaphoreType.DMA((3, W_RING)),
                            pltpu.SemaphoreType.DMA((X_RING,)),
                            pltpu.SemaphoreType.DMA((Y_RING,))]),
        out_shape=jax.ShapeDtypeStruct((N_ROWS * SLAB, LANES), f32),
        compiler_params=_cparams(("arbitrary",)),
        name="expert_ffn",
    )(ends, wg, wu, wd, xs)


def _combine_kernel(dest_ref, x_ref, cw_ref, g_ref, y_hbm, o_ref, ybuf, sem):
    i = pl.program_id(0)
    n_steps = pl.num_programs(0)

    def row_copy(step, t, k):
        slot = step % 2
        d = dest_ref[k, step * TM_COMB + t]
        return pltpu.make_async_copy(y_hbm.at[_slab_rows(d, 1), :],
                                     ybuf.at[slot, k, _slab_rows(t, 1), :], sem.at[slot])

    def issue(step):
        def one(t, _):
            for k in range(TOP_K):
                row_copy(step, t, k).start()
            return 0
        lax.fori_loop(0, TM_COMB, one, 0, unroll=ISSUE_UNROLL)

    def drain(step):
        def one(t, _):
            for k in range(TOP_K):
                row_copy(step, t, k).wait()
            return 0
        lax.fori_loop(0, TM_COMB, one, 0, unroll=ISSUE_UNROLL)

    @pl.when(i == 0)
    def _():
        issue(i)

    @pl.when(i + 1 < n_steps)
    def _():
        issue(i + 1)

    drain(i)
    slot = i % 2
    w = cw_ref[...]
    x = (x_ref[...] + w[:, 0:1] * _slab_load(ybuf, (slot, 0), TM_COMB)
         + w[:, 1:2] * _slab_load(ybuf, (slot, 1), TM_COMB))
    o_ref[...] = _rms(x, g_ref[...])


def _combine(dest, x1, cw_t, g, y):
    tm = TM_COMB
    return pl.pallas_call(
        _combine_kernel,
        grid_spec=pltpu.PrefetchScalarGridSpec(
            num_scalar_prefetch=1,
            grid=(N_TOK // tm,),
            in_specs=[pl.BlockSpec((tm, D_MODEL), lambda i, d: (i, 0)),
                      pl.BlockSpec((tm, TOP_K), lambda i, d: (i, 0)),
                      pl.BlockSpec((1, D_MODEL), lambda i, d: (0, 0)),
                      pl.BlockSpec(memory_space=pl.ANY)],
            out_specs=pl.BlockSpec((tm, D_MODEL), lambda i, d: (i, 0)),
            scratch_shapes=[pltpu.VMEM((2, TOP_K, tm * SLAB, LANES), f32),
                            pltpu.SemaphoreType.DMA((2,))]),
        out_shape=jax.ShapeDtypeStruct((N_TOK, D_MODEL), f32),
        compiler_params=_cparams(("arbitrary",)),
        name="moe_combine",
    )(dest, x1, cw_t, g, y)


TR_RELAYOUT = 256
KR_TILE = COL_KR // TR_RELAYOUT


def _w_in_relayout_kernel(w_ref, o_ref):
    i = pl.program_id(0)

    @pl.when(i < COL_V // TR_RELAYOUT)
    def _():
        q = ROPE_NA
        for g in range(TR_RELAYOUT // LANES):
            r = g * LANES
            o_ref[r:r + q, :] = w_ref[r:r + q, :].astype(bf16)
            o_ref[r + q:r + 2 * q, :] = w_ref[r + 2 * q:r + 3 * q, :].astype(bf16)
            o_ref[r + 2 * q:r + 3 * q, :] = w_ref[r + q:r + 2 * q, :].astype(bf16)
            o_ref[r + 3 * q:r + 4 * q, :] = w_ref[r + 3 * q:r + 4 * q, :].astype(bf16)

    @pl.when((i >= COL_V // TR_RELAYOUT) & (i != KR_TILE))
    def _():
        o_ref[...] = w_ref[...].astype(bf16)

    @pl.when(i == KR_TILE)
    def _():
        o_ref[...] = jnp.zeros_like(o_ref)
        o_ref[KR_LANE:KR_LANE + MLA_ROPE, :] = w_ref[0:MLA_ROPE, :].astype(bf16)


def _prep_w_in(w_in_t):
    n_src, d = w_in_t.shape
    gate_src = COL_KR + MLA_ROPE

    gate_tile = COL_GA // TR_RELAYOUT
    sub = 8
    shift = (COL_GA - gate_src) // sub

    def src_row(i):
        return ((TR_RELAYOUT // sub) * i - shift * (i // gate_tile)) * sub

    return pl.pallas_call(
        _w_in_relayout_kernel,
        grid=(PROJ_W // TR_RELAYOUT,),
        in_specs=[pl.BlockSpec((pl.Element(TR_RELAYOUT), pl.Element(d)), lambda i: (src_row(i), 0))],
        out_specs=pl.BlockSpec((TR_RELAYOUT, d), lambda i: (i, 0)),
        out_shape=jax.ShapeDtypeStruct((PROJ_W, d), bf16),
        compiler_params=_cparams(("arbitrary",)),
        name="w_in_relayout",
    )(w_in_t)


def _prep_w_uq(w_uq):
    w = w_uq.reshape(MLA_Q_RANK, HEADS, MLA_NOPE + MLA_ROPE)
    w = jnp.pad(w, ((0, 0), (0, 0), (0, HEAD_W - MLA_NOPE - MLA_ROPE)))
    return w.reshape(MLA_Q_RANK, HEADS * HEAD_W).astype(bf16)


def _prep_w_ukv(w_ukv):
    w = w_ukv.reshape(MLA_KV_RANK, HEADS, MLA_NOPE + HEAD_W)
    wk = jnp.pad(w[:, :, :MLA_NOPE], ((0, 0), (0, 0), (0, HEAD_W - MLA_NOPE)))
    wv = w[:, :, MLA_NOPE:]
    return jnp.concatenate([wk.reshape(MLA_KV_RANK, -1), wv.reshape(MLA_KV_RANK, -1)], axis=1).astype(bf16)


def _prep_router(w_group, b_group, w_router, b_router):
    z = lambda n: jnp.zeros((D_MODEL, n), f32)
    w = jnp.concatenate([w_group, z(RT_EXPERT_ROW - N_GROUPS), w_router,
                         z(LANES - RT_EXPERT_ROW - N_EXPERTS)], axis=1)
    b = jnp.concatenate([b_group, jnp.full((RT_EXPERT_ROW - N_GROUPS,), NEG_INF, f32), b_router,
                         jnp.zeros((LANES - RT_EXPERT_ROW - N_EXPERTS,), f32)])[None, :]
    w_hi = w.astype(bf16)
    w_lo = (w - w_hi.astype(f32)).astype(bf16)
    return jnp.concatenate([w_hi, w_lo], axis=1), b


def kernel(x, positions, attn_norm, w_in, da_lambda, da_subln, mla_q_norm, mla_w_uq, mla_kv_norm, mla_w_ukv, w_branch_a, w_branch_b, w_out, ffn_norm, w_group, b_group, w_router, b_router, w_exp_gate, w_exp_up, w_exp_down, final_norm):
    x2d = x.reshape(N_TOK, D_MODEL)
    cos_t, sin_t = _rope_tables(positions)

    proj = _in_proj(x2d, attn_norm[0][None, :], _prep_w_in(w_in[0].T), cos_t, sin_t)
    qb, kb, vb = _mla_prep(proj, mla_q_norm[0][None, :], mla_kv_norm[0][None, :],
                           _prep_w_uq(mla_w_uq[0]), _prep_w_ukv(mla_w_ukv[0]), cos_t, sin_t)
    oa = _attention(proj, proj, proj, COL_Q // HEAD_W, COL_K // HEAD_W, COL_V // HEAD_W,
                    heads=8, qtiles=2, extra=(da_lambda[0], da_subln[0][:, None]))
    ob = _attention(qb, kb, vb, 0, 0, 0, heads=8, qtiles=4)
    w_rt, b_rt = _prep_router(w_group[0], b_group[0], w_router[0], b_router[0])
    x1, hf, eid, cw = _merge(oa, ob, proj, x2d, w_branch_a[0].astype(bf16), w_branch_b[0].astype(bf16),
                             w_out[0].astype(bf16), ffn_norm[0][None, :], w_rt, b_rt)

    dest, ends = _plan(eid)
    dest = dest.reshape(TOP_K, N_TOK)
    ends = ends[0, :N_EXPERTS]
    xs = _dispatch(ends, dest, hf)
    y = _expert_ffn(ends, xs, w_exp_gate[0], w_exp_up[0], w_exp_down[0])
    out = _combine(dest, x1, cw.T, final_norm[None, :], y)
    return out.reshape(BATCH, SEQ, D_MODEL)
```

```python
import functools
import math

import jax
import jax.numpy as jnp
from jax import lax
from jax.experimental import pallas as pl
from jax.experimental.pallas import tpu as pltpu

f32 = jnp.float32
bf16 = jnp.bfloat16
i32 = jnp.int32

D_MODEL = 2048
BATCH = 4
SEQ = 2048
N_TOK = BATCH * SEQ
HEADS = 8
HEAD_W = 128
DA_HALF = 64
MLA_Q_RANK = 512
MLA_KV_RANK = 256
MLA_NOPE = 64
MLA_ROPE = 32
N_GROUPS = 4
EXPERTS_PER_GROUP = 8
N_EXPERTS = N_GROUPS * EXPERTS_PER_GROUP
TOP_K = 2
EXPERT_FF = 512
ROPE_THETA = 10000.0
NORM_EPS = 1e-6
NEG_INF = -1e30
LAM_INIT = 0.8 - 0.6 * math.exp(-0.3 * 0)
LOG2E = math.log2(math.e)

LANES = 128
VMEM_LIMIT = 56 * 1024 * 1024

COL_Q, COL_K, COL_V = 0, 1024, 2048
COL_CQ, COL_CKV = 3072, 3584
COL_KR = 3840
COL_GA, COL_GB = 4096, 6144
PROJ_W = 8192
KR_LANE = MLA_NOPE

TM_PROJ, TN_PROJ = 1024, 1024
TM_PREP = 512
TQ = 256
TM_MERGE = 256
ROW_BLK = 128
N_ROWS = (N_TOK * TOP_K // ROW_BLK + N_EXPERTS) * ROW_BLK
N_BLKS = N_ROWS // ROW_BLK
TT_DISP = 1024
TM_COMB = 128

Q_SCALE_A = (DA_HALF ** -0.5) * LOG2E
Q_SCALE_B = ((MLA_NOPE + MLA_ROPE) ** -0.5) * LOG2E


def _cparams(sem, **kw):
    return pltpu.CompilerParams(dimension_semantics=sem, vmem_limit_bytes=VMEM_LIMIT, **kw)


def _rms(x, g):
    return x * lax.rsqrt(jnp.mean(x * x, axis=-1, keepdims=True) + NORM_EPS) * g


SLAB = D_MODEL // LANES


ISSUE_UNROLL = 16


def _slab_rows(row0, nrows):
    return pl.ds(pl.multiple_of(row0 * SLAB, SLAB), nrows * SLAB)


def _slab_store(ref, idx, val):
    rows = val.shape[0]
    for j in range(SLAB):
        ref[idx + (pl.ds(j, rows, stride=SLAB), slice(None))] = val[:, j * LANES:(j + 1) * LANES]


def _slab_chunk(ref, idx, rows, j):
    return ref[idx + (pl.ds(j, rows, stride=SLAB), slice(None))]


def _slab_load(ref, idx, rows):
    return jnp.concatenate([_slab_chunk(ref, idx, rows, j) for j in range(SLAB)], axis=1)


def _rope_lanes(blk, cos, sin, first_half, half):
    up = pltpu.roll(blk, LANES - half, 1)
    down = pltpu.roll(blk, half, 1)
    return blk * cos + jnp.where(first_half, up, down) * sin


ROPE_NA = DA_HALF // 2
ROPE_NB = MLA_ROPE // 2


def _rope_tables_kernel(pos_ref, inv_ref, cos_ref, sin_ref):
    half = pos_ref.shape[0] // 2
    lane = lax.broadcasted_iota(i32, (half, LANES), 1)
    pos = jnp.where(lane < LANES // 2, pos_ref[:half, :], pos_ref[half:, :]).astype(f32)
    ang = pos * inv_ref[...]
    c = jnp.cos(ang)
    s = jnp.sin(ang)
    one = lambda n: jnp.ones((half, n), f32)
    zero = lambda n: jnp.zeros((half, n), f32)
    tail = LANES - KR_LANE - MLA_ROPE
    for part in range(2):
        o = part * (LANES // 2)
        rows = slice(part * half, (part + 1) * half)
        ca, sa = c[:, o:o + ROPE_NA], s[:, o:o + ROPE_NA]
        cb, sb = c[:, o + ROPE_NA:o + ROPE_NA + ROPE_NB], s[:, o + ROPE_NA:o + ROPE_NA + ROPE_NB]
        cos_ref[rows, :] = jnp.concatenate([ca, ca, ca, ca, one(KR_LANE), cb, cb, one(tail)], axis=1)
        sin_ref[rows, :] = jnp.concatenate([-sa, -sa, sa, sa, zero(KR_LANE), -sb, sb, zero(tail)],
                                           axis=1)


def _rope_tables(positions):
    inv_a = ROPE_THETA ** (-jnp.arange(0, DA_HALF, 2, dtype=f32) / DA_HALF)
    inv_b = ROPE_THETA ** (-jnp.arange(0, MLA_ROPE, 2, dtype=f32) / MLA_ROPE)
    inv = jnp.concatenate([inv_a, inv_b, jnp.zeros((LANES // 2 - ROPE_NA - ROPE_NB,), f32)])
    inv = jnp.tile(inv, 2)[None, :]
    pos = positions.reshape(N_TOK, 1)
    tm = 1024
    return pl.pallas_call(
        _rope_tables_kernel,
        grid=(N_TOK // tm,),
        in_specs=[pl.BlockSpec((tm, 1), lambda i: (i, 0)),
                  pl.BlockSpec((1, LANES), lambda i: (0, 0))],
        out_specs=[pl.BlockSpec((tm, 2 * LANES), lambda i: (i, 0)),
                   pl.BlockSpec((tm, 2 * LANES), lambda i: (i, 0))],
        out_shape=[jax.ShapeDtypeStruct((N_TOK, 2 * LANES), f32)] * 2,
        compiler_params=_cparams(("arbitrary",)),
        name="rope_tables",
    )(pos, inv)


def _in_proj_kernel(x_ref, g_ref, w_ref, cos_ref, sin_ref, o_ref, h_scr):
    n = pl.program_id(1)

    @pl.when(n == 0)
    def _():
        h_scr[...] = _rms(x_ref[...], g_ref[...]).astype(bf16)

    acc = lax.dot_general(h_scr[...], w_ref[...], (((1,), (1,)), ((), ())),
                          preferred_element_type=f32)
    n_rope = COL_V // TN_PROJ
    n_q = COL_K // TN_PROJ

    @pl.when(n < n_rope)
    def _():
        c = jnp.where(n < n_q, Q_SCALE_A, 1.0).astype(f32)
        cos = cos_ref[...] * c
        sin = sin_ref[...] * c
        for j in range(TN_PROJ // LANES):
            blk = acc[:, j * LANES:(j + 1) * LANES]
            rot = pltpu.roll(blk, LANES // 2, 1)
            o_ref[:, j * LANES:(j + 1) * LANES] = (blk * cos + rot * sin).astype(bf16)

    @pl.when(n >= n_rope)
    def _():
        o_ref[...] = acc.astype(bf16)


def _in_proj(x2d, g, w_pad, cos_t, sin_t):
    return pl.pallas_call(
        _in_proj_kernel,
        grid=(N_TOK // TM_PROJ, PROJ_W // TN_PROJ),
        in_specs=[pl.BlockSpec((TM_PROJ, D_MODEL), lambda i, n: (i, 0)),
                  pl.BlockSpec((1, D_MODEL), lambda i, n: (0, 0)),
                  pl.BlockSpec((TN_PROJ, D_MODEL), lambda i, n: (n, 0)),
                  pl.BlockSpec((TM_PROJ, LANES), lambda i, n: (i, 0)),
                  pl.BlockSpec((TM_PROJ, LANES), lambda i, n: (i, 0))],
        out_specs=pl.BlockSpec((TM_PROJ, TN_PROJ), lambda i, n: (i, n)),
        out_shape=jax.ShapeDtypeStruct((N_TOK, PROJ_W), bf16),
        scratch_shapes=[pltpu.VMEM((TM_PROJ, D_MODEL), bf16)],
        compiler_params=_cparams(("arbitrary", "arbitrary")),
        name="in_proj",
    )(x2d, g, w_pad, cos_t, sin_t)


def _mla_prep_kernel(cq_ref, ckv_ref, kr_ref, gq_ref, gkv_ref, wuq_ref, wkv_ref, cos_ref, sin_ref,
                     q_ref, k_ref, v_ref):
    cqn = _rms(cq_ref[...].astype(f32), gq_ref[...]).astype(bf16)
    q = jnp.dot(cqn, wuq_ref[...], preferred_element_type=f32)
    ckvn = _rms(ckv_ref[...].astype(f32), gkv_ref[...]).astype(bf16)
    kv = jnp.dot(ckvn, wkv_ref[...], preferred_element_type=f32)
    cos = cos_ref[...]
    sin = sin_ref[...]
    lane = lax.broadcasted_iota(i32, (TM_PREP, LANES), 1)
    first = lane < (KR_LANE + MLA_ROPE // 2)
    rope = lambda b: _rope_lanes(b, cos, sin, first, MLA_ROPE // 2)
    k_rot = rope(kr_ref[...].astype(f32))
    for h in range(HEADS):
        sl = slice(h * HEAD_W, (h + 1) * HEAD_W)
        q_ref[:, sl] = (rope(q[:, sl]) * Q_SCALE_B).astype(bf16)
        k_ref[:, sl] = (kv[:, sl] + k_rot).astype(bf16)
    v_ref[...] = kv[:, HEADS * HEAD_W:].astype(bf16)


def _mla_prep(proj, gq, gkv, wuq_p, wkv_p, cos_t, sin_t):
    tm = TM_PREP
    w = HEADS * HEAD_W
    return pl.pallas_call(
        _mla_prep_kernel,
        grid=(N_TOK // tm,),
        in_specs=[pl.BlockSpec((tm, MLA_Q_RANK), lambda i: (i, COL_CQ // MLA_Q_RANK)),
                  pl.BlockSpec((tm, MLA_KV_RANK), lambda i: (i, COL_CKV // MLA_KV_RANK)),
                  pl.BlockSpec((tm, LANES), lambda i: (i, COL_KR // LANES)),
                  pl.BlockSpec((1, MLA_Q_RANK), lambda i: (0, 0)),
                  pl.BlockSpec((1, MLA_KV_RANK), lambda i: (0, 0)),
                  pl.BlockSpec((MLA_Q_RANK, w), lambda i: (0, 0)),
                  pl.BlockSpec((MLA_KV_RANK, 2 * w), lambda i: (0, 0)),
                  pl.BlockSpec((tm, LANES), lambda i: (i, 1)),
                  pl.BlockSpec((tm, LANES), lambda i: (i, 1))],
        out_specs=[pl.BlockSpec((tm, w), lambda i: (i, 0))] * 3,
        out_shape=[jax.ShapeDtypeStruct((N_TOK, w), bf16)] * 3,
        compiler_params=_cparams(("arbitrary",)),
        name="mla_prep",
    )(proj, proj, proj, gq, gkv, wuq_p, wkv_p, cos_t, sin_t)


VT_ROWS = HEAD_W + 16


def _attn_kernel(*refs, ncomp, heads, qtiles):
    if ncomp == 2:
        q_ref, k_ref, v_ref, lam_ref, g_ref, o_ref, acc_scr, vt_scr = refs
    else:
        q_ref, k_ref, v_ref, o_ref, acc_scr, vt_scr = refs
    qi = pl.program_id(2)

    @pl.when(qi == 0)
    def _():
        for j in range(SEQ // TQ):
            for h in range(heads):
                vt_scr[j, h, :HEAD_W, :] = v_ref[j * TQ:(j + 1) * TQ, h * HEAD_W:(h + 1) * HEAD_W].T
                vt_scr[j, h, HEAD_W:, :] = jnp.ones((VT_ROWS - HEAD_W, TQ), bf16)

    lane = lax.broadcasted_iota(i32, (TQ, HEAD_W), 1)
    qs = []
    for t in range(qtiles):
        for h in range(heads):
            q = q_ref[t * TQ:(t + 1) * TQ, h * HEAD_W:(h + 1) * HEAD_W]
            if ncomp == 2:
                zero = jnp.zeros_like(q)
                in_a = (lane % DA_HALF) < (DA_HALF // 2)
                qs += [jnp.where(in_a, q, zero), jnp.where(in_a, zero, q)]
            else:
                qs.append(q)
    per_tile = heads * ncomp
    head_of = lambda n: (n % per_tile) // ncomp

    def step(j, carry, modes):
        off = pl.multiple_of(j * TQ, TQ)
        live = [n for n in range(len(qs)) if modes[n // per_tile] != "skip"]
        ss = {}
        for n in live:
            h = head_of(n)
            k = k_ref[pl.ds(off, TQ), h * HEAD_W:(h + 1) * HEAD_W]
            s = lax.dot_general(k, qs[n], (((1,), (1,)), ((), ())), preferred_element_type=f32)
            if modes[n // per_tile] == "diag":
                key = lax.broadcasted_iota(i32, (TQ, TQ), 0)
                qry = lax.broadcasted_iota(i32, (TQ, TQ), 1)
                s = jnp.where(key <= qry, s, NEG_INF)
            ss[n] = s
        stats = {}
        for n in live:
            m, l = carry[n]
            s = ss[n]
            m_new = jnp.maximum(m, jnp.max(s, axis=0, keepdims=True))
            alpha = jnp.exp2(m - m_new)
            p = jnp.exp2(s - m_new)
            stats[n] = (m_new, alpha * l, alpha, p.astype(bf16))
        out = list(carry)
        for n in live:
            m_new, l_scaled, alpha, p = stats[n]
            pv = jnp.dot(vt_scr[j, head_of(n)], p, preferred_element_type=f32)
            acc_scr[n] = alpha * acc_scr[n] + pv[:HEAD_W, :]
            out[n] = (m_new, l_scaled + pv[HEAD_W:HEAD_W + 1, :])
        return tuple(out)

    acc_scr[...] = jnp.zeros_like(acc_scr)
    init = tuple((jnp.full((1, TQ), NEG_INF, f32), jnp.zeros((1, TQ), f32)) for _ in qs)
    first = qi * qtiles
    carry = lax.fori_loop(0, first, lambda j, c: step(j, c, ("full",) * qtiles), init)
    for u in range(qtiles):
        modes = tuple("skip" if t < u else ("diag" if t == u else "full") for t in range(qtiles))
        carry = step(first + u, carry, modes)
    carry = tuple((m, l, acc_scr[n]) for n, (m, l) in enumerate(carry))

    if ncomp == 2:
        ll = lam_ref[...]
        s1 = jnp.sum(ll[0:1, :] * ll[1:2, :], axis=-1, keepdims=True)
        s2 = jnp.sum(ll[2:3, :] * ll[3:4, :], axis=-1, keepdims=True)
        lam = jnp.exp(s1) - jnp.exp(s2) + LAM_INIT
    for t in range(qtiles):
        for h in range(heads):
            n0 = (t * heads + h) * ncomp
            if ncomp == 2:
                (_, l0, a0), (_, l1, a1) = carry[n0], carry[n0 + 1]
                o = a0 / l0 - lam * (a1 / l1)
                ms = jnp.mean(o * o, axis=0, keepdims=True)
                o = o * lax.rsqrt(ms + NORM_EPS) * g_ref[...] * (1.0 - LAM_INIT)
            else:
                _, l0, a0 = carry[n0]
                o = a0 / l0
            o_ref[t * TQ:(t + 1) * TQ, h * HEAD_W:(h + 1) * HEAD_W] = o.T.astype(bf16)


def _attention(q_arr, k_arr, v_arr, q_blk0, k_blk0, v_blk0, heads, qtiles, extra=()):
    ncomp = 2 if extra else 1
    tq = qtiles * TQ
    nq = SEQ // tq
    w = heads * HEAD_W
    qb, kb, vb = q_blk0 // heads, k_blk0 // heads, v_blk0 // heads
    in_specs = [pl.BlockSpec((tq, w), lambda b, h, i: (b * nq + i, qb + h)),
                pl.BlockSpec((SEQ, w), lambda b, h, i: (b, kb + h)),
                pl.BlockSpec((SEQ, w), lambda b, h, i: (b, vb + h))]
    for a in extra:
        in_specs.append(pl.BlockSpec(a.shape, lambda b, h, i: (0, 0)))
    return pl.pallas_call(
        functools.partial(_attn_kernel, ncomp=ncomp, heads=heads, qtiles=qtiles),
        grid=(BATCH, HEADS // heads, nq),
        in_specs=in_specs,
        out_specs=pl.BlockSpec((tq, w), lambda b, h, i: (b * nq + i, h)),
        out_shape=jax.ShapeDtypeStruct((N_TOK, HEADS * HEAD_W), bf16),
        scratch_shapes=[pltpu.VMEM((qtiles * heads * ncomp, HEAD_W, TQ), f32),
                        pltpu.VMEM((SEQ // TQ, heads, VT_ROWS, TQ), bf16)],
        compiler_params=_cparams(("arbitrary", "arbitrary", "arbitrary")),
        name="diff_attn" if extra else "mla_attn",
    )(q_arr, k_arr, v_arr, *extra)


RT_GROUP_ROW = 0
RT_EXPERT_ROW = 8


def _route(hf, w_ref, b_ref, eid_ref, cw_ref):
    hf_hi = hf.astype(bf16)
    hf_lo = (hf - hf_hi.astype(f32)).astype(bf16)
    a = jnp.dot(hf_hi, w_ref[...], preferred_element_type=f32)
    b = jnp.dot(hf_lo, w_ref[:, :LANES], preferred_element_type=f32)
    logits = a[:, :LANES] + (a[:, LANES:] + b) + b_ref[...]
    lt = logits.T
    tm = lt.shape[1]
    ridx = lax.broadcasted_iota(i32, (EXPERTS_PER_GROUP, tm), 0).astype(f32)

    def top1(v):
        vmax = jnp.max(v, axis=0, keepdims=True)
        idx = jnp.min(jnp.where(v == vmax, ridx, float(EXPERTS_PER_GROUP)), axis=0, keepdims=True)
        return vmax, idx

    gl = lt[RT_GROUP_ROW:RT_GROUP_ROW + 8, :]
    gmax, gsel = top1(gl)
    p_g = 1.0 / jnp.sum(jnp.exp(gl - gmax), axis=0, keepdims=True)
    el = jnp.zeros((EXPERTS_PER_GROUP, tm), f32)
    for g in range(N_GROUPS):
        r0 = RT_EXPERT_ROW + g * EXPERTS_PER_GROUP
        el = jnp.where(gsel == float(g), lt[r0:r0 + EXPERTS_PER_GROUP, :], el)
    e1, i1 = top1(el)
    e2, i2 = top1(jnp.where(ridx == i1, -jnp.inf, el))
    r = jnp.exp(e2 - e1)
    w1 = 1.0 / (1.0 + r)
    base = gsel * EXPERTS_PER_GROUP
    eid_ref[0:1, :] = (base + i1).astype(i32)
    eid_ref[1:2, :] = (base + i2).astype(i32)
    cw_ref[0:1, :] = p_g * w1
    cw_ref[1:2, :] = p_g * (r * w1)


def _merge_kernel(oa_ref, ob_ref, ga_ref, gb_ref, x_ref, wa_ref, wb_ref, wo_ref, gf_ref, wr_ref, br_ref,
                  x1_ref, hf_ref, eid_ref, cw_ref):
    ya = jnp.dot(oa_ref[...], wa_ref[...], preferred_element_type=f32)
    yb = jnp.dot(ob_ref[...], wb_ref[...], preferred_element_type=f32)
    mixed = (jax.nn.sigmoid(ga_ref[...].astype(f32)) * ya
             + jax.nn.sigmoid(gb_ref[...].astype(f32)) * yb)
    x1 = x_ref[...] + jnp.dot(mixed.astype(bf16), wo_ref[...], preferred_element_type=f32)
    x1_ref[...] = x1
    hf = _rms(x1, gf_ref[...])
    _slab_store(hf_ref, (), hf)
    _route(hf, wr_ref, br_ref, eid_ref, cw_ref)


def _merge(oa, ob, proj, x2d, wa, wb, wo, gf, w_rt, b_rt):
    tm = TM_MERGE
    w = HEADS * HEAD_W
    const = lambda shape: pl.BlockSpec(shape, lambda i: (0, 0), pipeline_mode=pl.Buffered(1))
    return pl.pallas_call(
        _merge_kernel,
        grid=(N_TOK // tm,),
        in_specs=[pl.BlockSpec((tm, w), lambda i: (i, 0)),
                  pl.BlockSpec((tm, w), lambda i: (i, 0)),
                  pl.BlockSpec((tm, D_MODEL), lambda i: (i, COL_GA // D_MODEL)),
                  pl.BlockSpec((tm, D_MODEL), lambda i: (i, COL_GB // D_MODEL)),
                  pl.BlockSpec((tm, D_MODEL), lambda i: (i, 0)),
                  const((w, D_MODEL)), const((w, D_MODEL)), const((D_MODEL, D_MODEL)),
                  const((1, D_MODEL)), const((D_MODEL, 2 * LANES)), const((1, LANES))],
        out_specs=[pl.BlockSpec((tm, D_MODEL), lambda i: (i, 0)),
                   pl.BlockSpec((tm * SLAB, LANES), lambda i: (i, 0)),
                   pl.BlockSpec((TOP_K, tm), lambda i: (0, i)),
                   pl.BlockSpec((TOP_K, tm), lambda i: (0, i))],
        out_shape=[jax.ShapeDtypeStruct((N_TOK, D_MODEL), f32),
                   jax.ShapeDtypeStruct((N_TOK * SLAB, LANES), f32),
                   jax.ShapeDtypeStruct((TOP_K, N_TOK), i32),
                   jax.ShapeDtypeStruct((TOP_K, N_TOK), f32)],
        compiler_params=_cparams(("arbitrary",)),
        name="merge_route",
    )(oa, ob, proj, proj, x2d, wa, wb, wo, gf, w_rt, b_rt)


PLAN_N = 128


def _plan_kernel(eid_ref, dest_ref, ends_ref):
    eid = eid_ref[...]
    r = lax.broadcasted_iota(i32, (PLAN_N, PLAN_N), 0)
    c = lax.broadcasted_iota(i32, (PLAN_N, PLAN_N), 1)
    upper = (r < c).astype(bf16)
    lower = (c < r).astype(bf16)
    ones = jnp.ones((PLAN_N, PLAN_N), bf16)
    lane = lax.broadcasted_iota(i32, (1, LANES), 1)
    dest = jnp.zeros((PLAN_N, PLAN_N), f32)
    ends = jnp.zeros((1, LANES), f32)
    start = jnp.zeros((1, 1), f32)
    for e in range(N_EXPERTS):
        m = (eid == e).astype(f32)
        mb = m.astype(bf16)
        within = jnp.dot(mb, upper, preferred_element_type=f32)
        rowtot = jnp.dot(mb, ones, preferred_element_type=f32)
        rowpre = jnp.dot(lower, rowtot.astype(bf16), preferred_element_type=f32)
        cnt = jnp.sum(jnp.sum(m, axis=1, keepdims=True), axis=0, keepdims=True)
        dest = dest + m * (start + within + rowpre)
        start = start + jnp.floor((cnt + (ROW_BLK - 1)) * (1.0 / ROW_BLK)) * ROW_BLK
        ends = jnp.where(lane == e, start, ends)
    dest_ref[...] = dest.astype(i32)
    ends_ref[...] = ends.astype(i32)


def _plan(eid):
    whole = lambda shape: pl.BlockSpec(shape, lambda: (0, 0))
    return pl.pallas_call(
        _plan_kernel,
        in_specs=[whole((PLAN_N, PLAN_N))],
        out_specs=[whole((PLAN_N, PLAN_N)), whole((1, LANES))],
        out_shape=[jax.ShapeDtypeStruct((PLAN_N, PLAN_N), i32),
                   jax.ShapeDtypeStruct((1, LANES), i32)],
        name="moe_plan",
    )(eid.reshape(PLAN_N, PLAN_N))


def _dispatch_kernel(ends_ref, dest_ref, hf_ref, xs_hbm, zbuf, sem):
    i = pl.program_id(0)

    def tail_copy(e):
        pend = ends_ref[e]
        prev = jnp.where(e > 0, ends_ref[jnp.maximum(e - 1, 0)], 0)
        row0 = pl.multiple_of(jnp.maximum(pend - ROW_BLK, 0), ROW_BLK)
        return pend > prev, pltpu.make_async_copy(zbuf, xs_hbm.at[_slab_rows(row0, ROW_BLK), :], sem)

    @pl.when(i == 0)
    def _():
        zbuf[...] = jnp.zeros_like(zbuf)

        def zstart(e, _):
            used, cp = tail_copy(e)

            @pl.when(used)
            def _():
                cp.start()
            return 0

        def zwait(e, _):
            used, cp = tail_copy(e)

            @pl.when(used)
            def _():
                cp.wait()
            return 0

        def unused_copy(b):
            row0 = pl.multiple_of(b * ROW_BLK, ROW_BLK)
            return pltpu.make_async_copy(zbuf, xs_hbm.at[_slab_rows(row0, ROW_BLK), :], sem)

        def ustart(b, _):
            unused_copy(b).start()
            return 0

        def uwait(b, _):
            unused_copy(b).wait()
            return 0

        n_used = ends_ref[N_EXPERTS - 1] // ROW_BLK
        lax.fori_loop(0, N_EXPERTS, zstart, 0)
        lax.fori_loop(n_used, N_BLKS, ustart, 0)
        lax.fori_loop(0, N_EXPERTS, zwait, 0)
        lax.fori_loop(n_used, N_BLKS, uwait, 0)

    base = i * TT_DISP

    def row_copy(t, k):
        d = dest_ref[k, base + t]
        return pltpu.make_async_copy(hf_ref.at[_slab_rows(t, 1), :], xs_hbm.at[_slab_rows(d, 1), :], sem)

    def issue(t, _):
        for k in range(TOP_K):
            row_copy(t, k).start(priority=k)
        return 0

    def drain(t, _):
        for k in range(TOP_K):
            row_copy(t, k).wait()
        return 0

    lax.fori_loop(0, TT_DISP, issue, 0, unroll=ISSUE_UNROLL)
    lax.fori_loop(0, TT_DISP, drain, 0, unroll=ISSUE_UNROLL)


def _dispatch(ends, dest, hf):
    return pl.pallas_call(
        _dispatch_kernel,
        grid_spec=pltpu.PrefetchScalarGridSpec(
            num_scalar_prefetch=2,
            grid=(N_TOK // TT_DISP,),
            in_specs=[pl.BlockSpec((TT_DISP * SLAB, LANES), lambda i, e, d: (i, 0))],
            out_specs=pl.BlockSpec(memory_space=pl.ANY),
            scratch_shapes=[pltpu.VMEM((ROW_BLK * SLAB, LANES), f32),
                            pltpu.SemaphoreType.DMA(())]),
        out_shape=jax.ShapeDtypeStruct((N_ROWS * SLAB, LANES), f32),
        compiler_params=_cparams(("arbitrary",)),
        name="moe_dispatch",
    )(ends, dest, hf)


W_RING = 3
X_RING = 6
Y_RING = 3


def _ffn_kernel(ends_ref, wg_hbm, wu_hbm, wd_hbm, xs_hbm, y_hbm,
                wg_f, wu_f, wd_f, wg_s, wu_s, wd_s, xbuf, ybuf, sem_w, sem_x, sem_y):
    n_used = ends_ref[N_EXPERTS - 1] // ROW_BLK

    def w_copies(ex):
        slot = ex % W_RING
        return [pltpu.make_async_copy(hbm.at[ex], buf.at[slot], sem_w.at[i, slot])
                for i, (hbm, buf) in enumerate(((wg_hbm, wg_f), (wu_hbm, wu_f), (wd_hbm, wd_f)))]

    def advance(e_from, e_to):
        def one(ex, _):
            for cp in w_copies(ex):
                cp.wait()

            @pl.when(ex + (W_RING - 1) < N_EXPERTS)
            def _():
                for cp in w_copies(ex + (W_RING - 1)):
                    cp.start()
            return 0

        lax.fori_loop(e_from + 1, e_to + 1, one, 0)

    def x_copy(b):
        slot = b % X_RING
        return pltpu.make_async_copy(xs_hbm.at[_slab_rows(b * ROW_BLK, ROW_BLK), :], xbuf.at[slot],
                                     sem_x.at[slot])

    def y_copy(b):
        slot = b % Y_RING
        return pltpu.make_async_copy(ybuf.at[slot], y_hbm.at[_slab_rows(b * ROW_BLK, ROW_BLK), :],
                                     sem_y.at[slot])

    for ex in range(W_RING - 1):
        for cp in w_copies(ex):
            cp.start()
    for b0 in range(X_RING - 1):
        @pl.when(b0 < n_used)
        def _():
            x_copy(b0).start()

    def body(b, e):
        @pl.when(b + (X_RING - 1) < n_used)
        def _():
            x_copy(b + (X_RING - 1)).start()

        e_new = lax.while_loop(lambda ex: b * ROW_BLK >= ends_ref[ex], lambda ex: ex + 1,
                               jnp.maximum(e, 0))

        @pl.when(e_new != e)
        def _():
            advance(e, e_new)
            wslot = e_new % W_RING
            wg_s[...] = wg_f[wslot].astype(bf16)
            wu_s[...] = wu_f[wslot].astype(bf16)
            wd_s[...] = wd_f[wslot].astype(bf16)

        x_copy(b).wait()

        @pl.when(b >= Y_RING)
        def _():
            y_copy(b - Y_RING).wait()

        x = _slab_load(xbuf, (b % X_RING,), ROW_BLK).astype(bf16)
        h1 = jnp.dot(x, wg_s[...], preferred_element_type=f32)
        h2 = jnp.dot(x, wu_s[...], preferred_element_type=f32)
        a = (h1 * jax.nn.sigmoid(h1) * h2).astype(bf16)
        _slab_store(ybuf, (b % Y_RING,), jnp.dot(a, wd_s[...], preferred_element_type=f32))
        y_copy(b).start()
        return e_new

    e_last = lax.fori_loop(0, n_used, body, jnp.int32(-1))
    advance(e_last, jnp.int32(N_EXPERTS - 1))

    for back in range(Y_RING, 0, -1):
        @pl.when(n_used >= back)
        def _():
            y_copy(n_used - back).wait()

    ybuf[0] = jnp.zeros((ROW_BLK * SLAB, LANES), f32)

    def unused_copy(b):
        return pltpu.make_async_copy(ybuf.at[0], y_hbm.at[_slab_rows(b * ROW_BLK, ROW_BLK), :],
                                     sem_y.at[0])

    def ustart(b, _):
        unused_copy(b).start()
        return 0

    def uwait(b, _):
        unused_copy(b).wait()
        return 0

    lax.fori_loop(n_used, N_BLKS, ustart, 0)
    lax.fori_loop(n_used, N_BLKS, uwait, 0)


def _expert_ffn(ends, xs, wg, wu, wd):
    return pl.pallas_call(
        _ffn_kernel,
        grid_spec=pltpu.PrefetchScalarGridSpec(
            num_scalar_prefetch=1,
            grid=(1,),
            in_specs=[pl.BlockSpec(memory_space=pl.ANY)] * 4,
            out_specs=pl.BlockSpec(memory_space=pl.ANY),
            scratch_shapes=[pltpu.VMEM((W_RING, D_MODEL, EXPERT_FF), f32),
                            pltpu.VMEM((W_RING, D_MODEL, EXPERT_FF), f32),
                            pltpu.VMEM((W_RING, EXPERT_FF, D_MODEL), f32),
                            pltpu.VMEM((D_MODEL, EXPERT_FF), bf16),
                            pltpu.VMEM((D_MODEL, EXPERT_FF), bf16),
                            pltpu.VMEM((EXPERT_FF, D_MODEL), bf16),
                            pltpu.VMEM((X_RING, ROW_BLK * SLAB, LANES), f32),
                            pltpu.VMEM((Y_RING, ROW_BLK * SLAB, LANES), f32),
                            pltpu.SemaphoreType.DMA((3, W_RING)),
                            pltpu.SemaphoreType.DMA((X_RING,)),
                            pltpu.SemaphoreType.DMA((Y_RING,))]),
        out_shape=jax.ShapeDtypeStruct((N_ROWS * SLAB, LANES), f32),
        compiler_params=_cparams(("arbitrary",)),
        name="expert_ffn",
    )(ends, wg, wu, wd, xs)


def _combine_kernel(dest_ref, x_ref, cw_ref, g_ref, y_hbm, o_ref, ybuf, sem):
    i = pl.program_id(0)
    n_steps = pl.num_programs(0)

    def row_copy(step, t, k):
        slot = step % 2
        d = dest_ref[k, step * TM_COMB + t]
        return pltpu.make_async_copy(y_hbm.at[_slab_rows(d, 1), :],
                                     ybuf.at[slot, k, _slab_rows(t, 1), :], sem.at[slot])

    def issue(step):
        def one(t, _):
            for k in range(TOP_K):
                row_copy(step, t, k).start()
            return 0
        lax.fori_loop(0, TM_COMB, one, 0, unroll=ISSUE_UNROLL)

    def drain(step):
        def one(t, _):
            for k in range(TOP_K):
                row_copy(step, t, k).wait()
            return 0
        lax.fori_loop(0, TM_COMB, one, 0, unroll=ISSUE_UNROLL)

    @pl.when(i == 0)
    def _():
        issue(i)

    @pl.when(i + 1 < n_steps)
    def _():
        issue(i + 1)

    drain(i)
    slot = i % 2
    w = cw_ref[...]
    x = (x_ref[...] + w[:, 0:1] * _slab_load(ybuf, (slot, 0), TM_COMB)
         + w[:, 1:2] * _slab_load(ybuf, (slot, 1), TM_COMB))
    o_ref[...] = _rms(x, g_ref[...])


def _combine(dest, x1, cw_t, g, y):
    tm = TM_COMB
    return pl.pallas_call(
        _combine_kernel,
        grid_spec=pltpu.PrefetchScalarGridSpec(
            num_scalar_prefetch=1,
            grid=(N_TOK // tm,),
            in_specs=[pl.BlockSpec((tm, D_MODEL), lambda i, d: (i, 0)),
                      pl.BlockSpec((tm, TOP_K), lambda i, d: (i, 0)),
                      pl.BlockSpec((1, D_MODEL), lambda i, d: (0, 0)),
                      pl.BlockSpec(memory_space=pl.ANY)],
            out_specs=pl.BlockSpec((tm, D_MODEL), lambda i, d: (i, 0)),
            scratch_shapes=[pltpu.VMEM((2, TOP_K, tm * SLAB, LANES), f32),
                            pltpu.SemaphoreType.DMA((2,))]),
        out_shape=jax.ShapeDtypeStruct((N_TOK, D_MODEL), f32),
        compiler_params=_cparams(("arbitrary",)),
        name="moe_combine",
    )(dest, x1, cw_t, g, y)


TR_RELAYOUT = 256
KR_TILE = COL_KR // TR_RELAYOUT


def _w_in_relayout_kernel(w_ref, o_ref):
    i = pl.program_id(0)

    @pl.when(i < COL_V // TR_RELAYOUT)
    def _():
        q = ROPE_NA
        for g in range(TR_RELAYOUT // LANES):
            r = g * LANES
            o_ref[r:r + q, :] = w_ref[r:r + q, :].astype(bf16)
            o_ref[r + q:r + 2 * q, :] = w_ref[r + 2 * q:r + 3 * q, :].astype(bf16)
            o_ref[r + 2 * q:r + 3 * q, :] = w_ref[r + q:r + 2 * q, :].astype(bf16)
            o_ref[r + 3 * q:r + 4 * q, :] = w_ref[r + 3 * q:r + 4 * q, :].astype(bf16)

    @pl.when((i >= COL_V // TR_RELAYOUT) & (i != KR_TILE))
    def _():
        o_ref[...] = w_ref[...].astype(bf16)

    @pl.when(i == KR_TILE)
    def _():
        o_ref[...] = jnp.zeros_like(o_ref)
        o_ref[KR_LANE:KR_LANE + MLA_ROPE, :] = w_ref[0:MLA_ROPE, :].astype(bf16)


def _prep_w_in(w_in_t):
    n_src, d = w_in_t.shape
    gate_src = COL_KR + MLA_ROPE

    gate_tile = COL_GA // TR_RELAYOUT
    sub = 8
    shift = (COL_GA - gate_src) // sub

    def src_row(i):
        return ((TR_RELAYOUT // sub) * i - shift * (i // gate_tile)) * sub

    return pl.pallas_call(
        _w_in_relayout_kernel,
        grid=(PROJ_W // TR_RELAYOUT,),
        in_specs=[pl.BlockSpec((pl.Element(TR_RELAYOUT), pl.Element(d)), lambda i: (src_row(i), 0))],
        out_specs=pl.BlockSpec((TR_RELAYOUT, d), lambda i: (i, 0)),
        out_shape=jax.ShapeDtypeStruct((PROJ_W, d), bf16),
        compiler_params=_cparams(("arbitrary",)),
        name="w_in_relayout",
    )(w_in_t)


def _prep_w_uq(w_uq):
    w = w_uq.reshape(MLA_Q_RANK, HEADS, MLA_NOPE + MLA_ROPE)
    w = jnp.pad(w, ((0, 0), (0, 0), (0, HEAD_W - MLA_NOPE - MLA_ROPE)))
    return w.reshape(MLA_Q_RANK, HEADS * HEAD_W).astype(bf16)


def _prep_w_ukv(w_ukv):
    w = w_ukv.reshape(MLA_KV_RANK, HEADS, MLA_NOPE + HEAD_W)
    wk = jnp.pad(w[:, :, :MLA_NOPE], ((0, 0), (0, 0), (0, HEAD_W - MLA_NOPE)))
    wv = w[:, :, MLA_NOPE:]
    return jnp.concatenate([wk.reshape(MLA_KV_RANK, -1), wv.reshape(MLA_KV_RANK, -1)], axis=1).astype(bf16)


def _prep_router(w_group, b_group, w_router, b_router):
    z = lambda n: jnp.zeros((D_MODEL, n), f32)
    w = jnp.concatenate([w_group, z(RT_EXPERT_ROW - N_GROUPS), w_router,
                         z(LANES - RT_EXPERT_ROW - N_EXPERTS)], axis=1)
    b = jnp.concatenate([b_group, jnp.full((RT_EXPERT_ROW - N_GROUPS,), NEG_INF, f32), b_router,
                         jnp.zeros((LANES - RT_EXPERT_ROW - N_EXPERTS,), f32)])[None, :]
    w_hi = w.astype(bf16)
    w_lo = (w - w_hi.astype(f32)).astype(bf16)
    return jnp.concatenate([w_hi, w_lo], axis=1), b


def kernel(x, positions, attn_norm, w_in, da_lambda, da_subln, mla_q_norm, mla_w_uq, mla_kv_norm, mla_w_ukv, w_branch_a, w_branch_b, w_out, ffn_norm, w_group, b_group, w_router, b_router, w_exp_gate, w_exp_up, w_exp_down, final_norm):
    x2d = x.reshape(N_TOK, D_MODEL)
    cos_t, sin_t = _rope_tables(positions)

    proj = _in_proj(x2d, attn_norm[0][None, :], _prep_w_in(w_in[0].T), cos_t, sin_t)
    qb, kb, vb = _mla_prep(proj, mla_q_norm[0][None, :], mla_kv_norm[0][None, :],
                           _prep_w_uq(mla_w_uq[0]), _prep_w_ukv(mla_w_ukv[0]), cos_t, sin_t)
    oa = _attention(proj, proj, proj, COL_Q // HEAD_W, COL_K // HEAD_W, COL_V // HEAD_W,
                    heads=8, qtiles=2, extra=(da_lambda[0], da_subln[0][:, None]))
    ob = _attention(qb, kb, vb, 0, 0, 0, heads=8, qtiles=4)
    w_rt, b_rt = _prep_router(w_group[0], b_group[0], w_router[0], b_router[0])
    x1, hf, eid, cw = _merge(oa, ob, proj, x2d, w_branch_a[0].astype(bf16), w_branch_b[0].astype(bf16),
                             w_out[0].astype(bf16), ffn_norm[0][None, :], w_rt, b_rt)

    dest, ends = _plan(eid)
    dest = dest.reshape(TOP_K, N_TOK)
    ends = ends[0, :N_EXPERTS]
    xs = _dispatch(ends, dest, hf)
    y = _expert_ffn(ends, xs, w_exp_gate[0], w_exp_up[0], w_exp_down[0])
    out = _combine(dest, x1, cw.T, final_norm[None, :], y)
    return out.reshape(BATCH, SEQ, D_MODEL)
```

```python
import functools
import math

import jax
import jax.numpy as jnp
from jax import lax
from jax.experimental import pallas as pl
from jax.experimental.pallas import tpu as pltpu

f32 = jnp.float32
bf16 = jnp.bfloat16
i32 = jnp.int32

D_MODEL = 2048
BATCH = 4
SEQ = 2048
N_TOK = BATCH * SEQ
HEADS = 8
HEAD_W = 128
DA_HALF = 64
MLA_Q_RANK = 512
MLA_KV_RANK = 256
MLA_NOPE = 64
MLA_ROPE = 32
N_GROUPS = 4
EXPERTS_PER_GROUP = 8
N_EXPERTS = N_GROUPS * EXPERTS_PER_GROUP
TOP_K = 2
EXPERT_FF = 512
ROPE_THETA = 10000.0
NORM_EPS = 1e-6
NEG_INF = -1e30
LAM_INIT = 0.8 - 0.6 * math.exp(-0.3 * 0)
LOG2E = math.log2(math.e)

LANES = 128
VMEM_LIMIT = 56 * 1024 * 1024

COL_Q, COL_K, COL_V = 0, 1024, 2048
COL_CQ, COL_CKV = 3072, 3584
COL_KR = 3840
COL_GA, COL_GB = 4096, 6144
PROJ_W = 8192
KR_LANE = MLA_NOPE

TM_PROJ, TN_PROJ = 1024, 1024
TM_PREP = 512
TQ = 256
TM_MERGE = 256
ROW_BLK = 128
N_ROWS = (N_TOK * TOP_K // ROW_BLK + N_EXPERTS) * ROW_BLK
N_BLKS = N_ROWS // ROW_BLK
TT_DISP = 1024
TM_COMB = 128

Q_SCALE_A = (DA_HALF ** -0.5) * LOG2E
Q_SCALE_B = ((MLA_NOPE + MLA_ROPE) ** -0.5) * LOG2E


def _cparams(sem, **kw):
    return pltpu.CompilerParams(dimension_semantics=sem, vmem_limit_bytes=VMEM_LIMIT, **kw)


def _rms(x, g):
    return x * lax.rsqrt(jnp.mean(x * x, axis=-1, keepdims=True) + NORM_EPS) * g


SLAB = D_MODEL // LANES


ISSUE_UNROLL = 16


def _slab_rows(row0, nrows):
    return pl.ds(pl.multiple_of(row0 * SLAB, SLAB), nrows * SLAB)


def _slab_store(ref, idx, val):
    rows = val.shape[0]
    for j in range(SLAB):
        ref[idx + (pl.ds(j, rows, stride=SLAB), slice(None))] = val[:, j * LANES:(j + 1) * LANES]


def _slab_chunk(ref, idx, rows, j):
    return ref[idx + (pl.ds(j, rows, stride=SLAB), slice(None))]


def _slab_load(ref, idx, rows):
    return jnp.concatenate([_slab_chunk(ref, idx, rows, j) for j in range(SLAB)], axis=1)


def _rope_lanes(blk, cos, sin, first_half, half):
    up = pltpu.roll(blk, LANES - half, 1)
    down = pltpu.roll(blk, half, 1)
    return blk * cos + jnp.where(first_half, up, down) * sin


ROPE_NA = DA_HALF // 2
ROPE_NB = MLA_ROPE // 2


def _rope_tables_kernel(pos_ref, inv_ref, cos_ref, sin_ref):
    half = pos_ref.shape[0] // 2
    lane = lax.broadcasted_iota(i32, (half, LANES), 1)
    pos = jnp.where(lane < LANES // 2, pos_ref[:half, :], pos_ref[half:, :]).astype(f32)
    ang = pos * inv_ref[...]
    c = jnp.cos(ang)
    s = jnp.sin(ang)
    one = lambda n: jnp.ones((half, n), f32)
    zero = lambda n: jnp.zeros((half, n), f32)
    tail = LANES - KR_LANE - MLA_ROPE
    for part in range(2):
        o = part * (LANES // 2)
        rows = slice(part * half, (part + 1) * half)
        ca, sa = c[:, o:o + ROPE_NA], s[:, o:o + ROPE_NA]
        cb, sb = c[:, o + ROPE_NA:o + ROPE_NA + ROPE_NB], s[:, o + ROPE_NA:o + ROPE_NA + ROPE_NB]
        cos_ref[rows, :] = jnp.concatenate([ca, ca, ca, ca, one(KR_LANE), cb, cb, one(tail)], axis=1)
        sin_ref[rows, :] = jnp.concatenate([-sa, -sa, sa, sa, zero(KR_LANE), -sb, sb, zero(tail)],
                                           axis=1)


def _rope_tables(positions):
    inv_a = ROPE_THETA ** (-jnp.arange(0, DA_HALF, 2, dtype=f32) / DA_HALF)
    inv_b = ROPE_THETA ** (-jnp.arange(0, MLA_ROPE, 2, dtype=f32) / MLA_ROPE)
    inv = jnp.concatenate([inv_a, inv_b, jnp.zeros((LANES // 2 - ROPE_NA - ROPE_NB,), f32)])
    inv = jnp.tile(inv, 2)[None, :]
    pos = positions.reshape(N_TOK, 1)
    tm = 1024
    return pl.pallas_call(
        _rope_tables_kernel,
        grid=(N_TOK // tm,),
        in_specs=[pl.BlockSpec((tm, 1), lambda i: (i, 0)),
                  pl.BlockSpec((1, LANES), lambda i: (0, 0))],
        out_specs=[pl.BlockSpec((tm, 2 * LANES), lambda i: (i, 0)),
                   pl.BlockSpec((tm, 2 * LANES), lambda i: (i, 0))],
        out_shape=[jax.ShapeDtypeStruct((N_TOK, 2 * LANES), f32)] * 2,
        compiler_params=_cparams(("arbitrary",)),
        name="rope_tables",
    )(pos, inv)


def _in_proj_kernel(x_ref, g_ref, w_ref, cos_ref, sin_ref, o_ref, h_scr):
    n = pl.program_id(1)

    @pl.when(n == 0)
    def _():
        h_scr[...] = _rms(x_ref[...], g_ref[...]).astype(bf16)

    acc = lax.dot_general(h_scr[...], w_ref[...], (((1,), (1,)), ((), ())),
                          preferred_element_type=f32)
    n_rope = COL_V // TN_PROJ
    n_q = COL_K // TN_PROJ

    @pl.when(n < n_rope)
    def _():
        c = jnp.where(n < n_q, Q_SCALE_A, 1.0).astype(f32)
        cos = cos_ref[...] * c
        sin = sin_ref[...] * c
        for j in range(TN_PROJ // LANES):
            blk = acc[:, j * LANES:(j + 1) * LANES]
            rot = pltpu.roll(blk, LANES // 2, 1)
            o_ref[:, j * LANES:(j + 1) * LANES] = (blk * cos + rot * sin).astype(bf16)

    @pl.when(n >= n_rope)
    def _():
        o_ref[...] = acc.astype(bf16)


def _in_proj(x2d, g, w_pad, cos_t, sin_t):
    return pl.pallas_call(
        _in_proj_kernel,
        grid=(N_TOK // TM_PROJ, PROJ_W // TN_PROJ),
        in_specs=[pl.BlockSpec((TM_PROJ, D_MODEL), lambda i, n: (i, 0)),
                  pl.BlockSpec((1, D_MODEL), lambda i, n: (0, 0)),
                  pl.BlockSpec((TN_PROJ, D_MODEL), lambda i, n: (n, 0)),
                  pl.BlockSpec((TM_PROJ, LANES), lambda i, n: (i, 0)),
                  pl.BlockSpec((TM_PROJ, LANES), lambda i, n: (i, 0))],
        out_specs=pl.BlockSpec((TM_PROJ, TN_PROJ), lambda i, n: (i, n)),
        out_shape=jax.ShapeDtypeStruct((N_TOK, PROJ_W), bf16),
        scratch_shapes=[pltpu.VMEM((TM_PROJ, D_MODEL), bf16)],
        compiler_params=_cparams(("arbitrary", "arbitrary")),
        name="in_proj",
    )(x2d, g, w_pad, cos_t, sin_t)


def _mla_prep_kernel(cq_ref, ckv_ref, kr_ref, gq_ref, gkv_ref, wuq_ref, wkv_ref, cos_ref, sin_ref,
                     q_ref, k_ref, v_ref):
    cqn = _rms(cq_ref[...].astype(f32), gq_ref[...]).astype(bf16)
    q = jnp.dot(cqn, wuq_ref[...], preferred_element_type=f32)
    ckvn = _rms(ckv_ref[...].astype(f32), gkv_ref[...]).astype(bf16)
    kv = jnp.dot(ckvn, wkv_ref[...], preferred_element_type=f32)
    cos = cos_ref[...]
    sin = sin_ref[...]
    lane = lax.broadcasted_iota(i32, (TM_PREP, LANES), 1)
    first = lane < (KR_LANE + MLA_ROPE // 2)
    rope = lambda b: _rope_lanes(b, cos, sin, first, MLA_ROPE // 2)
    k_rot = rope(kr_ref[...].astype(f32))
    for h in range(HEADS):
        sl = slice(h * HEAD_W, (h + 1) * HEAD_W)
        q_ref[:, sl] = (rope(q[:, sl]) * Q_SCALE_B).astype(bf16)
        k_ref[:, sl] = (kv[:, sl] + k_rot).astype(bf16)
    v_ref[...] = kv[:, HEADS * HEAD_W:].astype(bf16)


def _mla_prep(proj, gq, gkv, wuq_p, wkv_p, cos_t, sin_t):
    tm = TM_PREP
    w = HEADS * HEAD_W
    return pl.pallas_call(
        _mla_prep_kernel,
        grid=(N_TOK // tm,),
        in_specs=[pl.BlockSpec((tm, MLA_Q_RANK), lambda i: (i, COL_CQ // MLA_Q_RANK)),
                  pl.BlockSpec((tm, MLA_KV_RANK), lambda i: (i, COL_CKV // MLA_KV_RANK)),
                  pl.BlockSpec((tm, LANES), lambda i: (i, COL_KR // LANES)),
                  pl.BlockSpec((1, MLA_Q_RANK), lambda i: (0, 0)),
                  pl.BlockSpec((1, MLA_KV_RANK), lambda i: (0, 0)),
                  pl.BlockSpec((MLA_Q_RANK, w), lambda i: (0, 0)),
                  pl.BlockSpec((MLA_KV_RANK, 2 * w), lambda i: (0, 0)),
                  pl.BlockSpec((tm, LANES), lambda i: (i, 1)),
                  pl.BlockSpec((tm, LANES), lambda i: (i, 1))],
        out_specs=[pl.BlockSpec((tm, w), lambda i: (i, 0))] * 3,
        out_shape=[jax.ShapeDtypeStruct((N_TOK, w), bf16)] * 3,
        compiler_params=_cparams(("arbitrary",)),
        name="mla_prep",
    )(proj, proj, proj, gq, gkv, wuq_p, wkv_p, cos_t, sin_t)


VT_ROWS = HEAD_W + 16


def _attn_kernel(*refs, ncomp, heads, qtiles):
    if ncomp == 2:
        q_ref, k_ref, v_ref, lam_ref, g_ref, o_ref, acc_scr, vt_scr = refs
    else:
        q_ref, k_ref, v_ref, o_ref, acc_scr, vt_scr = refs
    qi = pl.program_id(2)

    @pl.when(qi == 0)
    def _():
        for j in range(SEQ // TQ):
            for h in range(heads):
                vt_scr[j, h, :HEAD_W, :] = v_ref[j * TQ:(j + 1) * TQ, h * HEAD_W:(h + 1) * HEAD_W].T
                vt_scr[j, h, HEAD_W:, :] = jnp.ones((VT_ROWS - HEAD_W, TQ), bf16)

    lane = lax.broadcasted_iota(i32, (TQ, HEAD_W), 1)
    qs = []
    for t in range(qtiles):
        for h in range(heads):
            q = q_ref[t * TQ:(t + 1) * TQ, h * HEAD_W:(h + 1) * HEAD_W]
            if ncomp == 2:
                zero = jnp.zeros_like(q)
                in_a = (lane % DA_HALF) < (DA_HALF // 2)
                qs += [jnp.where(in_a, q, zero), jnp.where(in_a, zero, q)]
            else:
                qs.append(q)
    per_tile = heads * ncomp
    head_of = lambda n: (n % per_tile) // ncomp

    def step(j, carry, modes):
        off = pl.multiple_of(j * TQ, TQ)
        live = [n for n in range(len(qs)) if modes[n // per_tile] != "skip"]
        ss = {}
        for n in live:
            h = head_of(n)
            k = k_ref[pl.ds(off, TQ), h * HEAD_W:(h + 1) * HEAD_W]
            s = lax.dot_general(k, qs[n], (((1,), (1,)), ((), ())), preferred_element_type=f32)
            if modes[n // per_tile] == "diag":
                key = lax.broadcasted_iota(i32, (TQ, TQ), 0)
                qry = lax.broadcasted_iota(i32, (TQ, TQ), 1)
                s = jnp.where(key <= qry, s, NEG_INF)
            ss[n] = s
        stats = {}
        for n in live:
            m, l = carry[n]
            s = ss[n]
            m_new = jnp.maximum(m, jnp.max(s, axis=0, keepdims=True))
            alpha = jnp.exp2(m - m_new)
            p = jnp.exp2(s - m_new)
            stats[n] = (m_new, alpha * l, alpha, p.astype(bf16))
        out = list(carry)
        for n in live:
            m_new, l_scaled, alpha, p = stats[n]
            pv = jnp.dot(vt_scr[j, head_of(n)], p, preferred_element_type=f32)
            acc_scr[n] = alpha * acc_scr[n] + pv[:HEAD_W, :]
            out[n] = (m_new, l_scaled + pv[HEAD_W:HEAD_W + 1, :])
        return tuple(out)

    acc_scr[...] = jnp.zeros_like(acc_scr)
    init = tuple((jnp.full((1, TQ), NEG_INF, f32), jnp.zeros((1, TQ), f32)) for _ in qs)
    first = qi * qtiles
    carry = lax.fori_loop(0, first, lambda j, c: step(j, c, ("full",) * qtiles), init)
    for u in range(qtiles):
        modes = tuple("skip" if t < u else ("diag" if t == u else "full") for t in range(qtiles))
        carry = step(first + u, carry, modes)
    carry = tuple((m, l, acc_scr[n]) for n, (m, l) in enumerate(carry))

    if ncomp == 2:
        ll = lam_ref[...]
        s1 = jnp.sum(ll[0:1, :] * ll[1:2, :], axis=-1, keepdims=True)
        s2 = jnp.sum(ll[2:3, :] * ll[3:4, :], axis=-1, keepdims=True)
        lam = jnp.exp(s1) - jnp.exp(s2) + LAM_INIT
    for t in range(qtiles):
        for h in range(heads):
            n0 = (t * heads + h) * ncomp
            if ncomp == 2:
                (_, l0, a0), (_, l1, a1) = carry[n0], carry[n0 + 1]
                o = a0 / l0 - lam * (a1 / l1)
                ms = jnp.mean(o * o, axis=0, keepdims=True)
                o = o * lax.rsqrt(ms + NORM_EPS) * g_ref[...] * (1.0 - LAM_INIT)
            else:
                _, l0, a0 = carry[n0]
                o = a0 / l0
            o_ref[t * TQ:(t + 1) * TQ, h * HEAD_W:(h + 1) * HEAD_W] = o.T.astype(bf16)


def _attention(q_arr, k_arr, v_arr, q_blk0, k_blk0, v_blk0, heads, qtiles, extra=()):
    ncomp = 2 if extra else 1
    tq = qtiles * TQ
    nq = SEQ // tq
    w = heads * HEAD_W
    qb, kb, vb = q_blk0 // heads, k_blk0 // heads, v_blk0 // heads
    in_specs = [pl.BlockSpec((tq, w), lambda b, h, i: (b * nq + i, qb + h)),
                pl.BlockSpec((SEQ, w), lambda b, h, i: (b, kb + h)),
                pl.BlockSpec((SEQ, w), lambda b, h, i: (b, vb + h))]
    for a in extra:
        in_specs.append(pl.BlockSpec(a.shape, lambda b, h, i: (0, 0)))
    return pl.pallas_call(
        functools.partial(_attn_kernel, ncomp=ncomp, heads=heads, qtiles=qtiles),
        grid=(BATCH, HEADS // heads, nq),
        in_specs=in_specs,
        out_specs=pl.BlockSpec((tq, w), lambda b, h, i: (b * nq + i, h)),
        out_shape=jax.ShapeDtypeStruct((N_TOK, HEADS * HEAD_W), bf16),
        scratch_shapes=[pltpu.VMEM((qtiles * heads * ncomp, HEAD_W, TQ), f32),
                        pltpu.VMEM((SEQ // TQ, heads, VT_ROWS, TQ), bf16)],
        compiler_params=_cparams(("arbitrary", "arbitrary", "arbitrary")),
        name="diff_attn" if extra else "mla_attn",
    )(q_arr, k_arr, v_arr, *extra)


RT_GROUP_ROW = 0
RT_EXPERT_ROW = 8


def _route(hf, w_ref, b_ref, eid_ref, cw_ref):
    hf_hi = hf.astype(bf16)
    hf_lo = (hf - hf_hi.astype(f32)).astype(bf16)
    a = jnp.dot(hf_hi, w_ref[...], preferred_element_type=f32)
    b = jnp.dot(hf_lo, w_ref[:, :LANES], preferred_element_type=f32)
    logits = a[:, :LANES] + (a[:, LANES:] + b) + b_ref[...]
    lt = logits.T
    tm = lt.shape[1]
    ridx = lax.broadcasted_iota(i32, (EXPERTS_PER_GROUP, tm), 0).astype(f32)

    def top1(v):
        vmax = jnp.max(v, axis=0, keepdims=True)
        idx = jnp.min(jnp.where(v == vmax, ridx, float(EXPERTS_PER_GROUP)), axis=0, keepdims=True)
        return vmax, idx

    gl = lt[RT_GROUP_ROW:RT_GROUP_ROW + 8, :]
    gmax, gsel = top1(gl)
    p_g = 1.0 / jnp.sum(jnp.exp(gl - gmax), axis=0, keepdims=True)
    el = jnp.zeros((EXPERTS_PER_GROUP, tm), f32)
    for g in range(N_GROUPS):
        r0 = RT_EXPERT_ROW + g * EXPERTS_PER_GROUP
        el = jnp.where(gsel == float(g), lt[r0:r0 + EXPERTS_PER_GROUP, :], el)
    e1, i1 = top1(el)
    e2, i2 = top1(jnp.where(ridx == i1, -jnp.inf, el))
    r = jnp.exp(e2 - e1)
    w1 = 1.0 / (1.0 + r)
    base = gsel * EXPERTS_PER_GROUP
    eid_ref[0:1, :] = (base + i1).astype(i32)
    eid_ref[1:2, :] = (base + i2).astype(i32)
    cw_ref[0:1, :] = p_g * w1
    cw_ref[1:2, :] = p_g * (r * w1)


def _merge_kernel(oa_ref, ob_ref, ga_ref, gb_ref, x_ref, wa_ref, wb_ref, wo_ref, gf_ref, wr_ref, br_ref,
                  x1_ref, hf_ref, eid_ref, cw_ref):
    ya = jnp.dot(oa_ref[...], wa_ref[...], preferred_element_type=f32)
    yb = jnp.dot(ob_ref[...], wb_ref[...], preferred_element_type=f32)
    mixed = (jax.nn.sigmoid(ga_ref[...].astype(f32)) * ya
             + jax.nn.sigmoid(gb_ref[...].astype(f32)) * yb)
    x1 = x_ref[...] + jnp.dot(mixed.astype(bf16), wo_ref[...], preferred_element_type=f32)
    x1_ref[...] = x1
    hf = _rms(x1, gf_ref[...])
    _slab_store(hf_ref, (), hf)
    _route(hf, wr_ref, br_ref, eid_ref, cw_ref)


def _merge(oa, ob, proj, x2d, wa, wb, wo, gf, w_rt, b_rt):
    tm = TM_MERGE
    w = HEADS * HEAD_W
    const = lambda shape: pl.BlockSpec(shape, lambda i: (0, 0), pipeline_mode=pl.Buffered(1))
    return pl.pallas_call(
        _merge_kernel,
        grid=(N_TOK // tm,),
        in_specs=[pl.BlockSpec((tm, w), lambda i: (i, 0)),
                  pl.BlockSpec((tm, w), lambda i: (i, 0)),
                  pl.BlockSpec((tm, D_MODEL), lambda i: (i, COL_GA // D_MODEL)),
                  pl.BlockSpec((tm, D_MODEL), lambda i: (i, COL_GB // D_MODEL)),
                  pl.BlockSpec((tm, D_MODEL), lambda i: (i, 0)),
                  const((w, D_MODEL)), const((w, D_MODEL)), const((D_MODEL, D_MODEL)),
                  const((1, D_MODEL)), const((D_MODEL, 2 * LANES)), const((1, LANES))],
        out_specs=[pl.BlockSpec((tm, D_MODEL), lambda i: (i, 0)),
                   pl.BlockSpec((tm * SLAB, LANES), lambda i: (i, 0)),
                   pl.BlockSpec((TOP_K, tm), lambda i: (0, i)),
                   pl.BlockSpec((TOP_K, tm), lambda i: (0, i))],
        out_shape=[jax.ShapeDtypeStruct((N_TOK, D_MODEL), f32),
                   jax.ShapeDtypeStruct((N_TOK * SLAB, LANES), f32),
                   jax.ShapeDtypeStruct((TOP_K, N_TOK), i32),
                   jax.ShapeDtypeStruct((TOP_K, N_TOK), f32)],
        compiler_params=_cparams(("arbitrary",)),
        name="merge_route",
    )(oa, ob, proj, proj, x2d, wa, wb, wo, gf, w_rt, b_rt)


PLAN_N = 128


def _plan_kernel(eid_ref, dest_ref, ends_ref):
    eid = eid_ref[...]
    r = lax.broadcasted_iota(i32, (PLAN_N, PLAN_N), 0)
    c = lax.broadcasted_iota(i32, (PLAN_N, PLAN_N), 1)
    upper = (r < c).astype(bf16)
    lower = (c < r).astype(bf16)
    ones = jnp.ones((PLAN_N, PLAN_N), bf16)
    lane = lax.broadcasted_iota(i32, (1, LANES), 1)
    dest = jnp.zeros((PLAN_N, PLAN_N), f32)
    ends = jnp.zeros((1, LANES), f32)
    start = jnp.zeros((1, 1), f32)
    for e in range(N_EXPERTS):
        m = (eid == e).astype(f32)
        mb = m.astype(bf16)
        within = jnp.dot(mb, upper, preferred_element_type=f32)
        rowtot = jnp.dot(mb, ones, preferred_element_type=f32)
        rowpre = jnp.dot(lower, rowtot.astype(bf16), preferred_element_type=f32)
        cnt = jnp.sum(jnp.sum(m, axis=1, keepdims=True), axis=0, keepdims=True)
        dest = dest + m * (start + within + rowpre)
        start = start + jnp.floor((cnt + (ROW_BLK - 1)) * (1.0 / ROW_BLK)) * ROW_BLK
        ends = jnp.where(lane == e, start, ends)
    dest_ref[...] = dest.astype(i32)
    ends_ref[...] = ends.astype(i32)


def _plan(eid):
    whole = lambda shape: pl.BlockSpec(shape, lambda: (0, 0))
    return pl.pallas_call(
        _plan_kernel,
        in_specs=[whole((PLAN_N, PLAN_N))],
        out_specs=[whole((PLAN_N, PLAN_N)), whole((1, LANES))],
        out_shape=[jax.ShapeDtypeStruct((PLAN_N, PLAN_N), i32),
                   jax.ShapeDtypeStruct((1, LANES), i32)],
        name="moe_plan",
    )(eid.reshape(PLAN_N, PLAN_N))


H_RING = 3


def _dispatch_kernel(ends_ref, dest_ref, hf_hbm, xs_hbm, zbuf, hbuf, sem, sem_h, sem_r):
    def tail_copy(e):
        pend = ends_ref[e]
        prev = jnp.where(e > 0, ends_ref[jnp.maximum(e - 1, 0)], 0)
        row0 = pl.multiple_of(jnp.maximum(pend - ROW_BLK, 0), ROW_BLK)
        return pend > prev, pltpu.make_async_copy(zbuf, xs_hbm.at[_slab_rows(row0, ROW_BLK), :], sem)

    n_tiles = N_TOK // TT_DISP

    def h_copy(i):
        return pltpu.make_async_copy(hf_hbm.at[_slab_rows(i * TT_DISP, TT_DISP), :],
                                     hbuf.at[i % H_RING], sem_h.at[i % H_RING])

    for i in range(H_RING - 1):
        h_copy(i).start()

    zbuf[...] = jnp.zeros_like(zbuf)

    def zstart(e, _):
        used, cp = tail_copy(e)

        @pl.when(used)
        def _():
            cp.start()
        return 0

    def zwait(e, _):
        used, cp = tail_copy(e)

        @pl.when(used)
        def _():
            cp.wait()
        return 0

    def unused_copy(b):
        row0 = pl.multiple_of(b * ROW_BLK, ROW_BLK)
        return pltpu.make_async_copy(zbuf, xs_hbm.at[_slab_rows(row0, ROW_BLK), :], sem)

    def ustart(b, _):
        unused_copy(b).start()
        return 0

    def uwait(b, _):
        unused_copy(b).wait()
        return 0

    n_used = ends_ref[N_EXPERTS - 1] // ROW_BLK
    lax.fori_loop(0, N_EXPERTS, zstart, 0)
    lax.fori_loop(n_used, N_BLKS, ustart, 0)
    lax.fori_loop(0, N_EXPERTS, zwait, 0)
    lax.fori_loop(n_used, N_BLKS, uwait, 0)

    def row_copy(i, t, k):
        d = dest_ref[k, i * TT_DISP + t]
        return pltpu.make_async_copy(hbuf.at[i % H_RING, _slab_rows(t, 1), :],
                                     xs_hbm.at[_slab_rows(d, 1), :], sem_r.at[i % H_RING])

    def issue(i):
        def one(t, _):
            for k in range(TOP_K):
                row_copy(i, t, k).start(priority=k)
            return 0
        lax.fori_loop(0, TT_DISP, one, 0, unroll=ISSUE_UNROLL)

    def drain(i):
        def one(t, _):
            for k in range(TOP_K):
                row_copy(i, t, k).wait()
            return 0
        lax.fori_loop(0, TT_DISP, one, 0, unroll=ISSUE_UNROLL)

    for i in range(n_tiles):
        h_copy(i).wait()
        issue(i)
        if i >= 1:
            drain(i - 1)
        if i + H_RING - 1 < n_tiles:
            h_copy(i + H_RING - 1).start()
    drain(n_tiles - 1)


def _dispatch(ends, dest, hf):
    return pl.pallas_call(
        _dispatch_kernel,
        grid_spec=pltpu.PrefetchScalarGridSpec(
            num_scalar_prefetch=2,
            grid=(1,),
            in_specs=[pl.BlockSpec(memory_space=pl.ANY)],
            out_specs=pl.BlockSpec(memory_space=pl.ANY),
            scratch_shapes=[pltpu.VMEM((ROW_BLK * SLAB, LANES), f32),
                            pltpu.VMEM((H_RING, TT_DISP * SLAB, LANES), f32),
                            pltpu.SemaphoreType.DMA(()),
                            pltpu.SemaphoreType.DMA((H_RING,)),
                            pltpu.SemaphoreType.DMA((H_RING,))]),
        out_shape=jax.ShapeDtypeStruct((N_ROWS * SLAB, LANES), f32),
        compiler_params=_cparams(("arbitrary",)),
        name="moe_dispatch",
    )(ends, dest, hf)


W_RING = 3
X_RING = 6
Y_RING = 3


def _ffn_kernel(ends_ref, wg_hbm, wu_hbm, wd_hbm, xs_hbm, y_hbm,
                wg_f, wu_f, wd_f, wg_s, wu_s, wd_s, xbuf, ybuf, sem_w, sem_x, sem_y):
    n_used = ends_ref[N_EXPERTS - 1] // ROW_BLK

    def w_copies(ex):
        slot = ex % W_RING
        return [pltpu.make_async_copy(hbm.at[ex], buf.at[slot], sem_w.at[i, slot])
                for i, (hbm, buf) in enumerate(((wg_hbm, wg_f), (wu_hbm, wu_f), (wd_hbm, wd_f)))]

    def advance(e_from, e_to):
        def one(ex, _):
            for cp in w_copies(ex):
                cp.wait()

            @pl.when(ex + (W_RING - 1) < N_EXPERTS)
            def _():
                for cp in w_copies(ex + (W_RING - 1)):
                    cp.start()
            return 0

        lax.fori_loop(e_from + 1, e_to + 1, one, 0)

    def x_copy(b):
        slot = b % X_RING
        return pltpu.make_async_copy(xs_hbm.at[_slab_rows(b * ROW_BLK, ROW_BLK), :], xbuf.at[slot],
                                     sem_x.at[slot])

    def y_copy(b):
        slot = b % Y_RING
        return pltpu.make_async_copy(ybuf.at[slot], y_hbm.at[_slab_rows(b * ROW_BLK, ROW_BLK), :],
                                     sem_y.at[slot])

    for ex in range(W_RING - 1):
        for cp in w_copies(ex):
            cp.start()
    for b0 in range(X_RING - 1):
        @pl.when(b0 < n_used)
        def _():
            x_copy(b0).start()

    def body(b, e):
        @pl.when(b + (X_RING - 1) < n_used)
        def _():
            x_copy(b + (X_RING - 1)).start()

        e_new = lax.while_loop(lambda ex: b * ROW_BLK >= ends_ref[ex], lambda ex: ex + 1,
                               jnp.maximum(e, 0))

        @pl.when(e_new != e)
        def _():
            advance(e, e_new)
            wslot = e_new % W_RING
            wg_s[...] = wg_f[wslot].astype(bf16)
            wu_s[...] = wu_f[wslot].astype(bf16)
            wd_s[...] = wd_f[wslot].astype(bf16)

        x_copy(b).wait()

        @pl.when(b >= Y_RING)
        def _():
            y_copy(b - Y_RING).wait()

        x = _slab_load(xbuf, (b % X_RING,), ROW_BLK).astype(bf16)
        h1 = jnp.dot(x, wg_s[...], preferred_element_type=f32)
        h2 = jnp.dot(x, wu_s[...], preferred_element_type=f32)
        a = (h1 * jax.nn.sigmoid(h1) * h2).astype(bf16)
        _slab_store(ybuf, (b % Y_RING,), jnp.dot(a, wd_s[...], preferred_element_type=f32))
        y_copy(b).start()
        return e_new

    e_last = lax.fori_loop(0, n_used, body, jnp.int32(-1))
    advance(e_last, jnp.int32(N_EXPERTS - 1))

    for back in range(Y_RING, 0, -1):
        @pl.when(n_used >= back)
        def _():
            y_copy(n_used - back).wait()

    ybuf[0] = jnp.zeros((ROW_BLK * SLAB, LANES), f32)

    def unused_copy(b):
        return pltpu.make_async_copy(ybuf.at[0], y_hbm.at[_slab_rows(b * ROW_BLK, ROW_BLK), :],
                                     sem_y.at[0])

    def ustart(b, _):
        unused_copy(b).start()
        return 0

    def uwait(b, _):
        unused_copy(b).wait()
        return 0

    lax.fori_loop(n_used, N_BLKS, ustart, 0)
    lax.fori_loop(n_used, N_BLKS, uwait, 0)


def _expert_ffn(ends, xs, wg, wu, wd):
    return pl.pallas_call(
        _ffn_kernel,
        grid_spec=pltpu.PrefetchScalarGridSpec(
            num_scalar_prefetch=1,
            grid=(1,),
            in_specs=[pl.BlockSpec(memory_space=pl.ANY)] * 4,
            out_specs=pl.BlockSpec(memory_space=pl.ANY),
            scratch_shapes=[pltpu.VMEM((W_RING, D_MODEL, EXPERT_FF), f32),
                            pltpu.VMEM((W_RING, D_MODEL, EXPERT_FF), f32),
                            pltpu.VMEM((W_RING, EXPERT_FF, D_MODEL), f32),
                            pltpu.VMEM((D_MODEL, EXPERT_FF), bf16),
                            pltpu.VMEM((D_MODEL, EXPERT_FF), bf16),
                            pltpu.VMEM((EXPERT_FF, D_MODEL), bf16),
                            pltpu.VMEM((X_RING, ROW_BLK * SLAB, LANES), f32),
                            pltpu.VMEM((Y_RING, ROW_BLK * SLAB, LANES), f32),
                            pltpu.SemaphoreType.DMA((3, W_RING)),
                            pltpu.SemaphoreType.DMA((X_RING,)),
                            pltpu.SemaphoreType.DMA((Y_RING,))]),
        out_shape=jax.ShapeDtypeStruct((N_ROWS * SLAB, LANES), f32),
        compiler_params=_cparams(("arbitrary",)),
        name="expert_ffn",
    )(ends, wg, wu, wd, xs)


def _combine_kernel(dest_ref, x_ref, cw_ref, g_ref, y_hbm, o_ref, ybuf, sem):
    i = pl.program_id(0)
    n_steps = pl.num_programs(0)

    def row_copy(step, t, k):
        slot = step % 2
        d = dest_ref[k, step * TM_COMB + t]
        return pltpu.make_async_copy(y_hbm.at[_slab_rows(d, 1), :],
                                     ybuf.at[slot, k, _slab_rows(t, 1), :], sem.at[slot])

    def issue(step):
        def one(t, _):
            for k in range(TOP_K):
                row_copy(step, t, k).start()
            return 0
        lax.fori_loop(0, TM_COMB, one, 0, unroll=ISSUE_UNROLL)

    def drain(step):
        def one(t, _):
            for k in range(TOP_K):
                row_copy(step, t, k).wait()
            return 0
        lax.fori_loop(0, TM_COMB, one, 0, unroll=ISSUE_UNROLL)

    @pl.when(i == 0)
    def _():
        issue(i)

    @pl.when(i + 1 < n_steps)
    def _():
        issue(i + 1)

    drain(i)
    slot = i % 2
    w = cw_ref[...]
    x = (x_ref[...] + w[:, 0:1] * _slab_load(ybuf, (slot, 0), TM_COMB)
         + w[:, 1:2] * _slab_load(ybuf, (slot, 1), TM_COMB))
    o_ref[...] = _rms(x, g_ref[...])


def _combine(dest, x1, cw_t, g, y):
    tm = TM_COMB
    return pl.pallas_call(
        _combine_kernel,
        grid_spec=pltpu.PrefetchScalarGridSpec(
            num_scalar_prefetch=1,
            grid=(N_TOK // tm,),
            in_specs=[pl.BlockSpec((tm, D_MODEL), lambda i, d: (i, 0)),
                      pl.BlockSpec((tm, TOP_K), lambda i, d: (i, 0)),
                      pl.BlockSpec((1, D_MODEL), lambda i, d: (0, 0)),
                      pl.BlockSpec(memory_space=pl.ANY)],
            out_specs=pl.BlockSpec((tm, D_MODEL), lambda i, d: (i, 0)),
            scratch_shapes=[pltpu.VMEM((2, TOP_K, tm * SLAB, LANES), f32),
                            pltpu.SemaphoreType.DMA((2,))]),
        out_shape=jax.ShapeDtypeStruct((N_TOK, D_MODEL), f32),
        compiler_params=_cparams(("arbitrary",)),
        name="moe_combine",
    )(dest, x1, cw_t, g, y)


TR_RELAYOUT = 256
KR_TILE = COL_KR // TR_RELAYOUT


def _w_in_relayout_kernel(w_ref, o_ref):
    i = pl.program_id(0)

    @pl.when(i < COL_V // TR_RELAYOUT)
    def _():
        q = ROPE_NA
        for g in range(TR_RELAYOUT // LANES):
            r = g * LANES
            o_ref[r:r + q, :] = w_ref[r:r + q, :].astype(bf16)
            o_ref[r + q:r + 2 * q, :] = w_ref[r + 2 * q:r + 3 * q, :].astype(bf16)
            o_ref[r + 2 * q:r + 3 * q, :] = w_ref[r + q:r + 2 * q, :].astype(bf16)
            o_ref[r + 3 * q:r + 4 * q, :] = w_ref[r + 3 * q:r + 4 * q, :].astype(bf16)

    @pl.when((i >= COL_V // TR_RELAYOUT) & (i != KR_TILE))
    def _():
        o_ref[...] = w_ref[...].astype(bf16)

    @pl.when(i == KR_TILE)
    def _():
        o_ref[...] = jnp.zeros_like(o_ref)
        o_ref[KR_LANE:KR_LANE + MLA_ROPE, :] = w_ref[0:MLA_ROPE, :].astype(bf16)


def _prep_w_in(w_in_t):
    n_src, d = w_in_t.shape
    gate_src = COL_KR + MLA_ROPE

    gate_tile = COL_GA // TR_RELAYOUT
    sub = 8
    shift = (COL_GA - gate_src) // sub

    def src_row(i):
        return ((TR_RELAYOUT // sub) * i - shift * (i // gate_tile)) * sub

    return pl.pallas_call(
        _w_in_relayout_kernel,
        grid=(PROJ_W // TR_RELAYOUT,),
        in_specs=[pl.BlockSpec((pl.Element(TR_RELAYOUT), pl.Element(d)), lambda i: (src_row(i), 0))],
        out_specs=pl.BlockSpec((TR_RELAYOUT, d), lambda i: (i, 0)),
        out_shape=jax.ShapeDtypeStruct((PROJ_W, d), bf16),
        compiler_params=_cparams(("arbitrary",)),
        name="w_in_relayout",
    )(w_in_t)


def _prep_w_uq(w_uq):
    w = w_uq.reshape(MLA_Q_RANK, HEADS, MLA_NOPE + MLA_ROPE)
    w = jnp.pad(w, ((0, 0), (0, 0), (0, HEAD_W - MLA_NOPE - MLA_ROPE)))
    return w.reshape(MLA_Q_RANK, HEADS * HEAD_W).astype(bf16)


def _prep_w_ukv(w_ukv):
    w = w_ukv.reshape(MLA_KV_RANK, HEADS, MLA_NOPE + HEAD_W)
    wk = jnp.pad(w[:, :, :MLA_NOPE], ((0, 0), (0, 0), (0, HEAD_W - MLA_NOPE)))
    wv = w[:, :, MLA_NOPE:]
    return jnp.concatenate([wk.reshape(MLA_KV_RANK, -1), wv.reshape(MLA_KV_RANK, -1)], axis=1).astype(bf16)


def _prep_router(w_group, b_group, w_router, b_router):
    z = lambda n: jnp.zeros((D_MODEL, n), f32)
    w = jnp.concatenate([w_group, z(RT_EXPERT_ROW - N_GROUPS), w_router,
                         z(LANES - RT_EXPERT_ROW - N_EXPERTS)], axis=1)
    b = jnp.concatenate([b_group, jnp.full((RT_EXPERT_ROW - N_GROUPS,), NEG_INF, f32), b_router,
                         jnp.zeros((LANES - RT_EXPERT_ROW - N_EXPERTS,), f32)])[None, :]
    w_hi = w.astype(bf16)
    w_lo = (w - w_hi.astype(f32)).astype(bf16)
    return jnp.concatenate([w_hi, w_lo], axis=1), b


def kernel(x, positions, attn_norm, w_in, da_lambda, da_subln, mla_q_norm, mla_w_uq, mla_kv_norm, mla_w_ukv, w_branch_a, w_branch_b, w_out, ffn_norm, w_group, b_group, w_router, b_router, w_exp_gate, w_exp_up, w_exp_down, final_norm):
    x2d = x.reshape(N_TOK, D_MODEL)
    cos_t, sin_t = _rope_tables(positions)

    proj = _in_proj(x2d, attn_norm[0][None, :], _prep_w_in(w_in[0].T), cos_t, sin_t)
    qb, kb, vb = _mla_prep(proj, mla_q_norm[0][None, :], mla_kv_norm[0][None, :],
                           _prep_w_uq(mla_w_uq[0]), _prep_w_ukv(mla_w_ukv[0]), cos_t, sin_t)
    oa = _attention(proj, proj, proj, COL_Q // HEAD_W, COL_K // HEAD_W, COL_V // HEAD_W,
                    heads=8, qtiles=2, extra=(da_lambda[0], da_subln[0][:, None]))
    ob = _attention(qb, kb, vb, 0, 0, 0, heads=8, qtiles=4)
    w_rt, b_rt = _prep_router(w_group[0], b_group[0], w_router[0], b_router[0])
    x1, hf, eid, cw = _merge(oa, ob, proj, x2d, w_branch_a[0].astype(bf16), w_branch_b[0].astype(bf16),
                             w_out[0].astype(bf16), ffn_norm[0][None, :], w_rt, b_rt)

    dest, ends = _plan(eid)
    dest = dest.reshape(TOP_K, N_TOK)
    ends = ends[0, :N_EXPERTS]
    xs = _dispatch(ends, dest, hf)
    y = _expert_ffn(ends, xs, w_exp_gate[0], w_exp_up[0], w_exp_down[0])
    out = _combine(dest, x1, cw.T, final_norm[None, :], y)
    return out.reshape(BATCH, SEQ, D_MODEL)
```

```python
import functools
import math

import jax
import jax.numpy as jnp
from jax import lax
from jax.experimental import pallas as pl
from jax.experimental.pallas import tpu as pltpu

f32 = jnp.float32
bf16 = jnp.bfloat16
i32 = jnp.int32

D_MODEL = 2048
BATCH = 4
SEQ = 2048
N_TOK = BATCH * SEQ
HEADS = 8
HEAD_W = 128
DA_HALF = 64
MLA_Q_RANK = 512
MLA_KV_RANK = 256
MLA_NOPE = 64
MLA_ROPE = 32
N_GROUPS = 4
EXPERTS_PER_GROUP = 8
N_EXPERTS = N_GROUPS * EXPERTS_PER_GROUP
TOP_K = 2
EXPERT_FF = 512
ROPE_THETA = 10000.0
NORM_EPS = 1e-6
NEG_INF = -1e30
LAM_INIT = 0.8 - 0.6 * math.exp(-0.3 * 0)
LOG2E = math.log2(math.e)

LANES = 128
VMEM_LIMIT = 56 * 1024 * 1024

COL_Q, COL_K, COL_V = 0, 1024, 2048
COL_CQ, COL_CKV = 3072, 3584
COL_KR = 3840
COL_GA, COL_GB = 4096, 6144
PROJ_W = 8192
KR_LANE = MLA_NOPE

TM_PROJ, TN_PROJ = 1024, 1024
TM_PREP = 1024
TQ = 256
TM_MERGE = 256
ROW_BLK = 128
N_ROWS = (N_TOK * TOP_K // ROW_BLK + N_EXPERTS) * ROW_BLK
N_BLKS = N_ROWS // ROW_BLK
TT_DISP = 1024
TM_COMB = 128

Q_SCALE_A = (DA_HALF ** -0.5) * LOG2E
Q_SCALE_B = ((MLA_NOPE + MLA_ROPE) ** -0.5) * LOG2E


def _cparams(sem, **kw):
    return pltpu.CompilerParams(dimension_semantics=sem, vmem_limit_bytes=VMEM_LIMIT, **kw)


def _rms(x, g):
    return x * lax.rsqrt(jnp.mean(x * x, axis=-1, keepdims=True) + NORM_EPS) * g


SLAB = D_MODEL // LANES


ISSUE_UNROLL = 16


def _slab_rows(row0, nrows):
    return pl.ds(pl.multiple_of(row0 * SLAB, SLAB), nrows * SLAB)


def _slab_store(ref, idx, val):
    rows = val.shape[0]
    for j in range(SLAB):
        ref[idx + (pl.ds(j, rows, stride=SLAB), slice(None))] = val[:, j * LANES:(j + 1) * LANES]


def _slab_chunk(ref, idx, rows, j):
    return ref[idx + (pl.ds(j, rows, stride=SLAB), slice(None))]


def _slab_load(ref, idx, rows):
    return jnp.concatenate([_slab_chunk(ref, idx, rows, j) for j in range(SLAB)], axis=1)


def _rope_lanes(blk, cos, sin, first_half, half):
    up = pltpu.roll(blk, LANES - half, 1)
    down = pltpu.roll(blk, half, 1)
    return blk * cos + jnp.where(first_half, up, down) * sin


ROPE_NA = DA_HALF // 2
ROPE_NB = MLA_ROPE // 2


def _rope_tables_kernel(pos_ref, inv_ref, cos_ref, sin_ref):
    half = pos_ref.shape[0] // 2
    lane = lax.broadcasted_iota(i32, (half, LANES), 1)
    pos = jnp.where(lane < LANES // 2, pos_ref[:half, :], pos_ref[half:, :]).astype(f32)
    ang = pos * inv_ref[...]
    c = jnp.cos(ang)
    s = jnp.sin(ang)
    one = lambda n: jnp.ones((half, n), f32)
    zero = lambda n: jnp.zeros((half, n), f32)
    tail = LANES - KR_LANE - MLA_ROPE
    for part in range(2):
        o = part * (LANES // 2)
        rows = slice(part * half, (part + 1) * half)
        ca, sa = c[:, o:o + ROPE_NA], s[:, o:o + ROPE_NA]
        cb, sb = c[:, o + ROPE_NA:o + ROPE_NA + ROPE_NB], s[:, o + ROPE_NA:o + ROPE_NA + ROPE_NB]
        cos_ref[rows, :] = jnp.concatenate([ca, ca, ca, ca, one(KR_LANE), cb, cb, one(tail)], axis=1)
        sin_ref[rows, :] = jnp.concatenate([-sa, -sa, sa, sa, zero(KR_LANE), -sb, sb, zero(tail)],
                                           axis=1)


def _rope_tables(positions):
    inv_a = ROPE_THETA ** (-jnp.arange(0, DA_HALF, 2, dtype=f32) / DA_HALF)
    inv_b = ROPE_THETA ** (-jnp.arange(0, MLA_ROPE, 2, dtype=f32) / MLA_ROPE)
    inv = jnp.concatenate([inv_a, inv_b, jnp.zeros((LANES // 2 - ROPE_NA - ROPE_NB,), f32)])
    inv = jnp.tile(inv, 2)[None, :]
    pos = positions.reshape(N_TOK, 1)
    tm = 1024
    return pl.pallas_call(
        _rope_tables_kernel,
        grid=(N_TOK // tm,),
        in_specs=[pl.BlockSpec((tm, 1), lambda i: (i, 0)),
                  pl.BlockSpec((1, LANES), lambda i: (0, 0))],
        out_specs=[pl.BlockSpec((tm, 2 * LANES), lambda i: (i, 0)),
                   pl.BlockSpec((tm, 2 * LANES), lambda i: (i, 0))],
        out_shape=[jax.ShapeDtypeStruct((N_TOK, 2 * LANES), f32)] * 2,
        compiler_params=_cparams(("arbitrary",)),
        name="rope_tables",
    )(pos, inv)


def _in_proj_kernel(x_ref, g_ref, w_ref, cos_ref, sin_ref, o_ref, h_scr):
    n = pl.program_id(1)

    @pl.when(n == 0)
    def _():
        h_scr[...] = _rms(x_ref[...], g_ref[...]).astype(bf16)

    acc = lax.dot_general(h_scr[...], w_ref[...], (((1,), (1,)), ((), ())),
                          preferred_element_type=f32)
    n_rope = COL_V // TN_PROJ
    n_q = COL_K // TN_PROJ

    @pl.when(n < n_rope)
    def _():
        c = jnp.where(n < n_q, Q_SCALE_A, 1.0).astype(f32)
        cos = cos_ref[...] * c
        sin = sin_ref[...] * c
        for j in range(TN_PROJ // LANES):
            blk = acc[:, j * LANES:(j + 1) * LANES]
            rot = pltpu.roll(blk, LANES // 2, 1)
            o_ref[:, j * LANES:(j + 1) * LANES] = (blk * cos + rot * sin).astype(bf16)

    @pl.when(n >= n_rope)
    def _():
        o_ref[...] = acc.astype(bf16)


def _in_proj(x2d, g, w_pad, cos_t, sin_t):
    return pl.pallas_call(
        _in_proj_kernel,
        grid=(N_TOK // TM_PROJ, PROJ_W // TN_PROJ),
        in_specs=[pl.BlockSpec((TM_PROJ, D_MODEL), lambda i, n: (i, 0)),
                  pl.BlockSpec((1, D_MODEL), lambda i, n: (0, 0)),
                  pl.BlockSpec((TN_PROJ, D_MODEL), lambda i, n: (n, 0)),
                  pl.BlockSpec((TM_PROJ, LANES), lambda i, n: (i, 0)),
                  pl.BlockSpec((TM_PROJ, LANES), lambda i, n: (i, 0))],
        out_specs=pl.BlockSpec((TM_PROJ, TN_PROJ), lambda i, n: (i, n)),
        out_shape=jax.ShapeDtypeStruct((N_TOK, PROJ_W), bf16),
        scratch_shapes=[pltpu.VMEM((TM_PROJ, D_MODEL), bf16)],
        compiler_params=_cparams(("arbitrary", "arbitrary")),
        name="in_proj",
    )(x2d, g, w_pad, cos_t, sin_t)


def _mla_prep_kernel(cq_ref, ckv_ref, kr_ref, gq_ref, gkv_ref, wuq_ref, wkv_ref, cos_ref, sin_ref,
                     q_ref, k_ref, v_ref):
    cqn = _rms(cq_ref[...].astype(f32), gq_ref[...]).astype(bf16)
    q = jnp.dot(cqn, wuq_ref[...], preferred_element_type=f32)
    ckvn = _rms(ckv_ref[...].astype(f32), gkv_ref[...]).astype(bf16)
    kv = jnp.dot(ckvn, wkv_ref[...], preferred_element_type=f32)
    cos = cos_ref[...]
    sin = sin_ref[...]
    lane = lax.broadcasted_iota(i32, (TM_PREP, LANES), 1)
    first = lane < (KR_LANE + MLA_ROPE // 2)
    rope = lambda b: _rope_lanes(b, cos, sin, first, MLA_ROPE // 2)
    k_rot = rope(kr_ref[...].astype(f32))
    for h in range(HEADS):
        sl = slice(h * HEAD_W, (h + 1) * HEAD_W)
        q_ref[:, sl] = (rope(q[:, sl]) * Q_SCALE_B).astype(bf16)
        k_ref[:, sl] = (kv[:, sl] + k_rot).astype(bf16)
    v_ref[...] = kv[:, HEADS * HEAD_W:].astype(bf16)


def _mla_prep(proj, gq, gkv, wuq_p, wkv_p, cos_t, sin_t):
    tm = TM_PREP
    w = HEADS * HEAD_W
    return pl.pallas_call(
        _mla_prep_kernel,
        grid=(N_TOK // tm,),
        in_specs=[pl.BlockSpec((tm, MLA_Q_RANK), lambda i: (i, COL_CQ // MLA_Q_RANK)),
                  pl.BlockSpec((tm, MLA_KV_RANK), lambda i: (i, COL_CKV // MLA_KV_RANK)),
                  pl.BlockSpec((tm, LANES), lambda i: (i, COL_KR // LANES)),
                  pl.BlockSpec((1, MLA_Q_RANK), lambda i: (0, 0)),
                  pl.BlockSpec((1, MLA_KV_RANK), lambda i: (0, 0)),
                  pl.BlockSpec((MLA_Q_RANK, w), lambda i: (0, 0)),
                  pl.BlockSpec((MLA_KV_RANK, 2 * w), lambda i: (0, 0)),
                  pl.BlockSpec((tm, LANES), lambda i: (i, 1)),
                  pl.BlockSpec((tm, LANES), lambda i: (i, 1))],
        out_specs=[pl.BlockSpec((tm, w), lambda i: (i, 0))] * 3,
        out_shape=[jax.ShapeDtypeStruct((N_TOK, w), bf16)] * 3,
        compiler_params=_cparams(("arbitrary",)),
        name="mla_prep",
    )(proj, proj, proj, gq, gkv, wuq_p, wkv_p, cos_t, sin_t)


VT_ROWS = HEAD_W + 16


def _attn_kernel(*refs, ncomp, heads, qtiles):
    if ncomp == 2:
        q_ref, k_ref, v_ref, lam_ref, g_ref, o_ref, acc_scr, vt_scr = refs
    else:
        q_ref, k_ref, v_ref, o_ref, acc_scr, vt_scr = refs
    qi = pl.program_id(2)

    @pl.when(qi == 0)
    def _():
        for j in range(SEQ // TQ):
            for h in range(heads):
                vt_scr[j, h, :HEAD_W, :] = v_ref[j * TQ:(j + 1) * TQ, h * HEAD_W:(h + 1) * HEAD_W].T
                vt_scr[j, h, HEAD_W:, :] = jnp.ones((VT_ROWS - HEAD_W, TQ), bf16)

    lane = lax.broadcasted_iota(i32, (TQ, HEAD_W), 1)
    qs = []
    for t in range(qtiles):
        for h in range(heads):
            q = q_ref[t * TQ:(t + 1) * TQ, h * HEAD_W:(h + 1) * HEAD_W]
            if ncomp == 2:
                zero = jnp.zeros_like(q)
                in_a = (lane % DA_HALF) < (DA_HALF // 2)
                qs += [jnp.where(in_a, q, zero), jnp.where(in_a, zero, q)]
            else:
                qs.append(q)
    per_tile = heads * ncomp
    head_of = lambda n: (n % per_tile) // ncomp

    def step(j, carry, modes):
        off = pl.multiple_of(j * TQ, TQ)
        live = [n for n in range(len(qs)) if modes[n // per_tile] != "skip"]
        ss = {}
        for n in live:
            h = head_of(n)
            k = k_ref[pl.ds(off, TQ), h * HEAD_W:(h + 1) * HEAD_W]
            s = lax.dot_general(k, qs[n], (((1,), (1,)), ((), ())), preferred_element_type=f32)
            if modes[n // per_tile] == "diag":
                key = lax.broadcasted_iota(i32, (TQ, TQ), 0)
                qry = lax.broadcasted_iota(i32, (TQ, TQ), 1)
                s = jnp.where(key <= qry, s, NEG_INF)
            ss[n] = s
        stats = {}
        for n in live:
            m, l = carry[n]
            s = ss[n]
            m_new = jnp.maximum(m, jnp.max(s, axis=0, keepdims=True))
            alpha = jnp.exp2(m - m_new)
            p = jnp.exp2(s - m_new)
            stats[n] = (m_new, alpha * l, alpha, p.astype(bf16))
        out = list(carry)
        for n in live:
            m_new, l_scaled, alpha, p = stats[n]
            pv = jnp.dot(vt_scr[j, head_of(n)], p, preferred_element_type=f32)
            acc_scr[n] = alpha * acc_scr[n] + pv[:HEAD_W, :]
            out[n] = (m_new, l_scaled + pv[HEAD_W:HEAD_W + 1, :])
        return tuple(out)

    acc_scr[...] = jnp.zeros_like(acc_scr)
    init = tuple((jnp.full((1, TQ), NEG_INF, f32), jnp.zeros((1, TQ), f32)) for _ in qs)
    first = qi * qtiles
    carry = lax.fori_loop(0, first, lambda j, c: step(j, c, ("full",) * qtiles), init)
    for u in range(qtiles):
        modes = tuple("skip" if t < u else ("diag" if t == u else "full") for t in range(qtiles))
        carry = step(first + u, carry, modes)
    carry = tuple((m, l, acc_scr[n]) for n, (m, l) in enumerate(carry))

    if ncomp == 2:
        ll = lam_ref[...]
        s1 = jnp.sum(ll[0:1, :] * ll[1:2, :], axis=-1, keepdims=True)
        s2 = jnp.sum(ll[2:3, :] * ll[3:4, :], axis=-1, keepdims=True)
        lam = jnp.exp(s1) - jnp.exp(s2) + LAM_INIT
    for t in range(qtiles):
        for h in range(heads):
            n0 = (t * heads + h) * ncomp
            if ncomp == 2:
                (_, l0, a0), (_, l1, a1) = carry[n0], carry[n0 + 1]
                o = a0 / l0 - lam * (a1 / l1)
                ms = jnp.mean(o * o, axis=0, keepdims=True)
                o = o * lax.rsqrt(ms + NORM_EPS) * g_ref[...] * (1.0 - LAM_INIT)
            else:
                _, l0, a0 = carry[n0]
                o = a0 / l0
            o_ref[t * TQ:(t + 1) * TQ, h * HEAD_W:(h + 1) * HEAD_W] = o.T.astype(bf16)


def _attention(q_arr, k_arr, v_arr, q_blk0, k_blk0, v_blk0, heads, qtiles, extra=()):
    ncomp = 2 if extra else 1
    tq = qtiles * TQ
    nq = SEQ // tq
    w = heads * HEAD_W
    qb, kb, vb = q_blk0 // heads, k_blk0 // heads, v_blk0 // heads
    in_specs = [pl.BlockSpec((tq, w), lambda b, h, i: (b * nq + i, qb + h)),
                pl.BlockSpec((SEQ, w), lambda b, h, i: (b, kb + h)),
                pl.BlockSpec((SEQ, w), lambda b, h, i: (b, vb + h))]
    for a in extra:
        in_specs.append(pl.BlockSpec(a.shape, lambda b, h, i: (0, 0)))
    return pl.pallas_call(
        functools.partial(_attn_kernel, ncomp=ncomp, heads=heads, qtiles=qtiles),
        grid=(BATCH, HEADS // heads, nq),
        in_specs=in_specs,
        out_specs=pl.BlockSpec((tq, w), lambda b, h, i: (b * nq + i, h)),
        out_shape=jax.ShapeDtypeStruct((N_TOK, HEADS * HEAD_W), bf16),
        scratch_shapes=[pltpu.VMEM((qtiles * heads * ncomp, HEAD_W, TQ), f32),
                        pltpu.VMEM((SEQ // TQ, heads, VT_ROWS, TQ), bf16)],
        compiler_params=_cparams(("arbitrary", "arbitrary", "arbitrary")),
        name="diff_attn" if extra else "mla_attn",
    )(q_arr, k_arr, v_arr, *extra)


RT_GROUP_ROW = 0
RT_EXPERT_ROW = 8


def _route(hf, w_ref, b_ref, eid_ref, cw_ref):
    hf_hi = hf.astype(bf16)
    hf_lo = (hf - hf_hi.astype(f32)).astype(bf16)
    a = jnp.dot(hf_hi, w_ref[...], preferred_element_type=f32)
    b = jnp.dot(hf_lo, w_ref[:, :LANES], preferred_element_type=f32)
    logits = a[:, :LANES] + (a[:, LANES:] + b) + b_ref[...]
    lt = logits.T
    tm = lt.shape[1]
    ridx = lax.broadcasted_iota(i32, (EXPERTS_PER_GROUP, tm), 0).astype(f32)

    def top1(v):
        vmax = jnp.max(v, axis=0, keepdims=True)
        idx = jnp.min(jnp.where(v == vmax, ridx, float(EXPERTS_PER_GROUP)), axis=0, keepdims=True)
        return vmax, idx

    gl = lt[RT_GROUP_ROW:RT_GROUP_ROW + 8, :]
    gmax, gsel = top1(gl)
    p_g = 1.0 / jnp.sum(jnp.exp(gl - gmax), axis=0, keepdims=True)
    el = jnp.zeros((EXPERTS_PER_GROUP, tm), f32)
    for g in range(N_GROUPS):
        r0 = RT_EXPERT_ROW + g * EXPERTS_PER_GROUP
        el = jnp.where(gsel == float(g), lt[r0:r0 + EXPERTS_PER_GROUP, :], el)
    e1, i1 = top1(el)
    e2, i2 = top1(jnp.where(ridx == i1, -jnp.inf, el))
    r = jnp.exp(e2 - e1)
    w1 = 1.0 / (1.0 + r)
    base = gsel * EXPERTS_PER_GROUP
    eid_ref[0:1, :] = (base + i1).astype(i32)
    eid_ref[1:2, :] = (base + i2).astype(i32)
    cw_ref[0:1, :] = p_g * w1
    cw_ref[1:2, :] = p_g * (r * w1)


def _merge_kernel(oa_ref, ob_ref, ga_ref, gb_ref, x_ref, wa_ref, wb_ref, wo_ref, gf_ref, wr_ref, br_ref,
                  x1_ref, hf_ref, eid_ref, cw_ref):
    ya = jnp.dot(oa_ref[...], wa_ref[...], preferred_element_type=f32)
    yb = jnp.dot(ob_ref[...], wb_ref[...], preferred_element_type=f32)
    mixed = (jax.nn.sigmoid(ga_ref[...].astype(f32)) * ya
             + jax.nn.sigmoid(gb_ref[...].astype(f32)) * yb)
    x1 = x_ref[...] + jnp.dot(mixed.astype(bf16), wo_ref[...], preferred_element_type=f32)
    x1_ref[...] = x1
    hf = _rms(x1, gf_ref[...])
    _slab_store(hf_ref, (), hf)
    _route(hf, wr_ref, br_ref, eid_ref, cw_ref)


def _merge(oa, ob, proj, x2d, wa, wb, wo, gf, w_rt, b_rt):
    tm = TM_MERGE
    w = HEADS * HEAD_W
    const = lambda shape: pl.BlockSpec(shape, lambda i: (0, 0), pipeline_mode=pl.Buffered(1))
    return pl.pallas_call(
        _merge_kernel,
        grid=(N_TOK // tm,),
        in_specs=[pl.BlockSpec((tm, w), lambda i: (i, 0)),
                  pl.BlockSpec((tm, w), lambda i: (i, 0)),
                  pl.BlockSpec((tm, D_MODEL), lambda i: (i, COL_GA // D_MODEL)),
                  pl.BlockSpec((tm, D_MODEL), lambda i: (i, COL_GB // D_MODEL)),
                  pl.BlockSpec((tm, D_MODEL), lambda i: (i, 0)),
                  const((w, D_MODEL)), const((w, D_MODEL)), const((D_MODEL, D_MODEL)),
                  const((1, D_MODEL)), const((D_MODEL, 2 * LANES)), const((1, LANES))],
        out_specs=[pl.BlockSpec((tm, D_MODEL), lambda i: (i, 0)),
                   pl.BlockSpec((tm * SLAB, LANES), lambda i: (i, 0)),
                   pl.BlockSpec((TOP_K, tm), lambda i: (0, i)),
                   pl.BlockSpec((TOP_K, tm), lambda i: (0, i))],
        out_shape=[jax.ShapeDtypeStruct((N_TOK, D_MODEL), f32),
                   jax.ShapeDtypeStruct((N_TOK * SLAB, LANES), f32),
                   jax.ShapeDtypeStruct((TOP_K, N_TOK), i32),
                   jax.ShapeDtypeStruct((TOP_K, N_TOK), f32)],
        compiler_params=_cparams(("arbitrary",)),
        name="merge_route",
    )(oa, ob, proj, proj, x2d, wa, wb, wo, gf, w_rt, b_rt)


PLAN_N = 128


def _plan_kernel(eid_ref, dest_ref, ends_ref):
    eid = eid_ref[...]
    r = lax.broadcasted_iota(i32, (PLAN_N, PLAN_N), 0)
    c = lax.broadcasted_iota(i32, (PLAN_N, PLAN_N), 1)
    upper = (r < c).astype(bf16)
    lower = (c < r).astype(bf16)
    ones = jnp.ones((PLAN_N, PLAN_N), bf16)
    lane = lax.broadcasted_iota(i32, (1, LANES), 1)
    dest = jnp.zeros((PLAN_N, PLAN_N), f32)
    ends = jnp.zeros((1, LANES), f32)
    start = jnp.zeros((1, 1), f32)
    for e in range(N_EXPERTS):
        m = (eid == e).astype(f32)
        mb = m.astype(bf16)
        within = jnp.dot(mb, upper, preferred_element_type=f32)
        rowtot = jnp.dot(mb, ones, preferred_element_type=f32)
        rowpre = jnp.dot(lower, rowtot.astype(bf16), preferred_element_type=f32)
        cnt = jnp.sum(jnp.sum(m, axis=1, keepdims=True), axis=0, keepdims=True)
        dest = dest + m * (start + within + rowpre)
        start = start + jnp.floor((cnt + (ROW_BLK - 1)) * (1.0 / ROW_BLK)) * ROW_BLK
        ends = jnp.where(lane == e, start, ends)
    dest_ref[...] = dest.astype(i32)
    ends_ref[...] = ends.astype(i32)


def _plan(eid):
    whole = lambda shape: pl.BlockSpec(shape, lambda: (0, 0))
    return pl.pallas_call(
        _plan_kernel,
        in_specs=[whole((PLAN_N, PLAN_N))],
        out_specs=[whole((PLAN_N, PLAN_N)), whole((1, LANES))],
        out_shape=[jax.ShapeDtypeStruct((PLAN_N, PLAN_N), i32),
                   jax.ShapeDtypeStruct((1, LANES), i32)],
        name="moe_plan",
    )(eid.reshape(PLAN_N, PLAN_N))


def _dispatch_kernel(ends_ref, dest_ref, hf_ref, xs_hbm, zbuf, sem):
    i = pl.program_id(0)

    def tail_copy(e):
        pend = ends_ref[e]
        prev = jnp.where(e > 0, ends_ref[jnp.maximum(e - 1, 0)], 0)
        row0 = pl.multiple_of(jnp.maximum(pend - ROW_BLK, 0), ROW_BLK)
        return pend > prev, pltpu.make_async_copy(zbuf, xs_hbm.at[_slab_rows(row0, ROW_BLK), :], sem)

    @pl.when(i == 0)
    def _():
        zbuf[...] = jnp.zeros_like(zbuf)

        def zstart(e, _):
            used, cp = tail_copy(e)

            @pl.when(used)
            def _():
                cp.start()
            return 0

        def zwait(e, _):
            used, cp = tail_copy(e)

            @pl.when(used)
            def _():
                cp.wait()
            return 0

        def unused_copy(b):
            row0 = pl.multiple_of(b * ROW_BLK, ROW_BLK)
            return pltpu.make_async_copy(zbuf, xs_hbm.at[_slab_rows(row0, ROW_BLK), :], sem)

        def ustart(b, _):
            unused_copy(b).start()
            return 0

        def uwait(b, _):
            unused_copy(b).wait()
            return 0

        n_used = ends_ref[N_EXPERTS - 1] // ROW_BLK
        lax.fori_loop(0, N_EXPERTS, zstart, 0)
        lax.fori_loop(n_used, N_BLKS, ustart, 0)
        lax.fori_loop(0, N_EXPERTS, zwait, 0)
        lax.fori_loop(n_used, N_BLKS, uwait, 0)

    base = i * TT_DISP

    def row_copy(t, k):
        d = dest_ref[k, base + t]
        return pltpu.make_async_copy(hf_ref.at[_slab_rows(t, 1), :], xs_hbm.at[_slab_rows(d, 1), :], sem)

    def issue(t, _):
        for k in range(TOP_K):
            row_copy(t, k).start(priority=k)
        return 0

    def drain(t, _):
        for k in range(TOP_K):
            row_copy(t, k).wait()
        return 0

    lax.fori_loop(0, TT_DISP, issue, 0, unroll=ISSUE_UNROLL)
    lax.fori_loop(0, TT_DISP, drain, 0, unroll=ISSUE_UNROLL)


def _dispatch(ends, dest, hf):
    return pl.pallas_call(
        _dispatch_kernel,
        grid_spec=pltpu.PrefetchScalarGridSpec(
            num_scalar_prefetch=2,
            grid=(N_TOK // TT_DISP,),
            in_specs=[pl.BlockSpec((TT_DISP * SLAB, LANES), lambda i, e, d: (i, 0))],
            out_specs=pl.BlockSpec(memory_space=pl.ANY),
            scratch_shapes=[pltpu.VMEM((ROW_BLK * SLAB, LANES), f32),
                            pltpu.SemaphoreType.DMA(())]),
        out_shape=jax.ShapeDtypeStruct((N_ROWS * SLAB, LANES), f32),
        compiler_params=_cparams(("arbitrary",)),
        name="moe_dispatch",
    )(ends, dest, hf)


W_RING = 3
X_RING = 6
Y_RING = 3


def _ffn_kernel(ends_ref, wg_hbm, wu_hbm, wd_hbm, xs_hbm, y_hbm,
                wg_f, wu_f, wd_f, wg_s, wu_s, wd_s, xbuf, ybuf, sem_w, sem_x, sem_y):
    n_used = ends_ref[N_EXPERTS - 1] // ROW_BLK

    def w_copies(ex):
        slot = ex % W_RING
        return [pltpu.make_async_copy(hbm.at[ex], buf.at[slot], sem_w.at[i, slot])
                for i, (hbm, buf) in enumerate(((wg_hbm, wg_f), (wu_hbm, wu_f), (wd_hbm, wd_f)))]

    def advance(e_from, e_to):
        def one(ex, _):
            for cp in w_copies(ex):
                cp.wait()

            @pl.when(ex + (W_RING - 1) < N_EXPERTS)
            def _():
                for cp in w_copies(ex + (W_RING - 1)):
                    cp.start()
            return 0

        lax.fori_loop(e_from + 1, e_to + 1, one, 0)

    def x_copy(b):
        slot = b % X_RING
        return pltpu.make_async_copy(xs_hbm.at[_slab_rows(b * ROW_BLK, ROW_BLK), :], xbuf.at[slot],
                                     sem_x.at[slot])

    def y_copy(b):
        slot = b % Y_RING
        return pltpu.make_async_copy(ybuf.at[slot], y_hbm.at[_slab_rows(b * ROW_BLK, ROW_BLK), :],
                                     sem_y.at[slot])

    for ex in range(W_RING - 1):
        for cp in w_copies(ex):
            cp.start()
    for b0 in range(X_RING - 1):
        @pl.when(b0 < n_used)
        def _():
            x_copy(b0).start()

    def body(b, e):
        @pl.when(b + (X_RING - 1) < n_used)
        def _():
            x_copy(b + (X_RING - 1)).start()

        e_new = lax.while_loop(lambda ex: b * ROW_BLK >= ends_ref[ex], lambda ex: ex + 1,
                               jnp.maximum(e, 0))

        @pl.when(e_new != e)
        def _():
            advance(e, e_new)
            wslot = e_new % W_RING
            wg_s[...] = wg_f[wslot].astype(bf16)
            wu_s[...] = wu_f[wslot].astype(bf16)
            wd_s[...] = wd_f[wslot].astype(bf16)

        x_copy(b).wait()

        @pl.when(b >= Y_RING)
        def _():
            y_copy(b - Y_RING).wait()

        x = _slab_load(xbuf, (b % X_RING,), ROW_BLK).astype(bf16)
        h1 = jnp.dot(x, wg_s[...], preferred_element_type=f32)
        h2 = jnp.dot(x, wu_s[...], preferred_element_type=f32)
        a = (h1 * jax.nn.sigmoid(h1) * h2).astype(bf16)
        _slab_store(ybuf, (b % Y_RING,), jnp.dot(a, wd_s[...], preferred_element_type=f32))
        y_copy(b).start()
        return e_new

    e_last = lax.fori_loop(0, n_used, body, jnp.int32(-1))
    advance(e_last, jnp.int32(N_EXPERTS - 1))

    for back in range(Y_RING, 0, -1):
        @pl.when(n_used >= back)
        def _():
            y_copy(n_used - back).wait()

    ybuf[0] = jnp.zeros((ROW_BLK * SLAB, LANES), f32)

    def unused_copy(b):
        return pltpu.make_async_copy(ybuf.at[0], y_hbm.at[_slab_rows(b * ROW_BLK, ROW_BLK), :],
                                     sem_y.at[0])

    def ustart(b, _):
        unused_copy(b).start()
        return 0

    def uwait(b, _):
        unused_copy(b).wait()
        return 0

    lax.fori_loop(n_used, N_BLKS, ustart, 0)
    lax.fori_loop(n_used, N_BLKS, uwait, 0)


def _expert_ffn(ends, xs, wg, wu, wd):
    return pl.pallas_call(
        _ffn_kernel,
        grid_spec=pltpu.PrefetchScalarGridSpec(
            num_scalar_prefetch=1,
            grid=(1,),
            in_specs=[pl.BlockSpec(memory_space=pl.ANY)] * 4,
            out_specs=pl.BlockSpec(memory_space=pl.ANY),
            scratch_shapes=[pltpu.VMEM((W_RING, D_MODEL, EXPERT_FF), f32),
                            pltpu.VMEM((W_RING, D_MODEL, EXPERT_FF), f32),
                            pltpu.VMEM((W_RING, EXPERT_FF, D_MODEL), f32),
                            pltpu.VMEM((D_MODEL, EXPERT_FF), bf16),
                            pltpu.VMEM((D_MODEL, EXPERT_FF), bf16),
                            pltpu.VMEM((EXPERT_FF, D_MODEL), bf16),
                            pltpu.VMEM((X_RING, ROW_BLK * SLAB, LANES), f32),
                            pltpu.VMEM((Y_RING, ROW_BLK * SLAB, LANES), f32),
                            pltpu.SemaphoreType.DMA((3, W_RING)),
                            pltpu.SemaphoreType.DMA((X_RING,)),
                            pltpu.SemaphoreType.DMA((Y_RING,))]),
        out_shape=jax.ShapeDtypeStruct((N_ROWS * SLAB, LANES), f32),
        compiler_params=_cparams(("arbitrary",)),
        name="expert_ffn",
    )(ends, wg, wu, wd, xs)


def _combine_kernel(dest_ref, x_ref, cw_ref, g_ref, y_hbm, o_ref, ybuf, sem):
    i = pl.program_id(0)
    n_steps = pl.num_programs(0)

    def row_copy(step, t, k):
        slot = step % 2
        d = dest_ref[k, step * TM_COMB + t]
        return pltpu.make_async_copy(y_hbm.at[_slab_rows(d, 1), :],
                                     ybuf.at[slot, k, _slab_rows(t, 1), :], sem.at[slot])

    def issue(step):
        def one(t, _):
            for k in range(TOP_K):
                row_copy(step, t, k).start()
            return 0
        lax.fori_loop(0, TM_COMB, one, 0, unroll=ISSUE_UNROLL)

    def drain(step):
        def one(t, _):
            for k in range(TOP_K):
                row_copy(step, t, k).wait()
            return 0
        lax.fori_loop(0, TM_COMB, one, 0, unroll=ISSUE_UNROLL)

    @pl.when(i == 0)
    def _():
        issue(i)

    @pl.when(i + 1 < n_steps)
    def _():
        issue(i + 1)

    drain(i)
    slot = i % 2
    w = cw_ref[...]
    x = (x_ref[...] + w[:, 0:1] * _slab_load(ybuf, (slot, 0), TM_COMB)
         + w[:, 1:2] * _slab_load(ybuf, (slot, 1), TM_COMB))
    o_ref[...] = _rms(x, g_ref[...])


def _combine(dest, x1, cw_t, g, y):
    tm = TM_COMB
    return pl.pallas_call(
        _combine_kernel,
        grid_spec=pltpu.PrefetchScalarGridSpec(
            num_scalar_prefetch=1,
            grid=(N_TOK // tm,),
            in_specs=[pl.BlockSpec((tm, D_MODEL), lambda i, d: (i, 0)),
                      pl.BlockSpec((tm, TOP_K), lambda i, d: (i, 0)),
                      pl.BlockSpec((1, D_MODEL), lambda i, d: (0, 0)),
                      pl.BlockSpec(memory_space=pl.ANY)],
            out_specs=pl.BlockSpec((tm, D_MODEL), lambda i, d: (i, 0)),
            scratch_shapes=[pltpu.VMEM((2, TOP_K, tm * SLAB, LANES), f32),
                            pltpu.SemaphoreType.DMA((2,))]),
        out_shape=jax.ShapeDtypeStruct((N_TOK, D_MODEL), f32),
        compiler_params=_cparams(("arbitrary",)),
        name="moe_combine",
    )(dest, x1, cw_t, g, y)


TR_RELAYOUT = 256
KR_TILE = COL_KR // TR_RELAYOUT


def _w_in_relayout_kernel(w_ref, o_ref):
    i = pl.program_id(0)

    @pl.when(i < COL_V // TR_RELAYOUT)
    def _():
        q = ROPE_NA
        for g in range(TR_RELAYOUT // LANES):
            r = g * LANES
            o_ref[r:r + q, :] = w_ref[r:r + q, :].astype(bf16)
            o_ref[r + q:r + 2 * q, :] = w_ref[r + 2 * q:r + 3 * q, :].astype(bf16)
            o_ref[r + 2 * q:r + 3 * q, :] = w_ref[r + q:r + 2 * q, :].astype(bf16)
            o_ref[r + 3 * q:r + 4 * q, :] = w_ref[r + 3 * q:r + 4 * q, :].astype(bf16)

    @pl.when((i >= COL_V // TR_RELAYOUT) & (i != KR_TILE))
    def _():
        o_ref[...] = w_ref[...].astype(bf16)

    @pl.when(i == KR_TILE)
    def _():
        o_ref[...] = jnp.zeros_like(o_ref)
        o_ref[KR_LANE:KR_LANE + MLA_ROPE, :] = w_ref[0:MLA_ROPE, :].astype(bf16)


def _prep_w_in(w_in_t):
    n_src, d = w_in_t.shape
    gate_src = COL_KR + MLA_ROPE

    gate_tile = COL_GA // TR_RELAYOUT
    sub = 8
    shift = (COL_GA - gate_src) // sub

    def src_row(i):
        return ((TR_RELAYOUT // sub) * i - shift * (i // gate_tile)) * sub

    return pl.pallas_call(
        _w_in_relayout_kernel,
        grid=(PROJ_W // TR_RELAYOUT,),
        in_specs=[pl.BlockSpec((pl.Element(TR_RELAYOUT), pl.Element(d)), lambda i: (src_row(i), 0))],
        out_specs=pl.BlockSpec((TR_RELAYOUT, d), lambda i: (i, 0)),
        out_shape=jax.ShapeDtypeStruct((PROJ_W, d), bf16),
        compiler_params=_cparams(("arbitrary",)),
        name="w_in_relayout",
    )(w_in_t)


def _prep_w_uq(w_uq):
    w = w_uq.reshape(MLA_Q_RANK, HEADS, MLA_NOPE + MLA_ROPE)
    w = jnp.pad(w, ((0, 0), (0, 0), (0, HEAD_W - MLA_NOPE - MLA_ROPE)))
    return w.reshape(MLA_Q_RANK, HEADS * HEAD_W).astype(bf16)


def _prep_w_ukv(w_ukv):
    w = w_ukv.reshape(MLA_KV_RANK, HEADS, MLA_NOPE + HEAD_W)
    wk = jnp.pad(w[:, :, :MLA_NOPE], ((0, 0), (0, 0), (0, HEAD_W - MLA_NOPE)))
    wv = w[:, :, MLA_NOPE:]
    return jnp.concatenate([wk.reshape(MLA_KV_RANK, -1), wv.reshape(MLA_KV_RANK, -1)], axis=1).astype(bf16)


def _prep_router(w_group, b_group, w_router, b_router):
    z = lambda n: jnp.zeros((D_MODEL, n), f32)
    w = jnp.concatenate([w_group, z(RT_EXPERT_ROW - N_GROUPS), w_router,
                         z(LANES - RT_EXPERT_ROW - N_EXPERTS)], axis=1)
    b = jnp.concatenate([b_group, jnp.full((RT_EXPERT_ROW - N_GROUPS,), NEG_INF, f32), b_router,
                         jnp.zeros((LANES - RT_EXPERT_ROW - N_EXPERTS,), f32)])[None, :]
    w_hi = w.astype(bf16)
    w_lo = (w - w_hi.astype(f32)).astype(bf16)
    return jnp.concatenate([w_hi, w_lo], axis=1), b


def kernel(x, positions, attn_norm, w_in, da_lambda, da_subln, mla_q_norm, mla_w_uq, mla_kv_norm, mla_w_ukv, w_branch_a, w_branch_b, w_out, ffn_norm, w_group, b_group, w_router, b_router, w_exp_gate, w_exp_up, w_exp_down, final_norm):
    x2d = x.reshape(N_TOK, D_MODEL)
    cos_t, sin_t = _rope_tables(positions)

    proj = _in_proj(x2d, attn_norm[0][None, :], _prep_w_in(w_in[0].T), cos_t, sin_t)
    qb, kb, vb = _mla_prep(proj, mla_q_norm[0][None, :], mla_kv_norm[0][None, :],
                           _prep_w_uq(mla_w_uq[0]), _prep_w_ukv(mla_w_ukv[0]), cos_t, sin_t)
    oa = _attention(proj, proj, proj, COL_Q // HEAD_W, COL_K // HEAD_W, COL_V // HEAD_W,
                    heads=8, qtiles=2, extra=(da_lambda[0], da_subln[0][:, None]))
    ob = _attention(qb, kb, vb, 0, 0, 0, heads=8, qtiles=4)
    w_rt, b_rt = _prep_router(w_group[0], b_group[0], w_router[0], b_router[0])
    x1, hf, eid, cw = _merge(oa, ob, proj, x2d, w_branch_a[0].astype(bf16), w_branch_b[0].astype(bf16),
                             w_out[0].astype(bf16), ffn_norm[0][None, :], w_rt, b_rt)

    dest, ends = _plan(eid)
    dest = dest.reshape(TOP_K, N_TOK)
    ends = ends[0, :N_EXPERTS]
    xs = _dispatch(ends, dest, hf)
    y = _expert_ffn(ends, xs, w_exp_gate[0], w_exp_up[0], w_exp_down[0])
    out = _combine(dest, x1, cw.T, final_norm[None, :], y)
    return out.reshape(BATCH, SEQ, D_MODEL)
```

```python
import functools
import math

import jax
import jax.numpy as jnp
from jax import lax
from jax.experimental import pallas as pl
from jax.experimental.pallas import tpu as pltpu

f32 = jnp.float32
bf16 = jnp.bfloat16
i32 = jnp.int32

D_MODEL = 2048
BATCH = 4
SEQ = 2048
N_TOK = BATCH * SEQ
HEADS = 8
HEAD_W = 128
DA_HALF = 64
MLA_Q_RANK = 512
MLA_KV_RANK = 256
MLA_NOPE = 64
MLA_ROPE = 32
N_GROUPS = 4
EXPERTS_PER_GROUP = 8
N_EXPERTS = N_GROUPS * EXPERTS_PER_GROUP
TOP_K = 2
EXPERT_FF = 512
ROPE_THETA = 10000.0
NORM_EPS = 1e-6
NEG_INF = -1e30
LAM_INIT = 0.8 - 0.6 * math.exp(-0.3 * 0)
LOG2E = math.log2(math.e)

LANES = 128
VMEM_LIMIT = 56 * 1024 * 1024

COL_Q, COL_K, COL_V = 0, 1024, 2048
COL_CQ, COL_CKV = 3072, 3584
COL_KR = 3840
COL_GA, COL_GB = 4096, 6144
PROJ_W = 8192
KR_LANE = MLA_NOPE

TM_PROJ, TN_PROJ = 1024, 1024
TM_PREP = 512
TQ = 256
TM_MERGE = 256
ROW_BLK = 128
N_ROWS = (N_TOK * TOP_K // ROW_BLK + N_EXPERTS) * ROW_BLK
N_BLKS = N_ROWS // ROW_BLK
TT_DISP = 1024
TM_COMB = 128

Q_SCALE_A = (DA_HALF ** -0.5) * LOG2E
Q_SCALE_B = ((MLA_NOPE + MLA_ROPE) ** -0.5) * LOG2E


def _cparams(sem, **kw):
    return pltpu.CompilerParams(dimension_semantics=sem, vmem_limit_bytes=VMEM_LIMIT, **kw)


def _rms(x, g):
    return x * lax.rsqrt(jnp.mean(x * x, axis=-1, keepdims=True) + NORM_EPS) * g


SLAB = D_MODEL // LANES


ISSUE_UNROLL = 16


def _slab_rows(row0, nrows):
    return pl.ds(pl.multiple_of(row0 * SLAB, SLAB), nrows * SLAB)


def _slab_store(ref, idx, val):
    rows = val.shape[0]
    for j in range(SLAB):
        ref[idx + (pl.ds(j, rows, stride=SLAB), slice(None))] = val[:, j * LANES:(j + 1) * LANES]


def _slab_chunk(ref, idx, rows, j):
    return ref[idx + (pl.ds(j, rows, stride=SLAB), slice(None))]


def _slab_load(ref, idx, rows):
    return jnp.concatenate([_slab_chunk(ref, idx, rows, j) for j in range(SLAB)], axis=1)


def _rope_lanes(blk, cos, sin, first_half, half):
    up = pltpu.roll(blk, LANES - half, 1)
    down = pltpu.roll(blk, half, 1)
    return blk * cos + jnp.where(first_half, up, down) * sin


ROPE_NA = DA_HALF // 2
ROPE_NB = MLA_ROPE // 2


def _rope_tables_kernel(pos_ref, inv_ref, cos_ref, sin_ref):
    half = pos_ref.shape[0] // 2
    lane = lax.broadcasted_iota(i32, (half, LANES), 1)
    pos = jnp.where(lane < LANES // 2, pos_ref[:half, :], pos_ref[half:, :]).astype(f32)
    ang = pos * inv_ref[...]
    c = jnp.cos(ang)
    s = jnp.sin(ang)
    one = lambda n: jnp.ones((half, n), f32)
    zero = lambda n: jnp.zeros((half, n), f32)
    tail = LANES - KR_LANE - MLA_ROPE
    for part in range(2):
        o = part * (LANES // 2)
        rows = slice(part * half, (part + 1) * half)
        ca, sa = c[:, o:o + ROPE_NA], s[:, o:o + ROPE_NA]
        cb, sb = c[:, o + ROPE_NA:o + ROPE_NA + ROPE_NB], s[:, o + ROPE_NA:o + ROPE_NA + ROPE_NB]
        cos_ref[rows, :] = jnp.concatenate([ca, ca, ca, ca, one(KR_LANE), cb, cb, one(tail)], axis=1)
        sin_ref[rows, :] = jnp.concatenate([-sa, -sa, sa, sa, zero(KR_LANE), -sb, sb, zero(tail)],
                                           axis=1)


def _rope_tables(positions):
    inv_a = ROPE_THETA ** (-jnp.arange(0, DA_HALF, 2, dtype=f32) / DA_HALF)
    inv_b = ROPE_THETA ** (-jnp.arange(0, MLA_ROPE, 2, dtype=f32) / MLA_ROPE)
    inv = jnp.concatenate([inv_a, inv_b, jnp.zeros((LANES // 2 - ROPE_NA - ROPE_NB,), f32)])
    inv = jnp.tile(inv, 2)[None, :]
    pos = positions.reshape(N_TOK, 1)
    tm = 1024
    return pl.pallas_call(
        _rope_tables_kernel,
        grid=(N_TOK // tm,),
        in_specs=[pl.BlockSpec((tm, 1), lambda i: (i, 0)),
                  pl.BlockSpec((1, LANES), lambda i: (0, 0))],
        out_specs=[pl.BlockSpec((tm, 2 * LANES), lambda i: (i, 0)),
                   pl.BlockSpec((tm, 2 * LANES), lambda i: (i, 0))],
        out_shape=[jax.ShapeDtypeStruct((N_TOK, 2 * LANES), f32)] * 2,
        compiler_params=_cparams(("arbitrary",)),
        name="rope_tables",
    )(pos, inv)


def _in_proj_kernel(x_ref, g_ref, w_ref, cos_ref, sin_ref, o_ref, h_scr):
    n = pl.program_id(1)

    @pl.when(n == 0)
    def _():
        h_scr[...] = _rms(x_ref[...], g_ref[...]).astype(bf16)

    acc = lax.dot_general(h_scr[...], w_ref[...], (((1,), (1,)), ((), ())),
                          preferred_element_type=f32)
    n_rope = COL_V // TN_PROJ
    n_q = COL_K // TN_PROJ

    @pl.when(n < n_rope)
    def _():
        c = jnp.where(n < n_q, Q_SCALE_A, 1.0).astype(f32)
        cos = cos_ref[...] * c
        sin = sin_ref[...] * c
        for j in range(TN_PROJ // LANES):
            blk = acc[:, j * LANES:(j + 1) * LANES]
            rot = pltpu.roll(blk, LANES // 2, 1)
            o_ref[:, j * LANES:(j + 1) * LANES] = (blk * cos + rot * sin).astype(bf16)

    @pl.when(n >= n_rope)
    def _():
        o_ref[...] = acc.astype(bf16)


def _in_proj(x2d, g, w_pad, cos_t, sin_t):
    return pl.pallas_call(
        _in_proj_kernel,
        grid=(N_TOK // TM_PROJ, PROJ_W // TN_PROJ),
        in_specs=[pl.BlockSpec((TM_PROJ, D_MODEL), lambda i, n: (i, 0)),
                  pl.BlockSpec((1, D_MODEL), lambda i, n: (0, 0)),
                  pl.BlockSpec((TN_PROJ, D_MODEL), lambda i, n: (n, 0)),
                  pl.BlockSpec((TM_PROJ, LANES), lambda i, n: (i, 0)),
                  pl.BlockSpec((TM_PROJ, LANES), lambda i, n: (i, 0))],
        out_specs=pl.BlockSpec((TM_PROJ, TN_PROJ), lambda i, n: (i, n)),
        out_shape=jax.ShapeDtypeStruct((N_TOK, PROJ_W), bf16),
        scratch_shapes=[pltpu.VMEM((TM_PROJ, D_MODEL), bf16)],
        compiler_params=_cparams(("arbitrary", "arbitrary")),
        name="in_proj",
    )(x2d, g, w_pad, cos_t, sin_t)


def _mla_prep_kernel(cq_ref, ckv_ref, kr_ref, gq_ref, gkv_ref, wuq_ref, wkv_ref, cos_ref, sin_ref,
                     q_ref, k_ref, v_ref):
    cqn = _rms(cq_ref[...].astype(f32), gq_ref[...]).astype(bf16)
    q = jnp.dot(cqn, wuq_ref[...], preferred_element_type=f32)
    ckvn = _rms(ckv_ref[...].astype(f32), gkv_ref[...]).astype(bf16)
    kv = jnp.dot(ckvn, wkv_ref[...], preferred_element_type=f32)
    cos = cos_ref[...]
    sin = sin_ref[...]
    lane = lax.broadcasted_iota(i32, (TM_PREP, LANES), 1)
    first = lane < (KR_LANE + MLA_ROPE // 2)
    rope = lambda b: _rope_lanes(b, cos, sin, first, MLA_ROPE // 2)
    k_rot = rope(kr_ref[...].astype(f32))
    for h in range(HEADS):
        sl = slice(h * HEAD_W, (h + 1) * HEAD_W)
        q_ref[:, sl] = (rope(q[:, sl]) * Q_SCALE_B).astype(bf16)
        k_ref[:, sl] = (kv[:, sl] + k_rot).astype(bf16)
    v_ref[...] = kv[:, HEADS * HEAD_W:].astype(bf16)


def _mla_prep(proj, gq, gkv, wuq_p, wkv_p, cos_t, sin_t):
    tm = TM_PREP
    w = HEADS * HEAD_W
    return pl.pallas_call(
        _mla_prep_kernel,
        grid=(N_TOK // tm,),
        in_specs=[pl.BlockSpec((tm, MLA_Q_RANK), lambda i: (i, COL_CQ // MLA_Q_RANK)),
                  pl.BlockSpec((tm, MLA_KV_RANK), lambda i: (i, COL_CKV // MLA_KV_RANK)),
                  pl.BlockSpec((tm, LANES), lambda i: (i, COL_KR // LANES)),
                  pl.BlockSpec((1, MLA_Q_RANK), lambda i: (0, 0)),
                  pl.BlockSpec((1, MLA_KV_RANK), lambda i: (0, 0)),
                  pl.BlockSpec((MLA_Q_RANK, w), lambda i: (0, 0)),
                  pl.BlockSpec((MLA_KV_RANK, 2 * w), lambda i: (0, 0)),
                  pl.BlockSpec((tm, LANES), lambda i: (i, 1)),
                  pl.BlockSpec((tm, LANES), lambda i: (i, 1))],
        out_specs=[pl.BlockSpec((tm, w), lambda i: (i, 0))] * 3,
        out_shape=[jax.ShapeDtypeStruct((N_TOK, w), bf16)] * 3,
        compiler_params=_cparams(("arbitrary",)),
        name="mla_prep",
    )(proj, proj, proj, gq, gkv, wuq_p, wkv_p, cos_t, sin_t)


VT_ROWS = HEAD_W + 16


def _attn_kernel(*refs, ncomp, heads, qtiles):
    if ncomp == 2:
        q_ref, k_ref, v_ref, lam_ref, g_ref, o_ref, acc_scr, vt_scr = refs
    else:
        q_ref, k_ref, v_ref, o_ref, acc_scr, vt_scr = refs
    qi = pl.program_id(2)

    @pl.when(qi == 0)
    def _():
        for j in range(SEQ // TQ):
            for h in range(heads):
                vt_scr[j, h, :HEAD_W, :] = v_ref[j * TQ:(j + 1) * TQ, h * HEAD_W:(h + 1) * HEAD_W].T
                vt_scr[j, h, HEAD_W:, :] = jnp.ones((VT_ROWS - HEAD_W, TQ), bf16)

    lane = lax.broadcasted_iota(i32, (TQ, HEAD_W), 1)
    qs = []
    for t in range(qtiles):
        for h in range(heads):
            q = q_ref[t * TQ:(t + 1) * TQ, h * HEAD_W:(h + 1) * HEAD_W]
            if ncomp == 2:
                zero = jnp.zeros_like(q)
                in_a = (lane % DA_HALF) < (DA_HALF // 2)
                qs += [jnp.where(in_a, q, zero), jnp.where(in_a, zero, q)]
            else:
                qs.append(q)
    per_tile = heads * ncomp
    head_of = lambda n: (n % per_tile) // ncomp

    def step(j, carry, modes):
        off = pl.multiple_of(j * TQ, TQ)
        live = [n for n in range(len(qs)) if modes[n // per_tile] != "skip"]
        ss = {}
        for n in live:
            h = head_of(n)
            k = k_ref[pl.ds(off, TQ), h * HEAD_W:(h + 1) * HEAD_W]
            s = lax.dot_general(k, qs[n], (((1,), (1,)), ((), ())), preferred_element_type=f32)
            if modes[n // per_tile] == "diag":
                key = lax.broadcasted_iota(i32, (TQ, TQ), 0)
                qry = lax.broadcasted_iota(i32, (TQ, TQ), 1)
                s = jnp.where(key <= qry, s, NEG_INF)
            ss[n] = s
        stats = {}
        for n in live:
            m, l = carry[n]
            s = ss[n]
            m_new = jnp.maximum(m, jnp.max(s, axis=0, keepdims=True))
            alpha = jnp.exp2(m - m_new)
            p = jnp.exp2(s - m_new)
            stats[n] = (m_new, alpha * l, alpha, p.astype(bf16))
        out = list(carry)
        for n in live:
            m_new, l_scaled, alpha, p = stats[n]
            pv = jnp.dot(vt_scr[j, head_of(n)], p, preferred_element_type=f32)
            acc_scr[n] = alpha * acc_scr[n] + pv[:HEAD_W, :]
            out[n] = (m_new, l_scaled + pv[HEAD_W:HEAD_W + 1, :])
        return tuple(out)

    acc_scr[...] = jnp.zeros_like(acc_scr)
    init = tuple((jnp.full((1, TQ), NEG_INF, f32), jnp.zeros((1, TQ), f32)) for _ in qs)
    first = qi * qtiles
    carry = lax.fori_loop(0, first, lambda j, c: step(j, c, ("full",) * qtiles), init)
    for u in range(qtiles):
        modes = tuple("skip" if t < u else ("diag" if t == u else "full") for t in range(qtiles))
        carry = step(first + u, carry, modes)
    carry = tuple((m, l, acc_scr[n]) for n, (m, l) in enumerate(carry))

    if ncomp == 2:
        ll = lam_ref[...]
        s1 = jnp.sum(ll[0:1, :] * ll[1:2, :], axis=-1, keepdims=True)
        s2 = jnp.sum(ll[2:3, :] * ll[3:4, :], axis=-1, keepdims=True)
        lam = jnp.exp(s1) - jnp.exp(s2) + LAM_INIT
    for t in range(qtiles):
        for h in range(heads):
            n0 = (t * heads + h) * ncomp
            if ncomp == 2:
                (_, l0, a0), (_, l1, a1) = carry[n0], carry[n0 + 1]
                o = a0 / l0 - lam * (a1 / l1)
                ms = jnp.mean(o * o, axis=0, keepdims=True)
                o = o * lax.rsqrt(ms + NORM_EPS) * g_ref[...] * (1.0 - LAM_INIT)
            else:
                _, l0, a0 = carry[n0]
                o = a0 / l0
            o_ref[t * TQ:(t + 1) * TQ, h * HEAD_W:(h + 1) * HEAD_W] = o.T.astype(bf16)


def _attention(q_arr, k_arr, v_arr, q_blk0, k_blk0, v_blk0, heads, qtiles, extra=()):
    ncomp = 2 if extra else 1
    tq = qtiles * TQ
    nq = SEQ // tq
    w = heads * HEAD_W
    qb, kb, vb = q_blk0 // heads, k_blk0 // heads, v_blk0 // heads
    in_specs = [pl.BlockSpec((tq, w), lambda b, h, i: (b * nq + i, qb + h)),
                pl.BlockSpec((SEQ, w), lambda b, h, i: (b, kb + h)),
                pl.BlockSpec((SEQ, w), lambda b, h, i: (b, vb + h))]
    for a in extra:
        in_specs.append(pl.BlockSpec(a.shape, lambda b, h, i: (0, 0)))
    return pl.pallas_call(
        functools.partial(_attn_kernel, ncomp=ncomp, heads=heads, qtiles=qtiles),
        grid=(BATCH, HEADS // heads, nq),
        in_specs=in_specs,
        out_specs=pl.BlockSpec((tq, w), lambda b, h, i: (b * nq + i, h)),
        out_shape=jax.ShapeDtypeStruct((N_TOK, HEADS * HEAD_W), bf16),
        scratch_shapes=[pltpu.VMEM((qtiles * heads * ncomp, HEAD_W, TQ), f32),
                        pltpu.VMEM((SEQ // TQ, heads, VT_ROWS, TQ), bf16)],
        compiler_params=_cparams(("arbitrary", "arbitrary", "arbitrary")),
        name="diff_attn" if extra else "mla_attn",
    )(q_arr, k_arr, v_arr, *extra)


RT_GROUP_ROW = 0
RT_EXPERT_ROW = 8


def _route(hf, w_ref, b_ref, eid_ref, cw_ref):
    hf_hi = hf.astype(bf16)
    hf_lo = (hf - hf_hi.astype(f32)).astype(bf16)
    a = jnp.dot(hf_hi, w_ref[...], preferred_element_type=f32)
    b = jnp.dot(hf_lo, w_ref[:, :LANES], preferred_element_type=f32)
    logits = a[:, :LANES] + (a[:, LANES:] + b) + b_ref[...]
    lt = logits.T
    tm = lt.shape[1]
    ridx = lax.broadcasted_iota(i32, (EXPERTS_PER_GROUP, tm), 0).astype(f32)

    def top1(v):
        vmax = jnp.max(v, axis=0, keepdims=True)
        idx = jnp.min(jnp.where(v == vmax, ridx, float(EXPERTS_PER_GROUP)), axis=0, keepdims=True)
        return vmax, idx

    gl = lt[RT_GROUP_ROW:RT_GROUP_ROW + 8, :]
    gmax, gsel = top1(gl)
    p_g = 1.0 / jnp.sum(jnp.exp(gl - gmax), axis=0, keepdims=True)
    el = jnp.zeros((EXPERTS_PER_GROUP, tm), f32)
    for g in range(N_GROUPS):
        r0 = RT_EXPERT_ROW + g * EXPERTS_PER_GROUP
        el = jnp.where(gsel == float(g), lt[r0:r0 + EXPERTS_PER_GROUP, :], el)
    e1, i1 = top1(el)
    e2, i2 = top1(jnp.where(ridx == i1, -jnp.inf, el))
    r = jnp.exp(e2 - e1)
    w1 = 1.0 / (1.0 + r)
    base = gsel * EXPERTS_PER_GROUP
    eid_ref[0:1, :] = (base + i1).astype(i32)
    eid_ref[1:2, :] = (base + i2).astype(i32)
    cw_ref[0:1, :] = p_g * w1
    cw_ref[1:2, :] = p_g * (r * w1)


def _merge_kernel(oa_ref, ob_ref, ga_ref, gb_ref, x_ref, wa_ref, wb_ref, wo_ref, gf_ref, wr_ref, br_ref,
                  x1_ref, hf_ref, eid_ref, cw_ref):
    ya = jnp.dot(oa_ref[...], wa_ref[...], preferred_element_type=f32)
    yb = jnp.dot(ob_ref[...], wb_ref[...], preferred_element_type=f32)
    mixed = (jax.nn.sigmoid(ga_ref[...].astype(f32)) * ya
             + jax.nn.sigmoid(gb_ref[...].astype(f32)) * yb)
    x1 = x_ref[...] + jnp.dot(mixed.astype(bf16), wo_ref[...], preferred_element_type=f32)
    x1_ref[...] = x1
    hf = _rms(x1, gf_ref[...])
    _slab_store(hf_ref, (), hf)
    _route(hf, wr_ref, br_ref, eid_ref, cw_ref)


def _merge(oa, ob, proj, x2d, wa, wb, wo, gf, w_rt, b_rt):
    tm = TM_MERGE
    w = HEADS * HEAD_W
    const = lambda shape: pl.BlockSpec(shape, lambda i: (0, 0), pipeline_mode=pl.Buffered(1))
    return pl.pallas_call(
        _merge_kernel,
        grid=(N_TOK // tm,),
        in_specs=[pl.BlockSpec((tm, w), lambda i: (i, 0)),
                  pl.BlockSpec((tm, w), lambda i: (i, 0)),
                  pl.BlockSpec((tm, D_MODEL), lambda i: (i, COL_GA // D_MODEL)),
                  pl.BlockSpec((tm, D_MODEL), lambda i: (i, COL_GB // D_MODEL)),
                  pl.BlockSpec((tm, D_MODEL), lambda i: (i, 0)),
                  const((w, D_MODEL)), const((w, D_MODEL)), const((D_MODEL, D_MODEL)),
                  const((1, D_MODEL)), const((D_MODEL, 2 * LANES)), const((1, LANES))],
        out_specs=[pl.BlockSpec((tm, D_MODEL), lambda i: (i, 0)),
                   pl.BlockSpec((tm * SLAB, LANES), lambda i: (i, 0)),
                   pl.BlockSpec((TOP_K, tm), lambda i: (0, i)),
                   pl.BlockSpec((TOP_K, tm), lambda i: (0, i))],
        out_shape=[jax.ShapeDtypeStruct((N_TOK, D_MODEL), f32),
                   jax.ShapeDtypeStruct((N_TOK * SLAB, LANES), f32),
                   jax.ShapeDtypeStruct((TOP_K, N_TOK), i32),
                   jax.ShapeDtypeStruct((TOP_K, N_TOK), f32)],
        compiler_params=_cparams(("arbitrary",)),
        name="merge_route",
    )(oa, ob, proj, proj, x2d, wa, wb, wo, gf, w_rt, b_rt)


PLAN_N = 128


def _plan_kernel(eid_ref, dest_ref, ends_ref):
    eid = eid_ref[...]
    r = lax.broadcasted_iota(i32, (PLAN_N, PLAN_N), 0)
    c = lax.broadcasted_iota(i32, (PLAN_N, PLAN_N), 1)
    upper = (r < c).astype(bf16)
    lower = (c < r).astype(bf16)
    ones = jnp.ones((PLAN_N, PLAN_N), bf16)
    lane = lax.broadcasted_iota(i32, (1, LANES), 1)
    dest = jnp.zeros((PLAN_N, PLAN_N), f32)
    ends = jnp.zeros((1, LANES), f32)
    start = jnp.zeros((1, 1), f32)
    for e in range(N_EXPERTS):
        m = (eid == e).astype(f32)
        mb = m.astype(bf16)
        within = jnp.dot(mb, upper, preferred_element_type=f32)
        rowtot = jnp.dot(mb, ones, preferred_element_type=f32)
        rowpre = jnp.dot(lower, rowtot.astype(bf16), preferred_element_type=f32)
        cnt = jnp.sum(jnp.sum(m, axis=1, keepdims=True), axis=0, keepdims=True)
        dest = dest + m * (start + within + rowpre)
        start = start + jnp.floor((cnt + (ROW_BLK - 1)) * (1.0 / ROW_BLK)) * ROW_BLK
        ends = jnp.where(lane == e, start, ends)
    dest_ref[...] = dest.astype(i32)
    ends_ref[...] = ends.astype(i32)


def _plan(eid):
    whole = lambda shape: pl.BlockSpec(shape, lambda: (0, 0))
    return pl.pallas_call(
        _plan_kernel,
        in_specs=[whole((PLAN_N, PLAN_N))],
        out_specs=[whole((PLAN_N, PLAN_N)), whole((1, LANES))],
        out_shape=[jax.ShapeDtypeStruct((PLAN_N, PLAN_N), i32),
                   jax.ShapeDtypeStruct((1, LANES), i32)],
        name="moe_plan",
    )(eid.reshape(PLAN_N, PLAN_N))


def _dispatch_kernel(ends_ref, dest_ref, hf_ref, xs_hbm, zbuf, sem):
    i = pl.program_id(0)

    def tail_copy(e):
        pend = ends_ref[e]
        prev = jnp.where(e > 0, ends_ref[jnp.maximum(e - 1, 0)], 0)
        row0 = pl.multiple_of(jnp.maximum(pend - ROW_BLK, 0), ROW_BLK)
        return pend > prev, pltpu.make_async_copy(zbuf, xs_hbm.at[_slab_rows(row0, ROW_BLK), :], sem)

    @pl.when(i == 0)
    def _():
        zbuf[...] = jnp.zeros_like(zbuf)

        def zstart(e, _):
            used, cp = tail_copy(e)

            @pl.when(used)
            def _():
                cp.start()
            return 0

        def zwait(e, _):
            used, cp = tail_copy(e)

            @pl.when(used)
            def _():
                cp.wait()
            return 0

        def unused_copy(b):
            row0 = pl.multiple_of(b * ROW_BLK, ROW_BLK)
            return pltpu.make_async_copy(zbuf, xs_hbm.at[_slab_rows(row0, ROW_BLK), :], sem)

        def ustart(b, _):
            unused_copy(b).start()
            return 0

        def uwait(b, _):
            unused_copy(b).wait()
            return 0

        n_used = ends_ref[N_EXPERTS - 1] // ROW_BLK
        lax.fori_loop(0, N_EXPERTS, zstart, 0)
        lax.fori_loop(n_used, N_BLKS, ustart, 0)
        lax.fori_loop(0, N_EXPERTS, zwait, 0)
        lax.fori_loop(n_used, N_BLKS, uwait, 0)

    base = i * TT_DISP

    def row_copy(t, k):
        d = dest_ref[k, base + t]
        return pltpu.make_async_copy(hf_ref.at[_slab_rows(t, 1), :], xs_hbm.at[_slab_rows(d, 1), :], sem)

    def issue(t, _):
        for k in range(TOP_K):
            row_copy(t, k).start(priority=k)
        return 0

    def drain(t, _):
        for k in range(TOP_K):
            row_copy(t, k).wait()
        return 0

    lax.fori_loop(0, TT_DISP, issue, 0, unroll=ISSUE_UNROLL)
    lax.fori_loop(0, TT_DISP, drain, 0, unroll=ISSUE_UNROLL)


def _dispatch(ends, dest, hf):
    return pl.pallas_call(
        _dispatch_kernel,
        grid_spec=pltpu.PrefetchScalarGridSpec(
            num_scalar_prefetch=2,
            grid=(N_TOK // TT_DISP,),
            in_specs=[pl.BlockSpec((TT_DISP * SLAB, LANES), lambda i, e, d: (i, 0))],
            out_specs=pl.BlockSpec(memory_space=pl.ANY),
            scratch_shapes=[pltpu.VMEM((ROW_BLK * SLAB, LANES), f32),
                            pltpu.SemaphoreType.DMA(())]),
        out_shape=jax.ShapeDtypeStruct((N_ROWS * SLAB, LANES), f32),
        compiler_params=_cparams(("arbitrary",)),
        name="moe_dispatch",
    )(ends, dest, hf)


W_RING = 3
X_RING = 6
Y_RING = 3


def _ffn_kernel(ends_ref, wg_hbm, wu_hbm, wd_hbm, xs_hbm, y_hbm,
                wg_f, wu_f, wd_f, wg_s, wu_s, wd_s, xbuf, ybuf, sem_w, sem_x, sem_y):
    n_used = ends_ref[N_EXPERTS - 1] // ROW_BLK

    def w_copies(ex):
        slot = ex % W_RING
        return [pltpu.make_async_copy(hbm.at[ex], buf.at[slot], sem_w.at[i, slot])
                for i, (hbm, buf) in enumerate(((wg_hbm, wg_f), (wu_hbm, wu_f), (wd_hbm, wd_f)))]

    def advance(e_from, e_to):
        def one(ex, _):
            for cp in w_copies(ex):
                cp.wait()

            @pl.when(ex + (W_RING - 1) < N_EXPERTS)
            def _():
                for cp in w_copies(ex + (W_RING - 1)):
                    cp.start()
            return 0

        lax.fori_loop(e_from + 1, e_to + 1, one, 0)

    def x_copy(b):
        slot = b % X_RING
        return pltpu.make_async_copy(xs_hbm.at[_slab_rows(b * ROW_BLK, ROW_BLK), :], xbuf.at[slot],
                                     sem_x.at[slot])

    def y_copy(b):
        slot = b % Y_RING
        return pltpu.make_async_copy(ybuf.at[slot], y_hbm.at[_slab_rows(b * ROW_BLK, ROW_BLK), :],
                                     sem_y.at[slot])

    for ex in range(W_RING - 1):
        for cp in w_copies(ex):
            cp.start()
    for b0 in range(X_RING - 1):
        @pl.when(b0 < n_used)
        def _():
            x_copy(b0).start()

    def body(b, e):
        @pl.when(b + (X_RING - 1) < n_used)
        def _():
            x_copy(b + (X_RING - 1)).start()

        e_new = lax.while_loop(lambda ex: b * ROW_BLK >= ends_ref[ex], lambda ex: ex + 1,
                               jnp.maximum(e, 0))

        @pl.when(e_new != e)
        def _():
            advance(e, e_new)
            wslot = e_new % W_RING
            wg_s[...] = wg_f[wslot].astype(bf16)
            wu_s[...] = wu_f[wslot].astype(bf16)
            wd_s[...] = wd_f[wslot].astype(bf16)

        x_copy(b).wait()

        @pl.when(b >= Y_RING)
        def _():
            y_copy(b - Y_RING).wait()

        x = _slab_load(xbuf, (b % X_RING,), ROW_BLK).astype(bf16)
        h1 = jnp.dot(x, wg_s[...], preferred_element_type=f32)
        h2 = jnp.dot(x, wu_s[...], preferred_element_type=f32)
        a = (h1 * jax.nn.sigmoid(h1) * h2).astype(bf16)
        _slab_store(ybuf, (b % Y_RING,), jnp.dot(a, wd_s[...], preferred_element_type=f32))
        y_copy(b).start()
        return e_new

    e_last = lax.fori_loop(0, n_used, body, jnp.int32(-1))
    advance(e_last, jnp.int32(N_EXPERTS - 1))

    for back in range(Y_RING, 0, -1):
        @pl.when(n_used >= back)
        def _():
            y_copy(n_used - back).wait()

    ybuf[0] = jnp.zeros((ROW_BLK * SLAB, LANES), f32)

    def unused_copy(b):
        return pltpu.make_async_copy(ybuf.at[0], y_hbm.at[_slab_rows(b * ROW_BLK, ROW_BLK), :],
                                     sem_y.at[0])

    def ustart(b, _):
        unused_copy(b).start()
        return 0

    def uwait(b, _):
        unused_copy(b).wait()
        return 0

    lax.fori_loop(n_used, N_BLKS, ustart, 0)
    lax.fori_loop(n_used, N_BLKS, uwait, 0)


def _expert_ffn(ends, xs, wg, wu, wd):
    return pl.pallas_call(
        _ffn_kernel,
        grid_spec=pltpu.PrefetchScalarGridSpec(
            num_scalar_prefetch=1,
            grid=(1,),
            in_specs=[pl.BlockSpec(memory_space=pl.ANY)] * 4,
            out_specs=pl.BlockSpec(memory_space=pl.ANY),
            scratch_shapes=[pltpu.VMEM((W_RING, D_MODEL, EXPERT_FF), f32),
                            pltpu.VMEM((W_RING, D_MODEL, EXPERT_FF), f32),
                            pltpu.VMEM((W_RING, EXPERT_FF, D_MODEL), f32),
                            pltpu.VMEM((D_MODEL, EXPERT_FF), bf16),
                            pltpu.VMEM((D_MODEL, EXPERT_FF), bf16),
                            pltpu.VMEM((EXPERT_FF, D_MODEL), bf16),
                            pltpu.VMEM((X_RING, ROW_BLK * SLAB, LANES), f32),
                            pltpu.VMEM((Y_RING, ROW_BLK * SLAB, LANES), f32),
                            pltpu.SemaphoreType.DMA((3, W_RING)),
                            pltpu.SemaphoreType.DMA((X_RING,)),
                            pltpu.SemaphoreType.DMA((Y_RING,))]),
        out_shape=jax.ShapeDtypeStruct((N_ROWS * SLAB, LANES), f32),
        compiler_params=_cparams(("arbitrary",)),
        name="expert_ffn",
    )(ends, wg, wu, wd, xs)


def _combine_kernel(dest_ref, x_ref, cw_ref, g_ref, y_hbm, o_ref, ybuf, sem):
    i = pl.program_id(0)
    n_steps = pl.num_programs(0)

    def row_copy(step, t, k):
        slot = step % 2
        d = dest_ref[k, step * TM_COMB + t]
        return pltpu.make_async_copy(y_hbm.at[_slab_rows(d, 1), :],
                                     ybuf.at[slot, k, _slab_rows(t, 1), :], sem.at[slot])

    def issue(step):
        def one(t, _):
            for k in range(TOP_K):
                row_copy(step, t, k).start()
            return 0
        lax.fori_loop(0, TM_COMB, one, 0, unroll=ISSUE_UNROLL)

    def drain(step):
        def one(t, _):
            for k in range(TOP_K):
                row_copy(step, t, k).wait()
            return 0
        lax.fori_loop(0, TM_COMB, one, 0, unroll=ISSUE_UNROLL)

    @pl.when(i == 0)
    def _():
        issue(i)

    @pl.when(i + 1 < n_steps)
    def _():
        issue(i + 1)

    drain(i)
    slot = i % 2
    w = cw_ref[...]
    x = (x_ref[...] + w[:, 0:1] * _slab_load(ybuf, (slot, 0), TM_COMB)
         + w[:, 1:2] * _slab_load(ybuf, (slot, 1), TM_COMB))
    o_ref[...] = _rms(x, g_ref[...])


def _combine(dest, x1, cw_t, g, y):
    tm = TM_COMB
    return pl.pallas_call(
        _combine_kernel,
        grid_spec=pltpu.PrefetchScalarGridSpec(
            num_scalar_prefetch=1,
            grid=(N_TOK // tm,),
            in_specs=[pl.BlockSpec((tm, D_MODEL), lambda i, d: (i, 0)),
                      pl.BlockSpec((tm, TOP_K), lambda i, d: (i, 0)),
                      pl.BlockSpec((1, D_MODEL), lambda i, d: (0, 0)),
                      pl.BlockSpec(memory_space=pl.ANY)],
            out_specs=pl.BlockSpec((tm, D_MODEL), lambda i, d: (i, 0)),
            scratch_shapes=[pltpu.VMEM((2, TOP_K, tm * SLAB, LANES), f32),
                            pltpu.SemaphoreType.DMA((2,))]),
        out_shape=jax.ShapeDtypeStruct((N_TOK, D_MODEL), f32),
        compiler_params=_cparams(("arbitrary",)),
        name="moe_combine",
    )(dest, x1, cw_t, g, y)


TR_RELAYOUT = 256
KR_TILE = COL_KR // TR_RELAYOUT


def _w_in_relayout_kernel(w_ref, o_ref):
    i = pl.program_id(0)

    @pl.when(i < COL_V // TR_RELAYOUT)
    def _():
        q = ROPE_NA
        for g in range(TR_RELAYOUT // LANES):
            r = g * LANES
            o_ref[r:r + q, :] = w_ref[r:r + q, :].astype(bf16)
            o_ref[r + q:r + 2 * q, :] = w_ref[r + 2 * q:r + 3 * q, :].astype(bf16)
            o_ref[r + 2 * q:r + 3 * q, :] = w_ref[r + q:r + 2 * q, :].astype(bf16)
            o_ref[r + 3 * q:r + 4 * q, :] = w_ref[r + 3 * q:r + 4 * q, :].astype(bf16)

    @pl.when((i >= COL_V // TR_RELAYOUT) & (i != KR_TILE))
    def _():
        o_ref[...] = w_ref[...].astype(bf16)

    @pl.when(i == KR_TILE)
    def _():
        o_ref[...] = jnp.zeros_like(o_ref)
        o_ref[KR_LANE:KR_LANE + MLA_ROPE, :] = w_ref[0:MLA_ROPE, :].astype(bf16)


def _prep_w_in(w_in_t):
    n_src, d = w_in_t.shape
    gate_src = COL_KR + MLA_ROPE

    gate_tile = COL_GA // TR_RELAYOUT
    sub = 8
    shift = (COL_GA - gate_src) // sub

    def src_row(i):
        return ((TR_RELAYOUT // sub) * i - shift * (i // gate_tile)) * sub

    return pl.pallas_call(
        _w_in_relayout_kernel,
        grid=(PROJ_W // TR_RELAYOUT,),
        in_specs=[pl.BlockSpec((pl.Element(TR_RELAYOUT), pl.Element(d)), lambda i: (src_row(i), 0))],
        out_specs=pl.BlockSpec((TR_RELAYOUT, d), lambda i: (i, 0)),
        out_shape=jax.ShapeDtypeStruct((PROJ_W, d), bf16),
        compiler_params=_cparams(("arbitrary",)),
        name="w_in_relayout",
    )(w_in_t)


def _prep_w_uq(w_uq):
    w = w_uq.reshape(MLA_Q_RANK, HEADS, MLA_NOPE + MLA_ROPE)
    w = jnp.pad(w, ((0, 0), (0, 0), (0, HEAD_W - MLA_NOPE - MLA_ROPE)))
    return w.reshape(MLA_Q_RANK, HEADS * HEAD_W).astype(bf16)


def _prep_w_ukv(w_ukv):
    w = w_ukv.reshape(MLA_KV_RANK, HEADS, MLA_NOPE + HEAD_W)
    wk = jnp.pad(w[:, :, :MLA_NOPE], ((0, 0), (0, 0), (0, HEAD_W - MLA_NOPE)))
    wv = w[:, :, MLA_NOPE:]
    return jnp.concatenate([wk.reshape(MLA_KV_RANK, -1), wv.reshape(MLA_KV_RANK, -1)], axis=1).astype(bf16)


def _prep_router(w_group, b_group, w_router, b_router):
    z = lambda n: jnp.zeros((D_MODEL, n), f32)
    w = jnp.concatenate([w_group, z(RT_EXPERT_ROW - N_GROUPS), w_router,
                         z(LANES - RT_EXPERT_ROW - N_EXPERTS)], axis=1)
    b = jnp.concatenate([b_group, jnp.full((RT_EXPERT_ROW - N_GROUPS,), NEG_INF, f32), b_router,
                         jnp.zeros((LANES - RT_EXPERT_ROW - N_EXPERTS,), f32)])[None, :]
    w_hi = w.astype(bf16)
    w_lo = (w - w_hi.astype(f32)).astype(bf16)
    return jnp.concatenate([w_hi, w_lo], axis=1), b


def kernel(x, positions, attn_norm, w_in, da_lambda, da_subln, mla_q_norm, mla_w_uq, mla_kv_norm, mla_w_ukv, w_branch_a, w_branch_b, w_out, ffn_norm, w_group, b_group, w_router, b_router, w_exp_gate, w_exp_up, w_exp_down, final_norm):
    x2d = x.reshape(N_TOK, D_MODEL)
    cos_t, sin_t = _rope_tables(positions)

    proj = _in_proj(x2d, attn_norm[0][None, :], _prep_w_in(w_in[0].T), cos_t, sin_t)
    qb, kb, vb = _mla_prep(proj, mla_q_norm[0][None, :], mla_kv_norm[0][None, :],
                           _prep_w_uq(mla_w_uq[0]), _prep_w_ukv(mla_w_ukv[0]), cos_t, sin_t)
    oa = _attention(proj, proj, proj, COL_Q // HEAD_W, COL_K // HEAD_W, COL_V // HEAD_W,
                    heads=4, qtiles=4, extra=(da_lambda[0], da_subln[0][:, None]))
    ob = _attention(qb, kb, vb, 0, 0, 0, heads=8, qtiles=4)
    w_rt, b_rt = _prep_router(w_group[0], b_group[0], w_router[0], b_router[0])
    x1, hf, eid, cw = _merge(oa, ob, proj, x2d, w_branch_a[0].astype(bf16), w_branch_b[0].astype(bf16),
                             w_out[0].astype(bf16), ffn_norm[0][None, :], w_rt, b_rt)

    dest, ends = _plan(eid)
    dest = dest.reshape(TOP_K, N_TOK)
    ends = ends[0, :N_EXPERTS]
    xs = _dispatch(ends, dest, hf)
    y = _expert_ffn(ends, xs, w_exp_gate[0], w_exp_up[0], w_exp_down[0])
    out = _combine(dest, x1, cw.T, final_norm[None, :], y)
    return out.reshape(BATCH, SEQ, D_MODEL)
```

```python
import functools
import math

import jax
import jax.numpy as jnp
from jax import lax
from jax.experimental import pallas as pl
from jax.experimental.pallas import tpu as pltpu

f32 = jnp.float32
bf16 = jnp.bfloat16
i32 = jnp.int32

D_MODEL = 2048
BATCH = 4
SEQ = 2048
N_TOK = BATCH * SEQ
HEADS = 8
HEAD_W = 128
DA_HALF = 64
MLA_Q_RANK = 512
MLA_KV_RANK = 256
MLA_NOPE = 64
MLA_ROPE = 32
N_GROUPS = 4
EXPERTS_PER_GROUP = 8
N_EXPERTS = N_GROUPS * EXPERTS_PER_GROUP
TOP_K = 2
EXPERT_FF = 512
ROPE_THETA = 10000.0
NORM_EPS = 1e-6
NEG_INF = -1e30
LAM_INIT = 0.8 - 0.6 * math.exp(-0.3 * 0)
LOG2E = math.log2(math.e)

LANES = 128
VMEM_LIMIT = 56 * 1024 * 1024

COL_Q, COL_K, COL_V = 0, 1024, 2048
COL_CQ, COL_CKV = 3072, 3584
COL_KR = 3840
COL_GA, COL_GB = 4096, 6144
PROJ_W = 8192
KR_LANE = MLA_NOPE

TM_PROJ, TN_PROJ = 1024, 1024
TM_PREP = 512
TQ = 256
TM_MERGE = 256
ROW_BLK = 128
N_ROWS = (N_TOK * TOP_K // ROW_BLK + N_EXPERTS) * ROW_BLK
N_BLKS = N_ROWS // ROW_BLK
TT_DISP = 1024
TM_COMB = 128

Q_SCALE_A = (DA_HALF ** -0.5) * LOG2E
Q_SCALE_B = ((MLA_NOPE + MLA_ROPE) ** -0.5) * LOG2E


def _cparams(sem, **kw):
    return pltpu.CompilerParams(dimension_semantics=sem, vmem_limit_bytes=VMEM_LIMIT, **kw)


def _rms(x, g):
    return x * lax.rsqrt(jnp.mean(x * x, axis=-1, keepdims=True) + NORM_EPS) * g


SLAB = D_MODEL // LANES


ISSUE_UNROLL = 16


def _slab_rows(row0, nrows):
    return pl.ds(pl.multiple_of(row0 * SLAB, SLAB), nrows * SLAB)


def _slab_store(ref, idx, val):
    rows = val.shape[0]
    for j in range(SLAB):
        ref[idx + (pl.ds(j, rows, stride=SLAB), slice(None))] = val[:, j * LANES:(j + 1) * LANES]


def _slab_chunk(ref, idx, rows, j):
    return ref[idx + (pl.ds(j, rows, stride=SLAB), slice(None))]


def _slab_load(ref, idx, rows):
    return jnp.concatenate([_slab_chunk(ref, idx, rows, j) for j in range(SLAB)], axis=1)


def _rope_lanes(blk, cos, sin, first_half, half):
    up = pltpu.roll(blk, LANES - half, 1)
    down = pltpu.roll(blk, half, 1)
    return blk * cos + jnp.where(first_half, up, down) * sin


ROPE_NA = DA_HALF // 2
ROPE_NB = MLA_ROPE // 2


def _rope_tables_kernel(pos_ref, inv_ref, cos_ref, sin_ref):
    half = pos_ref.shape[0] // 2
    lane = lax.broadcasted_iota(i32, (half, LANES), 1)
    pos = jnp.where(lane < LANES // 2, pos_ref[:half, :], pos_ref[half:, :]).astype(f32)
    ang = pos * inv_ref[...]
    c = jnp.cos(ang)
    s = jnp.sin(ang)
    one = lambda n: jnp.ones((half, n), f32)
    zero = lambda n: jnp.zeros((half, n), f32)
    tail = LANES - KR_LANE - MLA_ROPE
    for part in range(2):
        o = part * (LANES // 2)
        rows = slice(part * half, (part + 1) * half)
        ca, sa = c[:, o:o + ROPE_NA], s[:, o:o + ROPE_NA]
        cb, sb = c[:, o + ROPE_NA:o + ROPE_NA + ROPE_NB], s[:, o + ROPE_NA:o + ROPE_NA + ROPE_NB]
        cos_ref[rows, :] = jnp.concatenate([ca, ca, ca, ca, one(KR_LANE), cb, cb, one(tail)], axis=1)
        sin_ref[rows, :] = jnp.concatenate([-sa, -sa, sa, sa, zero(KR_LANE), -sb, sb, zero(tail)],
                                           axis=1)


def _rope_tables(positions):
    inv_a = ROPE_THETA ** (-jnp.arange(0, DA_HALF, 2, dtype=f32) / DA_HALF)
    inv_b = ROPE_THETA ** (-jnp.arange(0, MLA_ROPE, 2, dtype=f32) / MLA_ROPE)
    inv = jnp.concatenate([inv_a, inv_b, jnp.zeros((LANES // 2 - ROPE_NA - ROPE_NB,), f32)])
    inv = jnp.tile(inv, 2)[None, :]
    pos = positions.reshape(N_TOK, 1)
    tm = 1024
    return pl.pallas_call(
        _rope_tables_kernel,
        grid=(N_TOK // tm,),
        in_specs=[pl.BlockSpec((tm, 1), lambda i: (i, 0)),
                  pl.BlockSpec((1, LANES), lambda i: (0, 0))],
        out_specs=[pl.BlockSpec((tm, 2 * LANES), lambda i: (i, 0)),
                   pl.BlockSpec((tm, 2 * LANES), lambda i: (i, 0))],
        out_shape=[jax.ShapeDtypeStruct((N_TOK, 2 * LANES), f32)] * 2,
        compiler_params=_cparams(("arbitrary",)),
        name="rope_tables",
    )(pos, inv)


def _in_proj_kernel(x_ref, g_ref, w_ref, cos_ref, sin_ref, o_ref, h_scr):
    n = pl.program_id(1)

    @pl.when(n == 0)
    def _():
        h_scr[...] = _rms(x_ref[...], g_ref[...]).astype(bf16)

    acc = lax.dot_general(h_scr[...], w_ref[...], (((1,), (1,)), ((), ())),
                          preferred_element_type=f32)
    n_rope = COL_V // TN_PROJ
    n_q = COL_K // TN_PROJ

    @pl.when(n < n_rope)
    def _():
        c = jnp.where(n < n_q, Q_SCALE_A, 1.0).astype(f32)
        cos = cos_ref[...] * c
        sin = sin_ref[...] * c
        for j in range(TN_PROJ // LANES):
            blk = acc[:, j * LANES:(j + 1) * LANES]
            rot = pltpu.roll(blk, LANES // 2, 1)
            o_ref[:, j * LANES:(j + 1) * LANES] = (blk * cos + rot * sin).astype(bf16)

    @pl.when(n >= n_rope)
    def _():
        o_ref[...] = acc.astype(bf16)


def _in_proj(x2d, g, w_pad, cos_t, sin_t):
    return pl.pallas_call(
        _in_proj_kernel,
        grid=(N_TOK // TM_PROJ, PROJ_W // TN_PROJ),
        in_specs=[pl.BlockSpec((TM_PROJ, D_MODEL), lambda i, n: (i, 0)),
                  pl.BlockSpec((1, D_MODEL), lambda i, n: (0, 0)),
                  pl.BlockSpec((TN_PROJ, D_MODEL), lambda i, n: (n, 0)),
                  pl.BlockSpec((TM_PROJ, LANES), lambda i, n: (i, 0)),
                  pl.BlockSpec((TM_PROJ, LANES), lambda i, n: (i, 0))],
        out_specs=pl.BlockSpec((TM_PROJ, TN_PROJ), lambda i, n: (i, n)),
        out_shape=jax.ShapeDtypeStruct((N_TOK, PROJ_W), bf16),
        scratch_shapes=[pltpu.VMEM((TM_PROJ, D_MODEL), bf16)],
        compiler_params=_cparams(("arbitrary", "arbitrary")),
        name="in_proj",
    )(x2d, g, w_pad, cos_t, sin_t)


def _mla_prep_kernel(cq_ref, ckv_ref, kr_ref, gq_ref, gkv_ref, wuq_ref, wkv_ref, cos_ref, sin_ref,
                     q_ref, k_ref, v_ref):
    cqn = _rms(cq_ref[...].astype(f32), gq_ref[...]).astype(bf16)
    q = jnp.dot(cqn, wuq_ref[...], preferred_element_type=f32)
    ckvn = _rms(ckv_ref[...].astype(f32), gkv_ref[...]).astype(bf16)
    kv = jnp.dot(ckvn, wkv_ref[...], preferred_element_type=f32)
    cos = cos_ref[...]
    sin = sin_ref[...]
    lane = lax.broadcasted_iota(i32, (TM_PREP, LANES), 1)
    first = lane < (KR_LANE + MLA_ROPE // 2)
    rope = lambda b: _rope_lanes(b, cos, sin, first, MLA_ROPE // 2)
    k_rot = rope(kr_ref[...].astype(f32))
    for h in range(HEADS):
        sl = slice(h * HEAD_W, (h + 1) * HEAD_W)
        q_ref[:, sl] = (rope(q[:, sl]) * Q_SCALE_B).astype(bf16)
        k_ref[:, sl] = (kv[:, sl] + k_rot).astype(bf16)
    v_ref[...] = kv[:, HEADS * HEAD_W:].astype(bf16)


def _mla_prep(proj, gq, gkv, wuq_p, wkv_p, cos_t, sin_t):
    tm = TM_PREP
    w = HEADS * HEAD_W
    return pl.pallas_call(
        _mla_prep_kernel,
        grid=(N_TOK // tm,),
        in_specs=[pl.BlockSpec((tm, MLA_Q_RANK), lambda i: (i, COL_CQ // MLA_Q_RANK)),
                  pl.BlockSpec((tm, MLA_KV_RANK), lambda i: (i, COL_CKV // MLA_KV_RANK)),
                  pl.BlockSpec((tm, LANES), lambda i: (i, COL_KR // LANES)),
                  pl.BlockSpec((1, MLA_Q_RANK), lambda i: (0, 0)),
                  pl.BlockSpec((1, MLA_KV_RANK), lambda i: (0, 0)),
                  pl.BlockSpec((MLA_Q_RANK, w), lambda i: (0, 0)),
                  pl.BlockSpec((MLA_KV_RANK, 2 * w), lambda i: (0, 0)),
                  pl.BlockSpec((tm, LANES), lambda i: (i, 1)),
                  pl.BlockSpec((tm, LANES), lambda i: (i, 1))],
        out_specs=[pl.BlockSpec((tm, w), lambda i: (i, 0))] * 3,
        out_shape=[jax.ShapeDtypeStruct((N_TOK, w), bf16)] * 3,
        compiler_params=_cparams(("arbitrary",)),
        name="mla_prep",
    )(proj, proj, proj, gq, gkv, wuq_p, wkv_p, cos_t, sin_t)


VT_ROWS = HEAD_W + 16


def _attn_kernel(*refs, ncomp, heads, qtiles):
    if ncomp == 2:
        q_ref, k_ref, v_ref, lam_ref, g_ref, o_ref, acc_scr, vt_scr = refs
    else:
        q_ref, k_ref, v_ref, o_ref, acc_scr, vt_scr = refs
    qi = pl.program_id(2)

    @pl.when(qi == 0)
    def _():
        for j in range(SEQ // TQ):
            for h in range(heads):
                vt_scr[j, h, :HEAD_W, :] = v_ref[j * TQ:(j + 1) * TQ, h * HEAD_W:(h + 1) * HEAD_W].T
                vt_scr[j, h, HEAD_W:, :] = jnp.ones((VT_ROWS - HEAD_W, TQ), bf16)

    lane = lax.broadcasted_iota(i32, (TQ, HEAD_W), 1)
    qs = []
    for t in range(qtiles):
        for h in range(heads):
            q = q_ref[t * TQ:(t + 1) * TQ, h * HEAD_W:(h + 1) * HEAD_W]
            if ncomp == 2:
                zero = jnp.zeros_like(q)
                in_a = (lane % DA_HALF) < (DA_HALF // 2)
                qs += [jnp.where(in_a, q, zero), jnp.where(in_a, zero, q)]
            else:
                qs.append(q)
    per_tile = heads * ncomp
    head_of = lambda n: (n % per_tile) // ncomp

    def step(j, carry, modes):
        off = pl.multiple_of(j * TQ, TQ)
        live = [n for n in range(len(qs)) if modes[n // per_tile] != "skip"]
        ss = {}
        for n in live:
            h = head_of(n)
            k = k_ref[pl.ds(off, TQ), h * HEAD_W:(h + 1) * HEAD_W]
            s = lax.dot_general(k, qs[n], (((1,), (1,)), ((), ())), preferred_element_type=f32)
            if modes[n // per_tile] == "diag":
                key = lax.broadcasted_iota(i32, (TQ, TQ), 0)
                qry = lax.broadcasted_iota(i32, (TQ, TQ), 1)
                s = jnp.where(key <= qry, s, NEG_INF)
            ss[n] = s
        stats = {}
        for n in live:
            m, l = carry[n]
            s = ss[n]
            m_new = jnp.maximum(m, jnp.max(s, axis=0, keepdims=True))
            alpha = jnp.exp2(m - m_new)
            p = jnp.exp2(s - m_new)
            stats[n] = (m_new, alpha * l, alpha, p.astype(bf16))
        out = list(carry)
        for n in live:
            m_new, l_scaled, alpha, p = stats[n]
            pv = jnp.dot(vt_scr[j, head_of(n)], p, preferred_element_type=f32)
            acc_scr[n] = alpha * acc_scr[n] + pv[:HEAD_W, :]
            out[n] = (m_new, l_scaled + pv[HEAD_W:HEAD_W + 1, :])
        return tuple(out)

    acc_scr[...] = jnp.zeros_like(acc_scr)
    init = tuple((jnp.full((1, TQ), NEG_INF, f32), jnp.zeros((1, TQ), f32)) for _ in qs)
    first = qi * qtiles
    carry = lax.fori_loop(0, first, lambda j, c: step(j, c, ("full",) * qtiles), init)
    for u in range(qtiles):
        modes = tuple("skip" if t < u else ("diag" if t == u else "full") for t in range(qtiles))
        carry = step(first + u, carry, modes)
    carry = tuple((m, l, acc_scr[n]) for n, (m, l) in enumerate(carry))

    if ncomp == 2:
        ll = lam_ref[...]
        s1 = jnp.sum(ll[0:1, :] * ll[1:2, :], axis=-1, keepdims=True)
        s2 = jnp.sum(ll[2:3, :] * ll[3:4, :], axis=-1, keepdims=True)
        lam = jnp.exp(s1) - jnp.exp(s2) + LAM_INIT
    for t in range(qtiles):
        for h in range(heads):
            n0 = (t * heads + h) * ncomp
            if ncomp == 2:
                (_, l0, a0), (_, l1, a1) = carry[n0], carry[n0 + 1]
                o = a0 / l0 - lam * (a1 / l1)
                ms = jnp.mean(o * o, axis=0, keepdims=True)
                o = o * lax.rsqrt(ms + NORM_EPS) * g_ref[...] * (1.0 - LAM_INIT)
            else:
                _, l0, a0 = carry[n0]
                o = a0 / l0
            o_ref[t * TQ:(t + 1) * TQ, h * HEAD_W:(h + 1) * HEAD_W] = o.T.astype(bf16)


def _attention(q_arr, k_arr, v_arr, q_blk0, k_blk0, v_blk0, heads, qtiles, extra=()):
    ncomp = 2 if extra else 1
    tq = qtiles * TQ
    nq = SEQ // tq
    w = heads * HEAD_W
    qb, kb, vb = q_blk0 // heads, k_blk0 // heads, v_blk0 // heads
    in_specs = [pl.BlockSpec((tq, w), lambda b, h, i: (b * nq + i, qb + h)),
                pl.BlockSpec((SEQ, w), lambda b, h, i: (b, kb + h)),
                pl.BlockSpec((SEQ, w), lambda b, h, i: (b, vb + h))]
    for a in extra:
        in_specs.append(pl.BlockSpec(a.shape, lambda b, h, i: (0, 0)))
    return pl.pallas_call(
        functools.partial(_attn_kernel, ncomp=ncomp, heads=heads, qtiles=qtiles),
        grid=(BATCH, HEADS // heads, nq),
        in_specs=in_specs,
        out_specs=pl.BlockSpec((tq, w), lambda b, h, i: (b * nq + i, h)),
        out_shape=jax.ShapeDtypeStruct((N_TOK, HEADS * HEAD_W), bf16),
        scratch_shapes=[pltpu.VMEM((qtiles * heads * ncomp, HEAD_W, TQ), f32),
                        pltpu.VMEM((SEQ // TQ, heads, VT_ROWS, TQ), bf16)],
        compiler_params=_cparams(("arbitrary", "arbitrary", "arbitrary")),
        name="diff_attn" if extra else "mla_attn",
    )(q_arr, k_arr, v_arr, *extra)


RT_GROUP_ROW = 0
RT_EXPERT_ROW = 8


def _route(hf, w_ref, b_ref, eid_ref, cw_ref):
    hf_hi = hf.astype(bf16)
    hf_lo = (hf - hf_hi.astype(f32)).astype(bf16)
    a = jnp.dot(hf_hi, w_ref[...], preferred_element_type=f32)
    b = jnp.dot(hf_lo, w_ref[:, :LANES], preferred_element_type=f32)
    logits = a[:, :LANES] + (a[:, LANES:] + b) + b_ref[...]
    lt = logits.T
    tm = lt.shape[1]
    ridx = lax.broadcasted_iota(i32, (EXPERTS_PER_GROUP, tm), 0).astype(f32)

    def top1(v):
        vmax = jnp.max(v, axis=0, keepdims=True)
        idx = jnp.min(jnp.where(v == vmax, ridx, float(EXPERTS_PER_GROUP)), axis=0, keepdims=True)
        return vmax, idx

    gl = lt[RT_GROUP_ROW:RT_GROUP_ROW + 8, :]
    gmax, gsel = top1(gl)
    p_g = 1.0 / jnp.sum(jnp.exp(gl - gmax), axis=0, keepdims=True)
    el = jnp.zeros((EXPERTS_PER_GROUP, tm), f32)
    for g in range(N_GROUPS):
        r0 = RT_EXPERT_ROW + g * EXPERTS_PER_GROUP
        el = jnp.where(gsel == float(g), lt[r0:r0 + EXPERTS_PER_GROUP, :], el)
    e1, i1 = top1(el)
    e2, i2 = top1(jnp.where(ridx == i1, -jnp.inf, el))
    r = jnp.exp(e2 - e1)
    w1 = 1.0 / (1.0 + r)
    base = gsel * EXPERTS_PER_GROUP
    eid_ref[0:1, :] = (base + i1).astype(i32)
    eid_ref[1:2, :] = (base + i2).astype(i32)
    cw_ref[0:1, :] = p_g * w1
    cw_ref[1:2, :] = p_g * (r * w1)


def _merge_kernel(oa_ref, ob_ref, ga_ref, gb_ref, x_ref, wa_ref, wb_ref, wo_ref, gf_ref, wr_ref, br_ref,
                  x1_ref, hf_ref, eid_ref, cw_ref):
    ya = jnp.dot(oa_ref[...], wa_ref[...], preferred_element_type=f32)
    yb = jnp.dot(ob_ref[...], wb_ref[...], preferred_element_type=f32)
    mixed = (jax.nn.sigmoid(ga_ref[...].astype(f32)) * ya
             + jax.nn.sigmoid(gb_ref[...].astype(f32)) * yb)
    x1 = x_ref[...] + jnp.dot(mixed.astype(bf16), wo_ref[...], preferred_element_type=f32)
    x1_ref[...] = x1
    hf = _rms(x1, gf_ref[...])
    _slab_store(hf_ref, (), hf)
    _route(hf, wr_ref, br_ref, eid_ref, cw_ref)


def _merge(oa, ob, proj, x2d, wa, wb, wo, gf, w_rt, b_rt):
    tm = TM_MERGE
    w = HEADS * HEAD_W
    const = lambda shape: pl.BlockSpec(shape, lambda i: (0, 0), pipeline_mode=pl.Buffered(1))
    return pl.pallas_call(
        _merge_kernel,
        grid=(N_TOK // tm,),
        in_specs=[pl.BlockSpec((tm, w), lambda i: (i, 0)),
                  pl.BlockSpec((tm, w), lambda i: (i, 0)),
                  pl.BlockSpec((tm, D_MODEL), lambda i: (i, COL_GA // D_MODEL)),
                  pl.BlockSpec((tm, D_MODEL), lambda i: (i, COL_GB // D_MODEL)),
                  pl.BlockSpec((tm, D_MODEL), lambda i: (i, 0)),
                  const((w, D_MODEL)), const((w, D_MODEL)), const((D_MODEL, D_MODEL)),
                  const((1, D_MODEL)), const((D_MODEL, 2 * LANES)), const((1, LANES))],
        out_specs=[pl.BlockSpec((tm, D_MODEL), lambda i: (i, 0)),
                   pl.BlockSpec((tm * SLAB, LANES), lambda i: (i, 0)),
                   pl.BlockSpec((TOP_K, tm), lambda i: (0, i)),
                   pl.BlockSpec((TOP_K, tm), lambda i: (0, i))],
        out_shape=[jax.ShapeDtypeStruct((N_TOK, D_MODEL), f32),
                   jax.ShapeDtypeStruct((N_TOK * SLAB, LANES), f32),
                   jax.ShapeDtypeStruct((TOP_K, N_TOK), i32),
                   jax.ShapeDtypeStruct((TOP_K, N_TOK), f32)],
        compiler_params=_cparams(("arbitrary",)),
        name="merge_route",
    )(oa, ob, proj, proj, x2d, wa, wb, wo, gf, w_rt, b_rt)


PLAN_N = 128


def _plan_kernel(eid_ref, dest_ref, ends_ref):
    eid = eid_ref[...]
    r = lax.broadcasted_iota(i32, (PLAN_N, PLAN_N), 0)
    c = lax.broadcasted_iota(i32, (PLAN_N, PLAN_N), 1)
    upper = (r < c).astype(bf16)
    lower = (c < r).astype(bf16)
    ones = jnp.ones((PLAN_N, PLAN_N), bf16)
    lane = lax.broadcasted_iota(i32, (1, LANES), 1)
    dest = jnp.zeros((PLAN_N, PLAN_N), f32)
    ends = jnp.zeros((1, LANES), f32)
    start = jnp.zeros((1, 1), f32)
    for e in range(N_EXPERTS):
        m = (eid == e).astype(f32)
        mb = m.astype(bf16)
        within = jnp.dot(mb, upper, preferred_element_type=f32)
        rowtot = jnp.dot(mb, ones, preferred_element_type=f32)
        rowpre = jnp.dot(lower, rowtot.astype(bf16), preferred_element_type=f32)
        cnt = jnp.sum(jnp.sum(m, axis=1, keepdims=True), axis=0, keepdims=True)
        dest = dest + m * (start + within + rowpre)
        start = start + jnp.floor((cnt + (ROW_BLK - 1)) * (1.0 / ROW_BLK)) * ROW_BLK
        ends = jnp.where(lane == e, start, ends)
    dest_ref[...] = dest.astype(i32)
    ends_ref[...] = ends.astype(i32)


def _plan(eid):
    whole = lambda shape: pl.BlockSpec(shape, lambda: (0, 0))
    return pl.pallas_call(
        _plan_kernel,
        in_specs=[whole((PLAN_N, PLAN_N))],
        out_specs=[whole((PLAN_N, PLAN_N)), whole((1, LANES))],
        out_shape=[jax.ShapeDtypeStruct((PLAN_N, PLAN_N), i32),
                   jax.ShapeDtypeStruct((1, LANES), i32)],
        name="moe_plan",
    )(eid.reshape(PLAN_N, PLAN_N))


def _dispatch_kernel(ends_ref, dest_ref, hf_ref, xs_hbm, zbuf, sem):
    i = pl.program_id(0)

    def tail_copy(e):
        pend = ends_ref[e]
        prev = jnp.where(e > 0, ends_ref[jnp.maximum(e - 1, 0)], 0)
        row0 = pl.multiple_of(jnp.maximum(pend - ROW_BLK, 0), ROW_BLK)
        return pend > prev, pltpu.make_async_copy(zbuf, xs_hbm.at[_slab_rows(row0, ROW_BLK), :], sem)

    @pl.when(i == 0)
    def _():
        zbuf[...] = jnp.zeros_like(zbuf)

        def zstart(e, _):
            used, cp = tail_copy(e)

            @pl.when(used)
            def _():
                cp.start()
            return 0

        def zwait(e, _):
            used, cp = tail_copy(e)

            @pl.when(used)
            def _():
                cp.wait()
            return 0

        def unused_copy(b):
            row0 = pl.multiple_of(b * ROW_BLK, ROW_BLK)
            return pltpu.make_async_copy(zbuf, xs_hbm.at[_slab_rows(row0, ROW_BLK), :], sem)

        def ustart(b, _):
            unused_copy(b).start()
            return 0

        def uwait(b, _):
            unused_copy(b).wait()
            return 0

        n_used = ends_ref[N_EXPERTS - 1] // ROW_BLK
        lax.fori_loop(0, N_EXPERTS, zstart, 0)
        lax.fori_loop(n_used, N_BLKS, ustart, 0)
        lax.fori_loop(0, N_EXPERTS, zwait, 0)
        lax.fori_loop(n_used, N_BLKS, uwait, 0)

    base = i * TT_DISP

    def row_copy(t, k):
        d = dest_ref[k, base + t]
        return pltpu.make_async_copy(hf_ref.at[_slab_rows(t, 1), :], xs_hbm.at[_slab_rows(d, 1), :], sem)

    def issue(t, _):
        for k in range(TOP_K):
            row_copy(t, k).start(priority=k)
        return 0

    def drain(t, _):
        for k in range(TOP_K):
            row_copy(t, k).wait()
        return 0

    lax.fori_loop(0, TT_DISP, issue, 0, unroll=ISSUE_UNROLL)
    lax.fori_loop(0, TT_DISP, drain, 0, unroll=ISSUE_UNROLL)


def _dispatch(ends, dest, hf):
    return pl.pallas_call(
        _dispatch_kernel,
        grid_spec=pltpu.PrefetchScalarGridSpec(
            num_scalar_prefetch=2,
            grid=(N_TOK // TT_DISP,),
            in_specs=[pl.BlockSpec((TT_DISP * SLAB, LANES), lambda i, e, d: (i, 0))],
            out_specs=pl.BlockSpec(memory_space=pl.ANY),
            scratch_shapes=[pltpu.VMEM((ROW_BLK * SLAB, LANES), f32),
                            pltpu.SemaphoreType.DMA(())]),
        out_shape=jax.ShapeDtypeStruct((N_ROWS * SLAB, LANES), f32),
        compiler_params=_cparams(("arbitrary",)),
        name="moe_dispatch",
    )(ends, dest, hf)


W_RING = 3
X_RING = 6
Y_RING = 3


def _ffn_kernel(ends_ref, wg_hbm, wu_hbm, wd_hbm, xs_hbm, y_hbm,
                wg_f, wu_f, wd_f, wg_s, wu_s, wd_s, xbuf, ybuf, sem_w, sem_x, sem_y):
    n_used = ends_ref[N_EXPERTS - 1] // ROW_BLK

    def w_copies(ex):
        slot = ex % W_RING
        return [pltpu.make_async_copy(hbm.at[ex], buf.at[slot], sem_w.at[i, slot])
                for i, (hbm, buf) in enumerate(((wg_hbm, wg_f), (wu_hbm, wu_f), (wd_hbm, wd_f)))]

    def advance(e_from, e_to):
        def one(ex, _):
            for cp in w_copies(ex):
                cp.wait()

            @pl.when(ex + (W_RING - 1) < N_EXPERTS)
            def _():
                for cp in w_copies(ex + (W_RING - 1)):
                    cp.start()
            return 0

        lax.fori_loop(e_from + 1, e_to + 1, one, 0)

    def x_copy(b):
        slot = b % X_RING
        return pltpu.make_async_copy(xs_hbm.at[_slab_rows(b * ROW_BLK, ROW_BLK), :], xbuf.at[slot],
                                     sem_x.at[slot])

    def y_copy(b):
        slot = b % Y_RING
        return pltpu.make_async_copy(ybuf.at[slot], y_hbm.at[_slab_rows(b * ROW_BLK, ROW_BLK), :],
                                     sem_y.at[slot])

    for ex in range(W_RING - 1):
        for cp in w_copies(ex):
            cp.start()
    for b0 in range(X_RING - 1):
        @pl.when(b0 < n_used)
        def _():
            x_copy(b0).start()

    def body(b, e):
        @pl.when(b + (X_RING - 1) < n_used)
        def _():
            x_copy(b + (X_RING - 1)).start()

        e_new = lax.while_loop(lambda ex: b * ROW_BLK >= ends_ref[ex], lambda ex: ex + 1,
                               jnp.maximum(e, 0))

        @pl.when(e_new != e)
        def _():
            advance(e, e_new)
            wslot = e_new % W_RING
            wg_s[...] = wg_f[wslot].astype(bf16)
            wu_s[...] = wu_f[wslot].astype(bf16)
            wd_s[...] = wd_f[wslot].astype(bf16)

        x_copy(b).wait()

        @pl.when(b >= Y_RING)
        def _():
            y_copy(b - Y_RING).wait()

        x = _slab_load(xbuf, (b % X_RING,), ROW_BLK).astype(bf16)
        h1 = jnp.dot(x, wg_s[...], preferred_element_type=f32)
        h2 = jnp.dot(x, wu_s[...], preferred_element_type=f32)
        a = (h1 * jax.nn.sigmoid(h1) * h2).astype(bf16)
        _slab_store(ybuf, (b % Y_RING,), jnp.dot(a, wd_s[...], preferred_element_type=f32))
        y_copy(b).start()
        return e_new

    e_last = lax.fori_loop(0, n_used, body, jnp.int32(-1))
    advance(e_last, jnp.int32(N_EXPERTS - 1))

    for back in range(Y_RING, 0, -1):
        @pl.when(n_used >= back)
        def _():
            y_copy(n_used - back).wait()

    ybuf[0] = jnp.zeros((ROW_BLK * SLAB, LANES), f32)

    def unused_copy(b):
        return pltpu.make_async_copy(ybuf.at[0], y_hbm.at[_slab_rows(b * ROW_BLK, ROW_BLK), :],
                                     sem_y.at[0])

    def ustart(b, _):
        unused_copy(b).start()
        return 0

    def uwait(b, _):
        unused_copy(b).wait()
        return 0

    lax.fori_loop(n_used, N_BLKS, ustart, 0)
    lax.fori_loop(n_used, N_BLKS, uwait, 0)


def _expert_ffn(ends, xs, wg, wu, wd):
    return pl.pallas_call(
        _ffn_kernel,
        grid_spec=pltpu.PrefetchScalarGridSpec(
            num_scalar_prefetch=1,
            grid=(1,),
            in_specs=[pl.BlockSpec(memory_space=pl.ANY)] * 4,
            out_specs=pl.BlockSpec(memory_space=pl.ANY),
            scratch_shapes=[pltpu.VMEM((W_RING, D_MODEL, EXPERT_FF), f32),
                            pltpu.VMEM((W_RING, D_MODEL, EXPERT_FF), f32),
                            pltpu.VMEM((W_RING, EXPERT_FF, D_MODEL), f32),
                            pltpu.VMEM((D_MODEL, EXPERT_FF), bf16),
                            pltpu.VMEM((D_MODEL, EXPERT_FF), bf16),
                            pltpu.VMEM((EXPERT_FF, D_MODEL), bf16),
                            pltpu.VMEM((X_RING, ROW_BLK * SLAB, LANES), f32),
                            pltpu.VMEM((Y_RING, ROW_BLK * SLAB, LANES), f32),
                            pltpu.SemaphoreType.DMA((3, W_RING)),
                            pltpu.SemaphoreType.DMA((X_RING,)),
                            pltpu.SemaphoreType.DMA((Y_RING,))]),
        out_shape=jax.ShapeDtypeStruct((N_ROWS * SLAB, LANES), f32),
        compiler_params=_cparams(("arbitrary",)),
        name="expert_ffn",
    )(ends, wg, wu, wd, xs)


def _combine_kernel(dest_ref, x_ref, cw_ref, g_ref, y_hbm, o_ref, ybuf, sem):
    i = pl.program_id(0)
    n_steps = pl.num_programs(0)

    def row_copy(step, t, k):
        slot = step % 2
        d = dest_ref[k, step * TM_COMB + t]
        return pltpu.make_async_copy(y_hbm.at[_slab_rows(d, 1), :],
                                     ybuf.at[slot, k, _slab_rows(t, 1), :], sem.at[slot])

    def issue(step):
        def one(t, _):
            for k in range(TOP_K):
                row_copy(step, t, k).start()
            return 0
        lax.fori_loop(0, TM_COMB, one, 0, unroll=ISSUE_UNROLL)

    def drain(step):
        def one(t, _):
            for k in range(TOP_K):
                row_copy(step, t, k).wait()
            return 0
        lax.fori_loop(0, TM_COMB, one, 0, unroll=ISSUE_UNROLL)

    @pl.when(i == 0)
    def _():
        issue(i)

    @pl.when(i + 1 < n_steps)
    def _():
        issue(i + 1)

    drain(i)
    slot = i % 2
    w = cw_ref[...]
    x = (x_ref[...] + w[:, 0:1] * _slab_load(ybuf, (slot, 0), TM_COMB)
         + w[:, 1:2] * _slab_load(ybuf, (slot, 1), TM_COMB))
    o_ref[...] = _rms(x, g_ref[...])


def _combine(dest, x1, cw_t, g, y):
    tm = TM_COMB
    return pl.pallas_call(
        _combine_kernel,
        grid_spec=pltpu.PrefetchScalarGridSpec(
            num_scalar_prefetch=1,
            grid=(N_TOK // tm,),
            in_specs=[pl.BlockSpec((tm, D_MODEL), lambda i, d: (i, 0)),
                      pl.BlockSpec((tm, TOP_K), lambda i, d: (i, 0)),
                      pl.BlockSpec((1, D_MODEL), lambda i, d: (0, 0)),
                      pl.BlockSpec(memory_space=pl.ANY)],
            out_specs=pl.BlockSpec((tm, D_MODEL), lambda i, d: (i, 0)),
            scratch_shapes=[pltpu.VMEM((2, TOP_K, tm * SLAB, LANES), f32),
                            pltpu.SemaphoreType.DMA((2,))]),
        out_shape=jax.ShapeDtypeStruct((N_TOK, D_MODEL), f32),
        compiler_params=_cparams(("arbitrary",)),
        name="moe_combine",
    )(dest, x1, cw_t, g, y)


TR_RELAYOUT = 256
KR_TILE = COL_KR // TR_RELAYOUT


def _w_in_relayout_kernel(w_ref, o_ref):
    i = pl.program_id(0)

    @pl.when(i < COL_V // TR_RELAYOUT)
    def _():
        q = ROPE_NA
        for g in range(TR_RELAYOUT // LANES):
            r = g * LANES
            o_ref[r:r + q, :] = w_ref[r:r + q, :].astype(bf16)
            o_ref[r + q:r + 2 * q, :] = w_ref[r + 2 * q:r + 3 * q, :].astype(bf16)
            o_ref[r + 2 * q:r + 3 * q, :] = w_ref[r + q:r + 2 * q, :].astype(bf16)
            o_ref[r + 3 * q:r + 4 * q, :] = w_ref[r + 3 * q:r + 4 * q, :].astype(bf16)

    @pl.when((i >= COL_V // TR_RELAYOUT) & (i != KR_TILE))
    def _():
        o_ref[...] = w_ref[...].astype(bf16)

    @pl.when(i == KR_TILE)
    def _():
        o_ref[...] = jnp.zeros_like(o_ref)
        o_ref[KR_LANE:KR_LANE + MLA_ROPE, :] = w_ref[0:MLA_ROPE, :].astype(bf16)


def _prep_w_in(w_in_t):
    n_src, d = w_in_t.shape
    gate_src = COL_KR + MLA_ROPE

    gate_tile = COL_GA // TR_RELAYOUT
    sub = 8
    shift = (COL_GA - gate_src) // sub

    def src_row(i):
        return ((TR_RELAYOUT // sub) * i - shift * (i // gate_tile)) * sub

    return pl.pallas_call(
        _w_in_relayout_kernel,
        grid=(PROJ_W // TR_RELAYOUT,),
        in_specs=[pl.BlockSpec((pl.Element(TR_RELAYOUT), pl.Element(d)), lambda i: (src_row(i), 0))],
        out_specs=pl.BlockSpec((TR_RELAYOUT, d), lambda i: (i, 0)),
        out_shape=jax.ShapeDtypeStruct((PROJ_W, d), bf16),
        compiler_params=_cparams(("arbitrary",)),
        name="w_in_relayout",
    )(w_in_t)


def _prep_w_uq(w_uq):
    w = w_uq.reshape(MLA_Q_RANK, HEADS, MLA_NOPE + MLA_ROPE)
    w = jnp.pad(w, ((0, 0), (0, 0), (0, HEAD_W - MLA_NOPE - MLA_ROPE)))
    return w.reshape(MLA_Q_RANK, HEADS * HEAD_W).astype(bf16)


def _prep_w_ukv(w_ukv):
    w = w_ukv.reshape(MLA_KV_RANK, HEADS, MLA_NOPE + HEAD_W)
    wk = jnp.pad(w[:, :, :MLA_NOPE], ((0, 0), (0, 0), (0, HEAD_W - MLA_NOPE)))
    wv = w[:, :, MLA_NOPE:]
    return jnp.concatenate([wk.reshape(MLA_KV_RANK, -1), wv.reshape(MLA_KV_RANK, -1)], axis=1).astype(bf16)


def _prep_router(w_group, b_group, w_router, b_router):
    z = lambda n: jnp.zeros((D_MODEL, n), f32)
    w = jnp.concatenate([w_group, z(RT_EXPERT_ROW - N_GROUPS), w_router,
                         z(LANES - RT_EXPERT_ROW - N_EXPERTS)], axis=1)
    b = jnp.concatenate([b_group, jnp.full((RT_EXPERT_ROW - N_GROUPS,), NEG_INF, f32), b_router,
                         jnp.zeros((LANES - RT_EXPERT_ROW - N_EXPERTS,), f32)])[None, :]
    w_hi = w.astype(bf16)
    w_lo = (w - w_hi.astype(f32)).astype(bf16)
    return jnp.concatenate([w_hi, w_lo], axis=1), b


def kernel(x, positions, attn_norm, w_in, da_lambda, da_subln, mla_q_norm, mla_w_uq, mla_kv_norm, mla_w_ukv, w_branch_a, w_branch_b, w_out, ffn_norm, w_group, b_group, w_router, b_router, w_exp_gate, w_exp_up, w_exp_down, final_norm):
    x2d = x.reshape(N_TOK, D_MODEL)
    cos_t, sin_t = _rope_tables(positions)

    proj = _in_proj(x2d, attn_norm[0][None, :], _prep_w_in(w_in[0].T), cos_t, sin_t)
    qb, kb, vb = _mla_prep(proj, mla_q_norm[0][None, :], mla_kv_norm[0][None, :],
                           _prep_w_uq(mla_w_uq[0]), _prep_w_ukv(mla_w_ukv[0]), cos_t, sin_t)
    oa = _attention(proj, proj, proj, COL_Q // HEAD_W, COL_K // HEAD_W, COL_V // HEAD_W,
                    heads=4, qtiles=4, extra=(da_lambda[0], da_subln[0][:, None]))
    ob = _attention(qb, kb, vb, 0, 0, 0, heads=4, qtiles=8)
    w_rt, b_rt = _prep_router(w_group[0], b_group[0], w_router[0], b_router[0])
    x1, hf, eid, cw = _merge(oa, ob, proj, x2d, w_branch_a[0].astype(bf16), w_branch_b[0].astype(bf16),
                             w_out[0].astype(bf16), ffn_norm[0][None, :], w_rt, b_rt)

    dest, ends = _plan(eid)
    dest = dest.reshape(TOP_K, N_TOK)
    ends = ends[0, :N_EXPERTS]
    xs = _dispatch(ends, dest, hf)
    y = _expert_ffn(ends, xs, w_exp_gate[0], w_exp_up[0], w_exp_down[0])
    out = _combine(dest, x1, cw.T, final_norm[None, :], y)
    return out.reshape(BATCH, SEQ, D_MODEL)
```
